```python
import jax, jax.numpy as jnp
from jax import lax
import numpy as np

D_MODEL = 4096
BATCH = 2
SEQ = 4096
DEPTH = 1
DEC_BATCH = 32
DEC_SEQ = 1
PAST_LEN = 8192
PAGE_SIZE = 128

N_HEADS = 16
N_KV_HEADS = 4
HEAD_DIM = 128
ATTN_WIDTH = N_HEADS * HEAD_DIM
KV_WIDTH = N_KV_HEADS * HEAD_DIM
IDX_HEADS = 32
IDX_DIM = 128
TOPK_MAX = 256
Q_BLOCK = 128
ROPE_THETA = 10000.0
LRU_WIDTH = D_MODEL // 2
LRU_BLOCKS = 16
LRU_BLOCK_DIM = LRU_WIDTH // LRU_BLOCKS
CONV_WIDTH = 4
LRU_C = 8.0
N_EXPERTS = 64
EXPERT_DIM = D_MODEL // 4
TOP_K = 8
N_GROUPS = 8
TOPK_GROUPS = 4
ROUTE_SCALE = 2.5
MOE_BLOCK = 256
LN_EPS = 1e-5
ALPHA = (2.0 * DEPTH) ** 0.25
BETA = (8.0 * DEPTH) ** -0.25
IN_SPLITS = (ATTN_WIDTH, KV_WIDTH, KV_WIDTH, IDX_HEADS * IDX_DIM, IDX_DIM, IDX_HEADS, LRU_WIDTH, D_MODEL, D_MODEL)
IN_WIDTH = sum(IN_SPLITS)

kernel_name = 'dsa_rglru_moe_hybrid_step'


def layer_norm(x, g, b):
    xf = x.astype(jnp.float32)
    mu = jnp.mean(xf, -1, keepdims=True)
    xc = xf - mu
    var = jnp.mean(xc * xc, -1, keepdims=True)
    return (xc * lax.rsqrt(var + LN_EPS) * g.astype(jnp.float32) + b.astype(jnp.float32)).astype(x.dtype)


def rope(x, pos):
    half = x.shape[-1] // 2
    inv = ROPE_THETA ** (-jnp.arange(half, dtype=jnp.float32) / half)
    ang = pos.astype(jnp.float32)[:, None] * inv[None, :]
    cos = jnp.cos(ang)[None, :, None, :]
    sin = jnp.sin(ang)[None, :, None, :]
    xf = x.astype(jnp.float32)
    x1, x2 = xf[..., :half], xf[..., half:]
    return jnp.concatenate([x1 * cos - x2 * sin, x2 * cos + x1 * sin], -1).astype(x.dtype)


def in_project(x, w_in, pos, ik_g, ik_b):
    n, t, _ = x.shape
    z = jnp.einsum('ntd,de->nte', x, w_in)
    q, k, v, qi, ki, wi, xl, ga, gb = jnp.split(z, np.cumsum(IN_SPLITS)[:-1].tolist(), axis=-1)
    q = rope(q.reshape(n, t, N_HEADS, HEAD_DIM), pos)
    k = rope(k.reshape(n, t, N_KV_HEADS, HEAD_DIM), pos)
    kv = jnp.stack([k, v.reshape(n, t, N_KV_HEADS, HEAD_DIM)], axis=2)
    qi = rope(qi.reshape(n, t, IDX_HEADS, IDX_DIM), pos)
    ki = rope(layer_norm(ki, ik_g, ik_b)[:, :, None, :], pos)[:, :, 0, :]
    wi = wi * IDX_HEADS ** -0.5
    return q, kv, qi, ki, wi, xl, ga, gb


def indexer_scores(qi, ki, wi):
    s = jax.nn.relu(jnp.einsum('nthd,nsd->nths', qi, ki).astype(jnp.float32) * IDX_DIM ** -0.5)
    return jnp.einsum('nths,nth->nts', s, wi.astype(jnp.float32))


def sparse_attend(q, kv_sel, valid):
    n, t, h, d = q.shape
    qg = q.reshape(n, t, N_KV_HEADS, h // N_KV_HEADS, d)
    k_sel, v_sel = kv_sel[:, :, :, 0], kv_sel[:, :, :, 1]
    s = jnp.einsum('ntcgd,ntjcd->ntcgj', qg, k_sel).astype(jnp.float32) * d ** -0.5
    s = jnp.where(valid[:, :, None, None, :], s, -jnp.inf)
    p = jax.nn.softmax(s, axis=-1).astype(v_sel.dtype)
    o = jnp.einsum('ntcgj,ntjcd->ntcgd', p, v_sel)
    return o.reshape(n, t, h * d)


def dsa_prompt(q, kv, qi, ki, wi):
    n, s_len = q.shape[:2]
    topk = min(TOPK_MAX, s_len // 4)
    kpos = jnp.arange(s_len)
    gather_rows = jax.vmap(lambda rows, idx: rows[idx])

    def one_block(start):
        sl = lambda a: lax.dynamic_slice_in_dim(a, start, Q_BLOCK, axis=1)
        qpos = start + jnp.arange(Q_BLOCK)
        causal = kpos[None, None, :] <= qpos[None, :, None]
        sc = jnp.where(causal, indexer_scores(sl(qi), ki, sl(wi)), -jnp.inf)
        _, sel = lax.top_k(sc, topk)
        valid = sel <= qpos[None, :, None]
        return sparse_attend(sl(q), gather_rows(kv, sel), valid)

    o = lax.map(one_block, jnp.arange(0, s_len, Q_BLOCK))
    return jnp.moveaxis(o, 0, 1).reshape(n, s_len, ATTN_WIDTH)


def dsa_sample(q, kv_new, qi, ki_new, wi, qpos, cache_kv, cache_idx_k, page_table):
    n, t = q.shape[:2]
    past = page_table.shape[1] * PAGE_SIZE
    l_vis = past + t
    topk = min(TOPK_MAX, l_vis // 4)
    ki_past = cache_idx_k[page_table].reshape(n, past, IDX_DIM).astype(ki_new.dtype)
    ki_all = jnp.concatenate([ki_past, ki_new], axis=1)
    causal = jnp.arange(l_vis)[None, None, :] <= qpos[None, :, None]
    sc = jnp.where(causal, indexer_scores(qi, ki_all, wi), -jnp.inf)
    _, sel = lax.top_k(sc, topk)
    valid = sel <= qpos[None, :, None]
    bidx = jnp.arange(n)[:, None, None]
    ps = jnp.clip(sel, 0, past - 1)
    phys = page_table[bidx, ps // PAGE_SIZE]
    rows_past = cache_kv[phys, ps % PAGE_SIZE].astype(kv_new.dtype)
    rows_new = kv_new[bidx, jnp.clip(sel - past, 0, t - 1)]
    kv_sel = jnp.where((sel < past)[..., None, None, None], rows_past, rows_new)
    return sparse_attend(q, kv_sel, valid)


def rglru_branch(xl, conv_hist, h0, conv_w, conv_b, w_a, b_a, w_x, b_x, lam):
    n, t, w = xl.shape
    xp = jnp.concatenate([conv_hist.astype(xl.dtype), xl], axis=1)
    xc = conv_b + xp[:, 0:t] * conv_w[0]
    for j in range(1, CONV_WIDTH):
        xc = xc + xp[:, j:j + t] * conv_w[j]
    new_hist = xp[:, t:]
    xb = xc.reshape(n, t, LRU_BLOCKS, LRU_BLOCK_DIM)
    r = jax.nn.sigmoid(jnp.einsum('ntki,kij->ntkj', xb, w_a).reshape(n, t, w) + b_a)
    i = jax.nn.sigmoid(jnp.einsum('ntki,kij->ntkj', xb, w_x).reshape(n, t, w) + b_x)
    log_a = (-LRU_C * r.astype(jnp.float32)) * jax.nn.softplus(-lam.astype(jnp.float32))
    a = jnp.exp(log_a)
    u = jnp.sqrt(-jnp.expm1(2.0 * log_a)) * (i * xc).astype(jnp.float32)

    def step(h, au):
        a_t, u_t = au
        h = a_t * h + u_t
        return h, h

    h_last, hs = lax.scan(step, h0.astype(jnp.float32), (jnp.moveaxis(a, 1, 0), jnp.moveaxis(u, 1, 0)))
    return jnp.moveaxis(hs, 0, 1).astype(xl.dtype), new_hist, h_last.astype(h0.dtype)


def moe(x, w_router, b_router, w_gate, w_up, w_down, ws_gate, ws_up, ws_down):
    n, t, d = x.shape
    n_tok = n * t
    blk = min(MOE_BLOCK, n_tok)
    pad = (-n_tok) % blk
    xt = jnp.pad(x.reshape(n_tok, d), ((0, pad), (0, 0))).reshape(-1, blk, d)

    def one_block(xb):
        s = jax.nn.sigmoid(jnp.einsum('td,de->te', xb, w_router).astype(jnp.float32))
        choice = s + b_router.astype(jnp.float32)
        grp_score = lax.top_k(choice.reshape(blk, N_GROUPS, N_EXPERTS // N_GROUPS), 2)[0].sum(-1)
        _, grp_sel = lax.top_k(grp_score, TOPK_GROUPS)
        grp_mask = jax.nn.one_hot(grp_sel, N_GROUPS, dtype=jnp.float32).sum(1) > 0
        exp_mask = jnp.repeat(grp_mask, N_EXPERTS // N_GROUPS, axis=-1)
        _, sel = lax.top_k(jnp.where(exp_mask, choice, -jnp.inf), TOP_K)
        wk = jnp.take_along_axis(s, sel, axis=-1)
        wk = ROUTE_SCALE * wk / jnp.sum(wk, -1, keepdims=True)
        gates = jnp.einsum('tk,tke->te', wk, jax.nn.one_hot(sel, N_EXPERTS, dtype=jnp.float32)).astype(xb.dtype)
        hid = jax.nn.silu(jnp.einsum('td,edf->tef', xb, w_gate)) * jnp.einsum('td,edf->tef', xb, w_up)
        routed = jnp.einsum('tef,efd->td', hid * gates[:, :, None], w_down)
        shared = jnp.einsum('tf,fd->td', jax.nn.silu(xb @ ws_gate) * (xb @ ws_up), ws_down)
        return routed + shared

    y = lax.map(one_block, xt).reshape(-1, d)[:n_tok]
    return y.reshape(n, t, d)


def block_tail(x, o_attn, h_lru, ga, gb, w_o_attn, w_o_lru, w_out, ln1_g, ln1_b,
               w_router, b_router, w_gate, w_up, w_down, ws_gate, ws_up, ws_down, ln2_g, ln2_b):
    merged = (jax.nn.sigmoid(ga) * jnp.einsum('nta,ad->ntd', o_attn, w_o_attn)
              + jax.nn.sigmoid(gb) * jnp.einsum('ntw,wd->ntd', h_lru, w_o_lru))
    x = layer_norm(ALPHA * x + jnp.einsum('nte,ed->ntd', merged, w_out), ln1_g, ln1_b)
    f = moe(x, w_router, b_router, w_gate, w_up, w_down, ws_gate, ws_up, ws_down)
    return layer_norm(ALPHA * x + f, ln2_g, ln2_b)


def setup_inputs(seed: int = 0) -> dict:
    key = jax.random.key(seed)
    ks = iter(jax.random.split(key, 48))

    def nrm(shape, scale=1.0):
        return scale * jax.random.normal(next(ks), shape, jnp.float32)

    def gain(shape):
        return 1.0 + nrm(shape, 0.05)

    n_pages = PAST_LEN // PAGE_SIZE
    n_used = DEC_BATCH * n_pages
    n_phys = n_used + n_used // 4 + 1
    page_table = jax.random.permutation(next(ks), n_phys)[:n_used].reshape(DEC_BATCH, n_pages).astype(jnp.int32)
    a_pow_c = jax.random.uniform(next(ks), (DEPTH, LRU_WIDTH), jnp.float32, 0.9, 0.999)
    a0 = a_pow_c ** (1.0 / LRU_C)
    lru_lambda = jnp.log(a0) - jnp.log1p(-a0)
    return {
        'x_prompt': nrm((BATCH, SEQ, D_MODEL)),
        'x_sample': nrm((DEC_BATCH, DEC_SEQ, D_MODEL)),
        'cache_kv': nrm((DEPTH, n_phys, PAGE_SIZE, 2, N_KV_HEADS, HEAD_DIM)),
        'cache_idx_k': nrm((DEPTH, n_phys, PAGE_SIZE, IDX_DIM)),
        'page_table': page_table,
        'state_conv': nrm((DEPTH, DEC_BATCH, CONV_WIDTH - 1, LRU_WIDTH)),
        'state_h': nrm((DEPTH, DEC_BATCH, LRU_WIDTH), 0.5),
        'w_in': nrm((DEPTH, D_MODEL, IN_WIDTH), D_MODEL ** -0.5),
        'idx_k_norm_g': gain((DEPTH, IDX_DIM)),
        'idx_k_norm_b': nrm((DEPTH, IDX_DIM), 0.02),
        'conv_w': nrm((DEPTH, CONV_WIDTH, LRU_WIDTH), CONV_WIDTH ** -0.5),
        'conv_b': nrm((DEPTH, LRU_WIDTH), 0.02),
        'lru_w_a': nrm((DEPTH, LRU_BLOCKS, LRU_BLOCK_DIM, LRU_BLOCK_DIM), LRU_BLOCK_DIM ** -0.5),
        'lru_b_a': nrm((DEPTH, LRU_WIDTH), 0.02),
        'lru_w_x': nrm((DEPTH, LRU_BLOCKS, LRU_BLOCK_DIM, LRU_BLOCK_DIM), LRU_BLOCK_DIM ** -0.5),
        'lru_b_x': nrm((DEPTH, LRU_WIDTH), 0.02),
        'lru_lambda': lru_lambda,
        'w_o_attn': nrm((DEPTH, ATTN_WIDTH, D_MODEL), BETA * ATTN_WIDTH ** -0.5),
        'w_o_lru': nrm((DEPTH, LRU_WIDTH, D_MODEL), BETA * LRU_WIDTH ** -0.5),
        'w_out': nrm((DEPTH, D_MODEL, D_MODEL), BETA * D_MODEL ** -0.5),
        'ln1_g': gain((DEPTH, D_MODEL)),
        'ln1_b': nrm((DEPTH, D_MODEL), 0.02),
        'w_router': nrm((DEPTH, D_MODEL, N_EXPERTS), D_MODEL ** -0.5),
        'b_router': nrm((DEPTH, N_EXPERTS), 0.01),
        'w_gate': nrm((DEPTH, N_EXPERTS, D_MODEL, EXPERT_DIM), D_MODEL ** -0.5),
        'w_up': nrm((DEPTH, N_EXPERTS, D_MODEL, EXPERT_DIM), D_MODEL ** -0.5),
        'w_down': nrm((DEPTH, N_EXPERTS, EXPERT_DIM, D_MODEL), BETA * EXPERT_DIM ** -0.5),
        'ws_gate': nrm((DEPTH, D_MODEL, EXPERT_DIM), D_MODEL ** -0.5),
        'ws_up': nrm((DEPTH, D_MODEL, EXPERT_DIM), D_MODEL ** -0.5),
        'ws_down': nrm((DEPTH, EXPERT_DIM, D_MODEL), BETA * EXPERT_DIM ** -0.5),
        'ln2_g': gain((DEPTH, D_MODEL)),
        'ln2_b': nrm((DEPTH, D_MODEL), 0.02),
    }


def reference(x_prompt, x_sample, cache_kv, cache_idx_k, page_table, state_conv, state_h,
              w_in, idx_k_norm_g, idx_k_norm_b, conv_w, conv_b, lru_w_a, lru_b_a, lru_w_x, lru_b_x,
              lru_lambda, w_o_attn, w_o_lru, w_out, ln1_g, ln1_b, w_router, b_router,
              w_gate, w_up, w_down, ws_gate, ws_up, ws_down, ln2_g, ln2_b):
    n_p, seq, _ = x_prompt.shape
    n_s, t_s, _ = x_sample.shape
    past = page_table.shape[1] * PAGE_SIZE
    pos_p = jnp.arange(seq)
    pos_s = past + jnp.arange(t_s)
    yp, ys = x_prompt, x_sample
    kv_p, ik_p, cv_p, h_p, kv_s, ik_s, cv_s, h_s = ([] for _ in range(8))
    for l in range(DEPTH):
        lru = (conv_w[l], conv_b[l], lru_w_a[l], lru_b_a[l], lru_w_x[l], lru_b_x[l], lru_lambda[l])
        tail = (w_o_attn[l], w_o_lru[l], w_out[l], ln1_g[l], ln1_b[l], w_router[l], b_router[l],
                w_gate[l], w_up[l], w_down[l], ws_gate[l], ws_up[l], ws_down[l], ln2_g[l], ln2_b[l])
        q, kv, qi, ki, wi, xl, ga, gb = in_project(yp, w_in[l], pos_p, idx_k_norm_g[l], idx_k_norm_b[l])
        o_attn = dsa_prompt(q, kv, qi, ki, wi)
        zero_hist = jnp.zeros((n_p, CONV_WIDTH - 1, LRU_WIDTH), yp.dtype)
        zero_h = jnp.zeros((n_p, LRU_WIDTH), yp.dtype)
        h_lru, hist, h_last = rglru_branch(xl, zero_hist, zero_h, *lru)
        yp = block_tail(yp, o_attn, h_lru, ga, gb, *tail)
        kv_p.append(kv); ik_p.append(ki); cv_p.append(hist); h_p.append(h_last)
        q, kv, qi, ki, wi, xl, ga, gb = in_project(ys, w_in[l], pos_s, idx_k_norm_g[l], idx_k_norm_b[l])
        o_attn = dsa_sample(q, kv, qi, ki, wi, pos_s, cache_kv[l], cache_idx_k[l], page_table)
        h_lru, hist, h_last = rglru_branch(xl, state_conv[l], state_h[l], *lru)
        ys = block_tail(ys, o_attn, h_lru, ga, gb, *tail)
        kv_s.append(kv); ik_s.append(ki); cv_s.append(hist); h_s.append(h_last)
    return (yp, ys, jnp.stack(kv_p), jnp.stack(ik_p), jnp.stack(cv_p), jnp.stack(h_p),
            jnp.stack(kv_s), jnp.stack(ik_s), jnp.stack(cv_s), jnp.stack(h_s))
```

```python
import functools
from typing import NamedTuple

import jax
import jax.numpy as jnp
import numpy as np
from jax import lax
from jax.experimental import pallas as pl
from jax.experimental.pallas import tpu as pltpu

F32 = jnp.float32
BF16 = jnp.bfloat16
I32 = jnp.int32

LANE = 128
SUBLANE = 8
VMEM_LIMIT = 56 * 1024 * 1024
MASKED = -1e30
INT_MIN = -2 ** 31


class Cfg(NamedTuple):
    n_heads: int = 16
    n_kv_heads: int = 4
    head_dim: int = 128
    idx_heads: int = 32
    idx_dim: int = 128
    topk_max: int = 256
    q_block: int = 128
    rope_theta: float = 10000.0
    lru_blocks: int = 16
    conv_width: int = 4
    lru_c: float = 8.0
    n_experts: int = 64
    top_k: int = 8
    n_groups: int = 8
    topk_groups: int = 4
    route_scale: float = 2.5
    ln_eps: float = 1e-5
    page_size: int = 128
    depth: int = 1
    key_chunk: int = 512
    moe_tile: int = 512
    row_align: int = 256


def _pick(dim, prefs):
    for p in prefs:
        if p <= dim and dim % p == 0:
            return p
    return dim


def _params(*sem):
    return pltpu.CompilerParams(dimension_semantics=sem, vmem_limit_bytes=VMEM_LIMIT)


def _mm_body(x_ref, w_ref, o_ref):
    o_ref[...] = jnp.dot(x_ref[...], w_ref[...], preferred_element_type=F32).astype(o_ref.dtype)


def _matmul(x, w, out_dtype=F32, name="matmul"):
    m, k = x.shape
    n = w.shape[1]
    tm = _pick(m, (1408, 1024, 768, 640, 512, 256))
    tn = _pick(n, (512, 384, 256, 128))
    return pl.pallas_call(
        _mm_body,
        grid=(m // tm, n // tn),
        in_specs=[pl.BlockSpec((tm, k), lambda i, j: (i, 0)),
                  pl.BlockSpec((k, tn), lambda i, j: (0, j))],
        out_specs=pl.BlockSpec((tm, tn), lambda i, j: (i, j)),
        out_shape=jax.ShapeDtypeStruct((m, n), out_dtype),
        compiler_params=_params("arbitrary", "arbitrary"),
        name=name,
    )(x, w)


def _rope(x, cosf, sinf):
    return x * cosf + pltpu.roll(x, x.shape[-1] // 2, axis=x.ndim - 1) * sinf


def _rope_tables(pos, dim, theta):
    half = dim // 2
    inv = theta ** (-jnp.arange(half, dtype=F32) / half)
    ang = pos.astype(F32)[:, None] * inv[None, :]
    cos, sin = jnp.cos(ang), jnp.sin(ang)
    return jnp.concatenate([cos, cos], -1), jnp.concatenate([-sin, sin], -1)


def _layer_norm(y, g, b, eps):
    mu = jnp.mean(y, axis=-1, keepdims=True)
    yc = y - mu
    var = jnp.mean(yc * yc, axis=-1, keepdims=True)
    return yc * lax.rsqrt(var + eps) * g + b


def _kv_body(kv_ref, ki_ref, cos_ref, sin_ref, g_ref, b_ref,
             kvo_ref, kio_ref, kbf_ref, kibf_ref, *, cfg):
    c_heads, d = cfg.n_kv_heads, cfg.head_dim
    cosf, sinf = cos_ref[...], sin_ref[...]
    kv = kv_ref[...]
    for c in range(c_heads):
        kc = _rope(kv[:, c * d:(c + 1) * d], cosf, sinf)
        kvo_ref[:, c * d:(c + 1) * d] = kc
        kbf_ref[:, c * d:(c + 1) * d] = kc.astype(BF16)
    kvo_ref[:, c_heads * d:] = kv[:, c_heads * d:]
    ki = _rope(_layer_norm(ki_ref[...], g_ref[...], b_ref[...], cfg.ln_eps), cosf, sinf)
    kio_ref[...] = ki
    kibf_ref[...] = ki.astype(BF16)


def _vt_body(v_ref, vt_ref, *, cfg):
    d = cfg.head_dim
    for c in range(cfg.n_kv_heads):
        vt_ref[0, c * d:(c + 1) * d, :] = v_ref[:, c * d:(c + 1) * d].T.astype(BF16)


def _transpose_values(z_qkv, t_p, cfg):
    kc = cfg.key_chunk
    kvw = cfg.n_kv_heads * cfg.head_dim
    v_blk = (cfg.n_heads * cfg.head_dim + kvw) // kvw
    assert t_p % kc == 0
    return pl.pallas_call(
        functools.partial(_vt_body, cfg=cfg),
        grid=(t_p // kc,),
        in_specs=[pl.BlockSpec((kc, kvw), lambda i: (i, v_blk))],
        out_specs=pl.BlockSpec((1, kvw, kc), lambda i: (i, 0, 0)),
        out_shape=jax.ShapeDtypeStruct((t_p // kc, kvw, kc), BF16),
        compiler_params=_params("arbitrary"),
        name="transpose_values",
    )(z_qkv)


def _finalize_keys(z_qkv, z_i, cosf, sinf, ik_g, ik_b, cfg):
    t_all = z_qkv.shape[0]
    c_heads, d = cfg.n_kv_heads, cfg.head_dim
    kvw = c_heads * d
    qw = cfg.n_heads * d
    tr = cfg.row_align
    assert t_all % tr == 0 and qw % (2 * kvw) == 0
    ki_blk = (cfg.idx_heads * cfg.idx_dim) // cfg.idx_dim
    return pl.pallas_call(
        functools.partial(_kv_body, cfg=cfg),
        grid=(t_all // tr,),
        in_specs=[pl.BlockSpec((tr, 2 * kvw), lambda i: (i, qw // (2 * kvw))),
                  pl.BlockSpec((tr, cfg.idx_dim), lambda i: (i, ki_blk)),
                  pl.BlockSpec((tr, d), lambda i: (i, 0)),
                  pl.BlockSpec((tr, d), lambda i: (i, 0)),
                  pl.BlockSpec((1, cfg.idx_dim), lambda i: (0, 0)),
                  pl.BlockSpec((1, cfg.idx_dim), lambda i: (0, 0))],
        out_specs=[pl.BlockSpec((tr, 2 * kvw), lambda i: (i, 0)),
                   pl.BlockSpec((tr, cfg.idx_dim), lambda i: (i, 0)),
                   pl.BlockSpec((tr, kvw), lambda i: (i, 0)),
                   pl.BlockSpec((tr, cfg.idx_dim), lambda i: (i, 0))],
        out_shape=[jax.ShapeDtypeStruct((t_all, 2 * kvw), F32),
                   jax.ShapeDtypeStruct((t_all, cfg.idx_dim), F32),
                   jax.ShapeDtypeStruct((t_all, kvw), BF16),
                   jax.ShapeDtypeStruct((t_all, cfg.idx_dim), BF16)],
        compiler_params=_params("arbitrary"),
        name="finalize_keys",
    )(z_qkv, z_i, cosf, sinf, ik_g, ik_b)


def _sort_key(s):
    b = pltpu.bitcast(s, I32)
    return b ^ ((b >> 31) & jnp.int32(0x7FFFFFFF))


def _kth_largest_key(count_ge, k, width):
    def bit_step(i, ans):
        cand = ans | lax.shift_left(jnp.int32(1), jnp.int32(31) - i)
        cnt = count_ge(cand ^ jnp.int32(INT_MIN))
        return jnp.where(cnt >= k, cand, ans)

    ans = lax.fori_loop(0, 32, bit_step, jnp.zeros((1, width), I32))
    return ans ^ jnp.int32(INT_MIN)


def _attn_body(q_ref, qi_ref, wi_ref, cos_ref, sin_ref, kbf_ref, kibf_ref, vt_ref,
               o_ref, qs_ref, qis_ref, key_ref, m_ref, l_ref, acc_ref, *, cfg, topk):
    qb_idx = pl.program_id(1)
    qb, kc = cfg.q_block, cfg.key_chunk
    d, di = cfg.head_dim, cfg.idx_dim
    c_heads = cfg.n_kv_heads
    g_heads = cfg.n_heads // c_heads
    cosf, sinf = cos_ref[...], sin_ref[...]

    for h in range(cfg.n_heads):
        qs_ref[h * qb:(h + 1) * qb, :] = _rope(q_ref[:, h * d:(h + 1) * d], cosf, sinf).astype(BF16)
    for h in range(cfg.idx_heads):
        qis_ref[h * qb:(h + 1) * qb, :] = _rope(qi_ref[:, h * di:(h + 1) * di], cosf, sinf).astype(BF16)
    w_t = wi_ref[...].T * F32((di * cfg.idx_heads) ** -0.5)

    q0 = qb_idx * qb
    n_chunks = (q0 + qb + kc - 1) // kc
    tpos = q0 + lax.broadcasted_iota(I32, (kc, qb), 1)
    krow = lax.broadcasted_iota(I32, (kc, qb), 0)
    nt = (((1,), (1,)), ((), ()))

    def score_chunk(ci, carry):
        k0 = pl.multiple_of(ci * kc, kc)
        ki_c = kibf_ref[pl.ds(k0, kc), :]
        acc = jnp.zeros((kc, qb), F32)
        for hp in range(cfg.idx_heads // 2):
            z = lax.dot_general(ki_c, qis_ref[hp * 2 * qb:(hp + 1) * 2 * qb, :], nt,
                                preferred_element_type=F32)
            acc = acc + jnp.maximum(z[:, :qb], 0.0) * w_t[2 * hp:2 * hp + 1, :]
            acc = acc + jnp.maximum(z[:, qb:], 0.0) * w_t[2 * hp + 1:2 * hp + 2, :]
        causal = (k0 + krow) <= tpos
        key_ref[pl.ds(k0, kc), :] = _sort_key(jnp.where(causal, acc, -jnp.inf))
        return carry

    lax.fori_loop(0, n_chunks, score_chunk, 0)

    def count_ge(cand):
        def cnt_chunk(ci, acc8):
            k0 = pl.multiple_of(ci * kc, kc)
            hit = jnp.where(key_ref[pl.ds(k0, kc), :] >= cand, 1, 0).astype(I32)
            return acc8 + jnp.sum(hit.reshape(kc // SUBLANE, SUBLANE, qb), axis=0)
        acc8 = lax.fori_loop(0, n_chunks, cnt_chunk, jnp.zeros((SUBLANE, qb), I32))
        return jnp.sum(acc8, axis=0, keepdims=True)

    thr = _kth_largest_key(count_ge, topk, qb)

    m_ref[...] = jnp.full(m_ref.shape, MASKED, F32)
    l_ref[...] = jnp.zeros(l_ref.shape, F32)
    acc_ref[...] = jnp.zeros(acc_ref.shape, F32)
    scale = F32(d ** -0.5)

    def attend_chunk(ci, carry):
        k0 = pl.multiple_of(ci * kc, kc)
        keep = (key_ref[pl.ds(k0, kc), :] >= thr) & ((k0 + krow) <= tpos)
        bias = jnp.where(keep, 0.0, MASKED).astype(F32)
        bias = jnp.concatenate([bias] * g_heads, axis=1)
        for c in range(c_heads):
            k_c = kbf_ref[pl.ds(k0, kc), c * d:(c + 1) * d]
            s = lax.dot_general(k_c, qs_ref[c * g_heads * qb:(c + 1) * g_heads * qb, :], nt,
                                preferred_element_type=F32) * scale + bias
            m_old = m_ref[c:c + 1, :]
            m_new = jnp.maximum(m_old, jnp.max(s, axis=0, keepdims=True))
            alpha = jnp.exp(m_old - m_new)
            p = jnp.exp(s - m_new)
            l_ref[c:c + 1, :] = alpha * l_ref[c:c + 1, :] + jnp.sum(p, axis=0, keepdims=True)
            pv = jnp.dot(vt_ref[ci, c * d:(c + 1) * d, :], p.astype(BF16), preferred_element_type=F32)
            acc_ref[c * d:(c + 1) * d, :] = alpha * acc_ref[c * d:(c + 1) * d, :] + pv
            m_ref[c:c + 1, :] = m_new
        return carry

    lax.fori_loop(0, n_chunks, attend_chunk, 0)

    for c in range(c_heads):
        o_t = acc_ref[c * d:(c + 1) * d, :] / l_ref[c:c + 1, :]
        for g in range(g_heads):
            h = c * g_heads + g
            o_ref[:, h * d:(h + 1) * d] = o_t[:, g * qb:(g + 1) * qb].T.astype(o_ref.dtype)


def _prompt_attention(z_qkv, z_i, z_w, cosf, sinf, k_bf, ki_bf, v_t, n_seq, seq, cfg):
    qb, kc = cfg.q_block, cfg.key_chunk
    d = cfg.head_dim
    qw = cfg.n_heads * d
    kvw = cfg.n_kv_heads * d
    iw = cfg.idx_heads * cfg.idx_dim
    assert seq % kc == 0 and seq % qb == 0 and kc % qb == 0
    assert cfg.head_dim == LANE and cfg.idx_dim == LANE
    nqb = seq // qb
    topk = min(cfg.topk_max, seq // 4)
    row = lambda n, j: (n * nqb + j, 0)
    return pl.pallas_call(
        functools.partial(_attn_body, cfg=cfg, topk=topk),
        grid=(n_seq, nqb),
        in_specs=[pl.BlockSpec((qb, qw), row),
                  pl.BlockSpec((qb, iw), row),
                  pl.BlockSpec((qb, LANE), row),
                  pl.BlockSpec((qb, d), row),
                  pl.BlockSpec((qb, d), row),
                  pl.BlockSpec((seq, kvw), lambda n, j: (n, 0)),
                  pl.BlockSpec((seq, cfg.idx_dim), lambda n, j: (n, 0)),
                  pl.BlockSpec((seq // kc, kvw, kc), lambda n, j: (n, 0, 0))],
        out_specs=pl.BlockSpec((qb, qw), row),
        out_shape=jax.ShapeDtypeStruct((n_seq * seq, qw), BF16),
        scratch_shapes=[pltpu.VMEM((cfg.n_heads * qb, d), BF16),
                        pltpu.VMEM((cfg.idx_heads * qb, cfg.idx_dim), BF16),
                        pltpu.VMEM((seq, qb), I32),
                        pltpu.VMEM((SUBLANE, (cfg.n_heads // cfg.n_kv_heads) * qb), F32),
                        pltpu.VMEM((SUBLANE, (cfg.n_heads // cfg.n_kv_heads) * qb), F32),
                        pltpu.VMEM((kvw, (cfg.n_heads // cfg.n_kv_heads) * qb), F32)],
        compiler_params=_params("arbitrary", "arbitrary"),
        name="prompt_attention",
    )(z_qkv, z_i, z_w, cosf, sinf, k_bf, ki_bf, v_t)


def _softplus(x):
    return jnp.maximum(x, 0.0) + jnp.log1p(jnp.exp(-jnp.abs(x)))


def _lru_gates(xc, wa_ref, wx_ref, ba, bx, lam, cfg):
    w = xc.shape[1]
    bd = w // cfg.lru_blocks
    xb = xc.astype(BF16)
    r_parts, i_parts = [], []
    for k in range(cfg.lru_blocks):
        xk = xb[:, k * bd:(k + 1) * bd]
        r_parts.append(jnp.dot(xk, wa_ref[k], preferred_element_type=F32))
        i_parts.append(jnp.dot(xk, wx_ref[k], preferred_element_type=F32))
    r = jax.nn.sigmoid(jnp.concatenate(r_parts, axis=1) + ba)
    gate_i = jax.nn.sigmoid(jnp.concatenate(i_parts, axis=1) + bx)
    log_a = (-cfg.lru_c * r) * _softplus(-lam)
    a = jnp.exp(log_a)
    u = jnp.sqrt(-jnp.tanh(log_a) * (a * a + 1.0)) * (gate_i * xc)
    return a, u


def _lru_prompt_body(xl_ref, cw_ref, cb_ref, wa_ref, wx_ref, ba_ref, bx_ref, lam_ref,
                     h_ref, hlast_ref, prev_ref, carry_ref, *, cfg):
    j = pl.program_id(1)
    tb = xl_ref.shape[0]
    cw = cfg.conv_width

    @pl.when(j == 0)
    def _():
        prev_ref[...] = jnp.zeros(prev_ref.shape, F32)
        carry_ref[...] = jnp.zeros(carry_ref.shape, F32)

    xl = xl_ref[...]
    ext = jnp.concatenate([prev_ref[...], xl], axis=0)
    off = SUBLANE - (cw - 1)
    xc = cb_ref[...] + ext[off:off + tb] * cw_ref[0:1, :]
    for t in range(1, cw):
        xc = xc + ext[off + t:off + t + tb] * cw_ref[t:t + 1, :]
    prev_ref[...] = xl[tb - SUBLANE:, :]

    a, u = _lru_gates(xc, wa_ref, wx_ref, ba_ref[...], bx_ref[...], lam_ref[...], cfg)

    row = lax.broadcasted_iota(I32, a.shape, 0) & (SUBLANE - 1)
    s = 1
    while s < SUBLANE:
        ok = row >= s
        a_sh = jnp.where(ok, pltpu.roll(a, s, axis=0), 1.0)
        u_sh = jnp.where(ok, pltpu.roll(u, s, axis=0), 0.0)
        u = u + a * u_sh
        a = a * a_sh
        s *= 2
    h_prev = carry_ref[...]
    for gi in range(tb // SUBLANE):
        sl = slice(gi * SUBLANE, (gi + 1) * SUBLANE)
        h_rows = u[sl] + a[sl] * h_prev
        h_ref[sl, :] = h_rows.astype(h_ref.dtype)
        h_prev = h_rows[SUBLANE - 1:SUBLANE, :]
    carry_ref[...] = h_prev
    hlast_ref[0] = h_prev


def _lru_prompt(z_b, conv_w, conv_b, wa, wx, ba, bx, lam, n_seq, seq, cfg):
    w = conv_w.shape[1]
    tb = _pick(seq, (256, 128))
    nb = seq // tb
    bd = w // cfg.lru_blocks
    vec = pl.BlockSpec((1, w), lambda n, j: (0, 0))
    return pl.pallas_call(
        functools.partial(_lru_prompt_body, cfg=cfg),
        grid=(n_seq, nb),
        in_specs=[pl.BlockSpec((tb, w), lambda n, j: (n * nb + j, 0)),
                  pl.BlockSpec((cfg.conv_width, w), lambda n, j: (0, 0)),
                  vec,
                  pl.BlockSpec((cfg.lru_blocks, bd, bd), lambda n, j: (0, 0, 0)),
                  pl.BlockSpec((cfg.lru_blocks, bd, bd), lambda n, j: (0, 0, 0)),
                  vec, vec, vec],
        out_specs=[pl.BlockSpec((tb, w), lambda n, j: (n * nb + j, 0)),
                   pl.BlockSpec((1, 1, w), lambda n, j: (n, 0, 0))],
        out_shape=[jax.ShapeDtypeStruct((n_seq * seq, w), BF16),
                   jax.ShapeDtypeStruct((n_seq, 1, w), F32)],
        scratch_shapes=[pltpu.VMEM((SUBLANE, w), F32), pltpu.VMEM((1, w), F32)],
        compiler_params=_params("arbitrary", "arbitrary"),
        name="lru_prompt",
    )(z_b, conv_w, conv_b, wa, wx, ba, bx, lam)


def _lru_sample_body(xl_ref, hist_ref, h0_ref, cw_ref, cb_ref, wa_ref, wx_ref, ba_ref, bx_ref, lam_ref,
                     h_ref, hbf_ref, *, cfg):
    cw = cfg.conv_width
    xc = cb_ref[...] + hist_ref[0] * cw_ref[0:1, :]
    for t in range(1, cw - 1):
        xc = xc + hist_ref[t] * cw_ref[t:t + 1, :]
    xc = xc + xl_ref[...] * cw_ref[cw - 1:cw, :]
    a, u = _lru_gates(xc, wa_ref, wx_ref, ba_ref[...], bx_ref[...], lam_ref[...], cfg)
    h = a * h0_ref[...] + u
    h_ref[...] = h
    hbf_ref[...] = h.astype(BF16)


def _lru_sample(xl, hist, h0, conv_w, conv_b, wa, wx, ba, bx, lam, cfg):
    n, w = xl.shape
    return pl.pallas_call(
        functools.partial(_lru_sample_body, cfg=cfg),
        out_shape=[jax.ShapeDtypeStruct((n, w), F32), jax.ShapeDtypeStruct((n, w), BF16)],
        compiler_params=pltpu.CompilerParams(vmem_limit_bytes=VMEM_LIMIT),
        name="lru_sample",
    )(xl, hist, h0, conv_w, conv_b, wa, wx, ba, bx, lam)


def _sample_scores_body(pt_ref, qi_ref, w_ref, kin_ref, cos_ref, sin_ref, *rest, cfg, pages_per_step):
    page_refs, o_ref = rest[:pages_per_step], rest[pages_per_step]
    j = pl.program_id(1)
    nj = pl.num_programs(1)
    cosf, sinf = cos_ref[0], sin_ref[0]
    qi = _rope(qi_ref[0], cosf, sinf)
    qi_b = qi.astype(BF16)
    w = w_ref[0] * F32((cfg.idx_dim * cfg.idx_heads) ** -0.5)
    nt = (((1,), (1,)), ((), ()))
    rows = []
    for p in range(pages_per_step):
        z = lax.dot_general(qi_b, page_refs[p][0].astype(BF16), nt, preferred_element_type=F32)
        rows.append(jnp.sum(jnp.maximum(z, 0.0) * w, axis=0, keepdims=True))
    r0 = pl.multiple_of(j * pages_per_step, pages_per_step)
    o_ref[0, pl.ds(r0, pages_per_step), :] = jnp.concatenate(rows, axis=0)

    @pl.when(j == nj - 1)
    def _():
        z_new = jnp.sum(qi * kin_ref[0], axis=1, keepdims=True)
        s_new = jnp.sum(jnp.maximum(z_new, 0.0) * w, axis=0, keepdims=True)
        lane = lax.broadcasted_iota(I32, (SUBLANE, LANE), 1)
        sub = lax.broadcasted_iota(I32, (SUBLANE, LANE), 0)
        tail = jnp.where((lane == 0) & (sub == 0), jnp.broadcast_to(s_new, (SUBLANE, LANE)), -jnp.inf)
        o_ref[0, pl.ds(nj * pages_per_step, SUBLANE), :] = tail


def _sample_scores(page_table, qi, w_rep, ki_new, cos_s, sin_s, cache_idx_k, cfg):
    n, n_pages = page_table.shape
    pps = _pick(n_pages, (SUBLANE,))
    assert cfg.page_size == LANE
    hi, di = cfg.idx_heads, cfg.idx_dim
    per_seq = lambda b, j, pt: (b, 0, 0)
    page_specs = [pl.BlockSpec((1, cfg.page_size, di),
                               functools.partial(lambda b, j, pt, p: (pt[b, j * pps + p], 0, 0), p=p))
                  for p in range(pps)]
    grid_spec = pltpu.PrefetchScalarGridSpec(
        num_scalar_prefetch=1,
        grid=(n, n_pages // pps),
        in_specs=[pl.BlockSpec((1, hi, di), per_seq),
                  pl.BlockSpec((1, hi, LANE), per_seq),
                  pl.BlockSpec((1, 1, di), per_seq),
                  pl.BlockSpec((1, 1, di), per_seq),
                  pl.BlockSpec((1, 1, di), per_seq)] + page_specs,
        out_specs=pl.BlockSpec((1, n_pages + SUBLANE, LANE), per_seq),
    )
    return pl.pallas_call(
        functools.partial(_sample_scores_body, cfg=cfg, pages_per_step=pps),
        grid_spec=grid_spec,
        out_shape=jax.ShapeDtypeStruct((n, n_pages + SUBLANE, LANE), F32),
        compiler_params=_params("arbitrary", "arbitrary"),
        name="sample_scores",
    )(page_table, qi, w_rep, ki_new, cos_s, sin_s, *([cache_idx_k] * pps))


def _sample_attend_body(pt_ref, s_ref, q_ref, kvn_ref, cos_ref, sin_ref, *rest,
                        cfg, pages_per_step, topk):
    page_refs = rest[:pages_per_step]
    o_ref, qs_ref, keep_ref, m_ref, l_ref, acc_ref = rest[pages_per_step:]
    j = pl.program_id(1)
    nj = pl.num_programs(1)
    nh, d = cfg.n_heads, cfg.head_dim
    c_heads = cfg.n_kv_heads
    g_heads = nh // c_heads
    ps = cfg.page_size
    n_rows = s_ref.shape[1]
    scale = F32(d ** -0.5)
    head_c = lax.broadcasted_iota(I32, (nh, 1), 0) // g_heads

    @pl.when(j == 0)
    def _():
        keys = _sort_key(s_ref[0])

        def count_ge(cand):
            hit = jnp.where(keys >= cand, 1, 0).astype(I32)
            return jnp.sum(jnp.sum(hit, axis=0, keepdims=True), axis=1, keepdims=True)

        thr = _kth_largest_key(count_ge, topk, 1)
        finite = s_ref[0] > -jnp.inf
        keep_ref[...] = jnp.where((keys >= thr) & finite, 0.0, MASKED).astype(F32)
        qs_ref[...] = _rope(q_ref[0], cos_ref[0], sin_ref[0]).astype(BF16)
        m_ref[...] = jnp.full(m_ref.shape, MASKED, F32)
        l_ref[...] = jnp.zeros(l_ref.shape, F32)
        acc_ref[...] = jnp.zeros(acc_ref.shape, F32)

    nt = (((1,), (1,)), ((), ()))
    qs = qs_ref[...]
    r0 = pl.multiple_of(j * pages_per_step, pages_per_step)
    bias_rows = keep_ref[pl.ds(r0, pages_per_step), :]
    kv = jnp.concatenate([page_refs[p][0] for p in range(pages_per_step)], axis=0).astype(BF16)
    bias = jnp.concatenate([bias_rows[p:p + 1, :] for p in range(pages_per_step)], axis=1)
    s = jnp.zeros((nh, pages_per_step * ps), F32)
    for c in range(c_heads):
        s_c = lax.dot_general(qs, kv[:, c * d:(c + 1) * d], nt, preferred_element_type=F32)
        s = jnp.where(head_c == c, s_c, s)
    s = s * scale + bias
    m_old = m_ref[...]
    m_new = jnp.maximum(m_old, jnp.max(s, axis=1, keepdims=True))
    alpha = jnp.exp(m_old - m_new)
    p_un = jnp.exp(s - m_new)
    l_ref[...] = alpha * l_ref[...] + jnp.sum(p_un, axis=1, keepdims=True)
    p_b = p_un.astype(BF16)
    pv = jnp.zeros((nh, d), F32)
    for c in range(c_heads):
        pv_c = jnp.dot(p_b, kv[:, (c_heads + c) * d:(c_heads + c + 1) * d], preferred_element_type=F32)
        pv = jnp.where(head_c == c, pv_c, pv)
    acc_ref[...] = alpha * acc_ref[...] + pv
    m_ref[...] = m_new

    @pl.when(j == nj - 1)
    def _():
        kvn = kvn_ref[0]
        k_new = jnp.zeros((nh, d), F32)
        v_new = jnp.zeros((nh, d), F32)
        for c in range(c_heads):
            k_new = jnp.where(head_c == c, kvn[c:c + 1, :], k_new)
            v_new = jnp.where(head_c == c, kvn[c_heads + c:c_heads + c + 1, :], v_new)
        q_f = _rope(q_ref[0], cos_ref[0], sin_ref[0])
        bias_new = keep_ref[n_rows - SUBLANE:n_rows - SUBLANE + 1, 0:1]
        s_new = jnp.sum(q_f * k_new, axis=1, keepdims=True) * scale + bias_new
        m_o = m_ref[...]
        m_n = jnp.maximum(m_o, s_new)
        al = jnp.exp(m_o - m_n)
        p_new = jnp.exp(s_new - m_n)
        l_fin = al * l_ref[...] + p_new
        acc_fin = al * acc_ref[...] + p_new * v_new
        o_ref[0] = (acc_fin / l_fin).astype(o_ref.dtype)


def _sample_attend(page_table, scores, q, kv_new, cos_s, sin_s, cache_kv, cfg):
    n, n_pages = page_table.shape
    pps = _pick(n_pages, (SUBLANE,))
    nh, d = cfg.n_heads, cfg.head_dim
    kvw2 = 2 * cfg.n_kv_heads * d
    topk = min(cfg.topk_max, (n_pages * cfg.page_size + 1) // 4)
    per_seq = lambda b, j, pt: (b, 0, 0)
    page_specs = [pl.BlockSpec((1, cfg.page_size, kvw2),
                               functools.partial(lambda b, j, pt, p: (pt[b, j * pps + p], 0, 0), p=p))
                  for p in range(pps)]
    n_rows = scores.shape[1]
    grid_spec = pltpu.PrefetchScalarGridSpec(
        num_scalar_prefetch=1,
        grid=(n, n_pages // pps),
        in_specs=[pl.BlockSpec((1, n_rows, LANE), per_seq),
                  pl.BlockSpec((1, nh, d), per_seq),
                  pl.BlockSpec((1, 2 * cfg.n_kv_heads, d), per_seq),
                  pl.BlockSpec((1, 1, d), per_seq),
                  pl.BlockSpec((1, 1, d), per_seq)] + page_specs,
        out_specs=pl.BlockSpec((1, nh, d), per_seq),
        scratch_shapes=[pltpu.VMEM((nh, d), BF16),
                        pltpu.VMEM((n_rows, LANE), F32),
                        pltpu.VMEM((nh, 1), F32),
                        pltpu.VMEM((nh, 1), F32),
                        pltpu.VMEM((nh, d), F32)],
    )
    return pl.pallas_call(
        functools.partial(_sample_attend_body, cfg=cfg, pages_per_step=pps, topk=topk),
        grid_spec=grid_spec,
        out_shape=jax.ShapeDtypeStruct((n, nh, d), BF16),
        compiler_params=_params("arbitrary", "arbitrary"),
        name="sample_attend",
    )(page_table, scores, q, kv_new, cos_s, sin_s, *([cache_kv] * pps))


def _merge_body(oa_ref, hl_ref, woa_ref, wol_ref, ga_ref, gb_ref, o_ref):
    ya = jnp.dot(oa_ref[...], woa_ref[...], preferred_element_type=F32)
    yl = jnp.dot(hl_ref[...], wol_ref[...], preferred_element_type=F32)
    o_ref[...] = (jax.nn.sigmoid(ga_ref[...]) * ya + jax.nn.sigmoid(gb_ref[...]) * yl).astype(o_ref.dtype)


def _merge(o_attn, h_lru, w_oa, w_ol, z_b, lru_w, cfg):
    m, aw = o_attn.shape
    dm = w_oa.shape[1]
    tm = _pick(m, (1408, 1024, 768, 640, 512, 256))
    tn = _pick(dm, (512, 256, 128))
    assert lru_w % tn == 0
    ga_off = lru_w // tn
    gb_off = (lru_w + dm) // tn
    return pl.pallas_call(
        _merge_body,
        grid=(m // tm, dm // tn),
        in_specs=[pl.BlockSpec((tm, aw), lambda i, j: (i, 0)),
                  pl.BlockSpec((tm, lru_w), lambda i, j: (i, 0)),
                  pl.BlockSpec((aw, tn), lambda i, j: (0, j)),
                  pl.BlockSpec((lru_w, tn), lambda i, j: (0, j)),
                  pl.BlockSpec((tm, tn), lambda i, j: (i, ga_off + j)),
                  pl.BlockSpec((tm, tn), lambda i, j: (i, gb_off + j))],
        out_specs=pl.BlockSpec((tm, tn), lambda i, j: (i, j)),
        out_shape=jax.ShapeDtypeStruct((m, dm), BF16),
        compiler_params=_params("arbitrary", "arbitrary"),
        name="merge_mixers",
    )(o_attn, h_lru, w_oa, w_ol, z_b, z_b)


def _proj_ln_body(a_ref, w_ref, x_ref, g_ref, b_ref, o_ref, obf_ref, *, alpha, eps, tn):
    j = pl.program_id(1)
    c0 = pl.multiple_of(j * tn, tn)
    y = jnp.dot(a_ref[...], w_ref[...], preferred_element_type=F32)
    o_ref[:, pl.ds(c0, tn)] = F32(alpha) * x_ref[...] + y

    @pl.when(j == pl.num_programs(1) - 1)
    def _():
        out = _layer_norm(o_ref[...], g_ref[...], b_ref[...], eps)
        o_ref[...] = out
        obf_ref[...] = out.astype(BF16)


def _proj_residual_ln(a, w, x, g, b, alpha, cfg):
    m, k = a.shape
    dm = w.shape[1]
    tm = _pick(m, (384, 256))
    tn = _pick(dm, (512, 256, 128))
    return pl.pallas_call(
        functools.partial(_proj_ln_body, alpha=alpha, eps=cfg.ln_eps, tn=tn),
        grid=(m // tm, dm // tn),
        in_specs=[pl.BlockSpec((tm, k), lambda i, j: (i, 0)),
                  pl.BlockSpec((k, tn), lambda i, j: (0, j)),
                  pl.BlockSpec((tm, tn), lambda i, j: (i, j)),
                  pl.BlockSpec((1, dm), lambda i, j: (0, 0)),
                  pl.BlockSpec((1, dm), lambda i, j: (0, 0))],
        out_specs=[pl.BlockSpec((tm, dm), lambda i, j: (i, 0)),
                   pl.BlockSpec((tm, dm), lambda i, j: (i, 0))],
        out_shape=[jax.ShapeDtypeStruct((m, dm), F32), jax.ShapeDtypeStruct((m, dm), BF16)],
        compiler_params=_params("arbitrary", "arbitrary"),
        name="proj_residual_ln",
    )(a, w, x, g, b)


def _first_index_of_max(v, idx, big):
    m = jnp.max(v, axis=0, keepdims=True)
    first = jnp.min(jnp.where(v == m, idx, big), axis=0, keepdims=True)
    return m, first


def _router_body(x_ref, wr_ref, br_ref, ids_ref, gate_ref, *, cfg):
    e, ng = cfg.n_experts, cfg.n_groups
    per = e // ng
    tm = x_ref.shape[0]
    nt = (((1,), (1,)), ((), ()))
    logits = lax.dot_general(wr_ref[...], x_ref[...], nt, preferred_element_type=F32)
    s = jax.nn.sigmoid(logits)
    choice = s + br_ref[...]
    eidx = lax.broadcasted_iota(I32, (e, tm), 0)

    grp_rows = []
    jidx = lax.broadcasted_iota(I32, (per, tm), 0)
    for g in range(ng):
        cg = choice[g * per:(g + 1) * per, :]
        m1, j1 = _first_index_of_max(cg, jidx, per)
        m2 = jnp.max(jnp.where(jidx == j1, -jnp.inf, cg), axis=0, keepdims=True)
        grp_rows.append(m1 + m2)
    grp = jnp.concatenate(grp_rows, axis=0)

    gidx = lax.broadcasted_iota(I32, (ng, tm), 0)
    grp_keep = jnp.zeros((ng, tm), F32)
    work = grp
    for _ in range(cfg.topk_groups):
        _, gsel = _first_index_of_max(work, gidx, ng)
        hit = gidx == gsel
        grp_keep = jnp.where(hit, 1.0, grp_keep)
        work = jnp.where(hit, -jnp.inf, work)

    keep_rows = [jnp.broadcast_to(grp_keep[g:g + 1, :], (per, tm)) for g in range(ng)]
    masked = jnp.where(jnp.concatenate(keep_rows, axis=0) > 0.5, choice, -jnp.inf)

    ids, wts = [], []
    for _ in range(cfg.top_k):
        _, esel = _first_index_of_max(masked, eidx, e)
        hit = eidx == esel
        ids.append(esel)
        wts.append(jnp.sum(jnp.where(hit, s, 0.0), axis=0, keepdims=True))
        masked = jnp.where(hit, -jnp.inf, masked)
    wk = jnp.concatenate(wts, axis=0)
    ids_ref[...] = jnp.concatenate(ids, axis=0)
    gate_ref[...] = F32(cfg.route_scale) * wk / jnp.sum(wk, axis=0, keepdims=True)


def _router(x_bf, w_r_t, b_r, cfg):
    m, dm = x_bf.shape
    tm = _pick(m, (256, 128))
    e = cfg.n_experts
    return pl.pallas_call(
        functools.partial(_router_body, cfg=cfg),
        grid=(m // tm,),
        in_specs=[pl.BlockSpec((tm, dm), lambda i: (i, 0)),
                  pl.BlockSpec((e, dm), lambda i: (0, 0)),
                  pl.BlockSpec((e, 1), lambda i: (0, 0))],
        out_specs=[pl.BlockSpec((cfg.top_k, tm), lambda i: (0, i)),
                   pl.BlockSpec((cfg.top_k, tm), lambda i: (0, i))],
        out_shape=[jax.ShapeDtypeStruct((cfg.top_k, m), I32),
                   jax.ShapeDtypeStruct((cfg.top_k, m), F32)],
        compiler_params=_params("arbitrary"),
        name="router",
    )(x_bf, w_r_t, b_r)


def _gather_rows(idx_vmem, idx_smem, idx_sem, src_hbm, buf_ref, sem, n):
    cp = pltpu.make_async_copy(idx_vmem, idx_smem, idx_sem)
    cp.start()
    cp.wait()

    def start(r, carry):
        pltpu.make_async_copy(src_hbm.at[pl.ds(idx_smem[0, r], 1), :],
                              buf_ref.at[pl.ds(r, 1), :], sem).start()
        return carry
    lax.fori_loop(0, n, start, 0)


def _dispatch_body(nrows_ref, src_ref, x_hbm, o_ref, idx_smem, buf_ref, idx_sem, sem, *, tile):
    i = pl.program_id(0)

    @pl.when(i * tile < nrows_ref[0])
    def _():
        _gather_rows(src_ref.at[0], idx_smem, idx_sem, x_hbm, buf_ref, sem, tile)
        pltpu.make_async_copy(x_hbm.at[pl.ds(0, tile), :], buf_ref, sem).wait()
        o_ref[...] = buf_ref[...].astype(o_ref.dtype)


def _dispatch(x, src_rows, n_rows, tile):
    r_max = src_rows.shape[0]
    dm = x.shape[1]
    n_tiles = r_max // tile
    last = lambda i, nr: jnp.minimum(i, (nr[0] - 1) // tile)
    grid_spec = pltpu.PrefetchScalarGridSpec(
        num_scalar_prefetch=1,
        grid=(n_tiles,),
        in_specs=[pl.BlockSpec((1, 1, tile), lambda i, nr: (last(i, nr), 0, 0)),
                  pl.BlockSpec(memory_space=pl.ANY)],
        out_specs=pl.BlockSpec((tile, dm), lambda i, nr: (last(i, nr), 0)),
        scratch_shapes=[pltpu.SMEM((1, tile), I32), pltpu.VMEM((tile, dm), x.dtype),
                        pltpu.SemaphoreType.DMA(()), pltpu.SemaphoreType.DMA(())],
    )
    return pl.pallas_call(
        functools.partial(_dispatch_body, tile=tile),
        grid_spec=grid_spec,
        out_shape=jax.ShapeDtypeStruct((r_max, dm), BF16),
        compiler_params=_params("arbitrary"),
        name="moe_dispatch",
    )(n_rows, src_rows.reshape(n_tiles, 1, tile), x)


def _expert_up_body(te_ref, tf_ref, nt_ref, x_ref, wg_ref, wu_ref, gw_ref, o_ref, wg_s, wu_s):
    m = pl.program_id(1)

    @pl.when(tf_ref[m] == 1)
    def _():
        wg_s[...] = wg_ref[0].astype(BF16)
        wu_s[...] = wu_ref[0].astype(BF16)

    @pl.when(m < nt_ref[0])
    def _():
        x = x_ref[...]
        g = jnp.dot(x, wg_s[...], preferred_element_type=F32)
        u = jnp.dot(x, wu_s[...], preferred_element_type=F32)
        o_ref[...] = (jax.nn.silu(g) * u * gw_ref[...]).astype(o_ref.dtype)


def _expert_up(xs, w_gate, w_up, gate_rows, tile_expert, tile_first, n_tiles, tile):
    r_max, dm = xs.shape
    e, _, f = w_gate.shape
    tf = _pick(f, (256, 128))
    n_mt = r_max // tile
    clamp = lambda m, nt: jnp.minimum(m, nt[0] - 1)
    grid_spec = pltpu.PrefetchScalarGridSpec(
        num_scalar_prefetch=3,
        grid=(f // tf, n_mt),
        in_specs=[pl.BlockSpec((tile, dm), lambda fi, m, te, tfst, nt: (clamp(m, nt), 0)),
                  pl.BlockSpec((1, dm, tf), lambda fi, m, te, tfst, nt: (te[m], 0, fi)),
                  pl.BlockSpec((1, dm, tf), lambda fi, m, te, tfst, nt: (te[m], 0, fi)),
                  pl.BlockSpec((tile, 1), lambda fi, m, te, tfst, nt: (clamp(m, nt), 0))],
        out_specs=pl.BlockSpec((tile, tf), lambda fi, m, te, tfst, nt: (clamp(m, nt), fi)),
        scratch_shapes=[pltpu.VMEM((dm, tf), BF16), pltpu.VMEM((dm, tf), BF16)],
    )
    return pl.pallas_call(
        _expert_up_body,
        grid_spec=grid_spec,
        out_shape=jax.ShapeDtypeStruct((r_max, f), BF16),
        compiler_params=_params("arbitrary", "arbitrary"),
        name="expert_up",
    )(tile_expert, tile_first, n_tiles, xs, w_gate, w_up, gate_rows)


def _expert_down_body(te_ref, tf_ref, nt_ref, h_ref, wd_ref, o_ref, wd_s):
    m = pl.program_id(1)

    @pl.when(tf_ref[m] == 1)
    def _():
        wd_s[...] = wd_ref[0].astype(BF16)

    @pl.when(m < nt_ref[0])
    def _():
        o_ref[...] = jnp.dot(h_ref[...], wd_s[...], preferred_element_type=F32)


def _expert_down(hid, w_down, tile_expert, tile_first, n_tiles, tile):
    r_max, f = hid.shape
    dm = w_down.shape[2]
    tn = _pick(dm, (1024, 512, 256, 128))
    n_mt = r_max // tile
    clamp = lambda m, nt: jnp.minimum(m, nt[0] - 1)
    grid_spec = pltpu.PrefetchScalarGridSpec(
        num_scalar_prefetch=3,
        grid=(dm // tn, n_mt),
        in_specs=[pl.BlockSpec((tile, f), lambda ni, m, te, tfst, nt: (clamp(m, nt), 0)),
                  pl.BlockSpec((1, f, tn), lambda ni, m, te, tfst, nt: (te[m], 0, ni))],
        out_specs=pl.BlockSpec((tile, tn), lambda ni, m, te, tfst, nt: (clamp(m, nt), ni)),
        scratch_shapes=[pltpu.VMEM((f, tn), BF16)],
    )
    return pl.pallas_call(
        _expert_down_body,
        grid_spec=grid_spec,
        out_shape=jax.ShapeDtypeStruct((r_max, dm), F32),
        compiler_params=_params("arbitrary", "arbitrary"),
        name="expert_down",
    )(tile_expert, tile_first, n_tiles, hid, w_down)


def _shared_up_body(x_ref, wg_ref, wu_ref, o_ref):
    x = x_ref[...]
    g = jnp.dot(x, wg_ref[...], preferred_element_type=F32)
    u = jnp.dot(x, wu_ref[...], preferred_element_type=F32)
    o_ref[...] = (jax.nn.silu(g) * u).astype(o_ref.dtype)


def _shared_up(x_bf, wg, wu):
    m, dm = x_bf.shape
    f = wg.shape[1]
    tm = _pick(m, (1408, 1024, 768, 640, 512, 256))
    tf = _pick(f, (256, 128))
    return pl.pallas_call(
        _shared_up_body,
        grid=(m // tm, f // tf),
        in_specs=[pl.BlockSpec((tm, dm), lambda i, j: (i, 0)),
                  pl.BlockSpec((dm, tf), lambda i, j: (0, j)),
                  pl.BlockSpec((dm, tf), lambda i, j: (0, j))],
        out_specs=pl.BlockSpec((tm, tf), lambda i, j: (i, j)),
        out_shape=jax.ShapeDtypeStruct((m, f), BF16),
        compiler_params=_params("arbitrary", "arbitrary"),
        name="shared_up",
    )(x_bf, wg, wu)


def _combine_body(pos_ref, ys_hbm, x_ref, sh_ref, wsd_ref, g_ref, b_ref, o_ref,
                  idx_smem, buf_ref, idx_sem, sem, *, tb, top_k, alpha, eps):
    n = tb * top_k
    _gather_rows(pos_ref.at[0], idx_smem, idx_sem, ys_hbm, buf_ref, sem, n)
    shared = jnp.dot(sh_ref[...], wsd_ref[...], preferred_element_type=F32)
    pltpu.make_async_copy(ys_hbm.at[pl.ds(0, n), :], buf_ref, sem).wait()
    routed = buf_ref[0:tb, :]
    for k in range(1, top_k):
        routed = routed + buf_ref[k * tb:(k + 1) * tb, :]
    y = F32(alpha) * x_ref[...] + (routed + shared)
    o_ref[...] = _layer_norm(y, g_ref[...], b_ref[...], eps)


def _combine(pos, ys, x, sh_hid, ws_down, g, b, alpha, cfg):
    m, dm = x.shape
    f = sh_hid.shape[1]
    tb = _pick(m, (64, 32, 16, 8))
    n = tb * cfg.top_k
    return pl.pallas_call(
        functools.partial(_combine_body, tb=tb, top_k=cfg.top_k, alpha=alpha, eps=cfg.ln_eps),
        grid=(m // tb,),
        in_specs=[pl.BlockSpec((1, 1, n), lambda i: (i, 0, 0)),
                  pl.BlockSpec(memory_space=pl.ANY),
                  pl.BlockSpec((tb, dm), lambda i: (i, 0)),
                  pl.BlockSpec((tb, f), lambda i: (i, 0)),
                  pl.BlockSpec((f, dm), lambda i: (0, 0)),
                  pl.BlockSpec((1, dm), lambda i: (0, 0)),
                  pl.BlockSpec((1, dm), lambda i: (0, 0))],
        out_specs=pl.BlockSpec((tb, dm), lambda i: (i, 0)),
        out_shape=jax.ShapeDtypeStruct((m, dm), F32),
        scratch_shapes=[pltpu.SMEM((1, n), I32), pltpu.VMEM((n, dm), F32),
                        pltpu.SemaphoreType.DMA(()), pltpu.SemaphoreType.DMA(())],
        compiler_params=_params("arbitrary"),
        name="moe_combine",
    )(pos.reshape(m // tb, 1, n), ys, x, sh_hid, ws_down, g, b)


def _routing_tables(ids, gates, n_tok, tile, cfg):
    e, k = cfg.n_experts, cfg.top_k
    n_pairs = n_tok * k
    r_max = ((n_pairs + e * (tile - 1)) // tile + 1) * tile
    e_flat = ids[:, :n_tok].T.reshape(-1)
    g_flat = gates[:, :n_tok].T.reshape(-1)
    order = jnp.argsort(e_flat, stable=True).astype(I32)
    e_sorted = e_flat[order]
    counts = jnp.sum((e_flat[:, None] == jnp.arange(e, dtype=I32)[None, :]).astype(I32), axis=0)
    padded = ((counts + tile - 1) // tile) * tile
    start_pad = jnp.cumsum(padded) - padded
    start_raw = jnp.cumsum(counts) - counts
    dest = start_pad[e_sorted] + (jnp.arange(n_pairs, dtype=I32) - start_raw[e_sorted])
    src_rows = jnp.zeros((r_max,), I32).at[dest].set(order // k, unique_indices=True, indices_are_sorted=True)
    gate_rows = jnp.zeros((r_max,), F32).at[dest].set(g_flat[order], unique_indices=True, indices_are_sorted=True)
    pos_of_pair = jnp.zeros((n_pairs,), I32).at[order].set(dest, unique_indices=True)
    n_rows = jnp.sum(padded).astype(I32)
    n_tiles = n_rows // tile
    tile_start = jnp.arange(r_max // tile, dtype=I32) * tile
    tile_expert = jnp.clip(jnp.searchsorted(start_pad + padded, tile_start, side="right"), 0, e - 1).astype(I32)
    prev = jnp.concatenate([jnp.full((1,), -1, I32), tile_expert[:-1]])
    tile_first = (tile_expert != prev).astype(I32)
    return src_rows, gate_rows[:, None], pos_of_pair, n_rows.reshape(1), n_tiles.reshape(1), tile_expert, tile_first


def _layer(cfg, l, x_all, n_p, seq, n_s, pos_all, cache_kv, cache_idx_k, page_table, state_conv, state_h,
           w_in, ik_g, ik_b, conv_w, conv_b, lru_w_a, lru_b_a, lru_w_x, lru_b_x, lru_lambda,
           w_o_attn, w_o_lru, w_out, ln1_g, ln1_b, w_router, b_router,
           w_gate, w_up, w_down, ws_gate, ws_up, ws_down, ln2_g, ln2_b):
    t_all, dm = x_all.shape
    t_p = n_p * seq
    d, di = cfg.head_dim, cfg.idx_dim
    qw, kvw, iw = cfg.n_heads * d, cfg.n_kv_heads * d, cfg.idx_heads * di
    lw = conv_w.shape[1]
    alpha = (2.0 * cfg.depth) ** 0.25
    c_qkv = qw + 2 * kvw
    c_i = c_qkv + iw + di
    c_w = c_i + cfg.idx_heads

    x_bf = x_all.astype(BF16)
    z_qkv = _matmul(x_bf, w_in[:, :c_qkv].astype(BF16), name="in_proj_qkv")
    z_i = _matmul(x_bf, w_in[:, c_qkv:c_i].astype(BF16), name="in_proj_idx")
    w_wi = jnp.pad(w_in[:, c_i:c_w], ((0, 0), (0, LANE - cfg.idx_heads))).astype(BF16)
    z_w = _matmul(x_bf, w_wi, name="in_proj_idx_w")
    z_b = _matmul(x_bf, w_in[:, c_w:].astype(BF16), name="in_proj_lru_gates")

    cosf, sinf = _rope_tables(pos_all, d, cfg.rope_theta)
    kv_all, ki_all, k_bf, ki_bf = _finalize_keys(
        z_qkv, z_i, cosf, sinf, ik_g.reshape(1, di), ik_b.reshape(1, di), cfg)
    v_t = _transpose_values(z_qkv, t_p, cfg)

    o_attn_p = _prompt_attention(z_qkv, z_i, z_w, cosf, sinf, k_bf, ki_bf, v_t, n_p, seq, cfg)
    wa_bf, wx_bf = lru_w_a.astype(BF16), lru_w_x.astype(BF16)
    row = lambda v: v.reshape(1, -1)
    h_lru_p, h_last_p = _lru_prompt(z_b, conv_w, row(conv_b), wa_bf, wx_bf, row(lru_b_a), row(lru_b_x),
                                    row(lru_lambda), n_p, seq, cfg)

    sl = slice(t_p, t_p + n_s)
    cos_s, sin_s = cosf[sl][:, None, :], sinf[sl][:, None, :]
    qi_s = z_i[sl, :iw].reshape(n_s, cfg.idx_heads, di)
    w_rep = jnp.broadcast_to(z_w[sl, :cfg.idx_heads][:, :, None], (n_s, cfg.idx_heads, LANE))
    n_phys = cache_idx_k.shape[0]
    scores = _sample_scores(page_table, qi_s, w_rep, ki_all[sl][:, None, :], cos_s, sin_s, cache_idx_k, cfg)
    q_s = z_qkv[sl, :qw].reshape(n_s, cfg.n_heads, d)
    kv_new = kv_all[sl].reshape(n_s, 2 * cfg.n_kv_heads, d)
    o_attn_s = _sample_attend(page_table, scores, q_s, kv_new, cos_s, sin_s,
                              cache_kv.reshape(n_phys, cfg.page_size, 2 * kvw), cfg)
    xl_s = z_b[sl, :lw]
    hist_s = jnp.moveaxis(state_conv, 1, 0)
    h_s, h_s_bf = _lru_sample(xl_s, hist_s, state_h, conv_w, row(conv_b), wa_bf, wx_bf,
                              row(lru_b_a), row(lru_b_x), row(lru_lambda), cfg)

    pad = t_all - t_p - n_s
    o_attn = jnp.concatenate([o_attn_p, o_attn_s.reshape(n_s, qw), jnp.zeros((pad, qw), BF16)], axis=0)
    h_lru = jnp.concatenate([h_lru_p, h_s_bf, jnp.zeros((pad, lw), BF16)], axis=0)

    merged = _merge(o_attn, h_lru, w_o_attn.astype(BF16), w_o_lru.astype(BF16), z_b, lw, cfg)
    x1, x1_bf = _proj_residual_ln(merged, w_out.astype(BF16), x_all, row(ln1_g), row(ln1_b), alpha, cfg)

    ids, gates = _router(x1_bf, w_router.T.astype(BF16), b_router.reshape(-1, 1), cfg)
    n_tok = t_p + n_s
    tile = cfg.moe_tile
    src_rows, gate_rows, pos_of_pair, n_rows, n_tiles, tile_expert, tile_first = _routing_tables(
        ids, gates, n_tok, tile, cfg)
    xs = _dispatch(x1, src_rows, n_rows, tile)
    hid = _expert_up(xs, w_gate, w_up, gate_rows, tile_expert, tile_first, n_tiles, tile)
    ys = _expert_down(hid, w_down, tile_expert, tile_first, n_tiles, tile)
    sh_hid = _shared_up(x1_bf, ws_gate.astype(BF16), ws_up.astype(BF16))
    tb = _pick(t_all, (64, 32, 16, 8))
    pos_pad = jnp.concatenate([pos_of_pair.reshape(n_tok, cfg.top_k),
                               jnp.zeros((t_all - n_tok, cfg.top_k), I32)], axis=0)
    pos_blocks = pos_pad.reshape(t_all // tb, tb, cfg.top_k).transpose(0, 2, 1).reshape(-1)
    y_all = _combine(pos_blocks, ys, x1, sh_hid, ws_down.astype(BF16), row(ln2_g), row(ln2_b), alpha, cfg)

    conv_p = z_b[:t_p, :lw].reshape(n_p, seq, lw)[:, seq - (cfg.conv_width - 1):, :]
    conv_s = jnp.concatenate([state_conv[:, 1:, :], xl_s[:, None, :]], axis=1)
    outs = dict(
        kv_p=kv_all[:t_p].reshape(n_p, seq, 2, cfg.n_kv_heads, d),
        ik_p=ki_all[:t_p].reshape(n_p, seq, di),
        cv_p=conv_p, h_p=h_last_p.reshape(n_p, lw),
        kv_s=kv_all[sl].reshape(n_s, 1, 2, cfg.n_kv_heads, d),
        ik_s=ki_all[sl].reshape(n_s, 1, di),
        cv_s=conv_s, h_s=h_s)
    return y_all, outs


def _forward(cfg, x_prompt, x_sample, cache_kv, cache_idx_k, page_table, state_conv, state_h,
             w_in, idx_k_norm_g, idx_k_norm_b, conv_w, conv_b, lru_w_a, lru_b_a, lru_w_x, lru_b_x,
             lru_lambda, w_o_attn, w_o_lru, w_out, ln1_g, ln1_b, w_router, b_router,
             w_gate, w_up, w_down, ws_gate, ws_up, ws_down, ln2_g, ln2_b):
    n_p, seq, dm = x_prompt.shape
    n_s, t_s, _ = x_sample.shape
    assert t_s == 1 and cfg.depth == 1 and w_in.shape[0] == 1
    past = page_table.shape[1] * cfg.page_size
    t_p = n_p * seq
    t_all = -(-(t_p + n_s) // cfg.row_align) * cfg.row_align
    pad = t_all - t_p - n_s
    x_all = jnp.concatenate([x_prompt.reshape(t_p, dm), x_sample.reshape(n_s, dm),
                             jnp.zeros((pad, dm), x_prompt.dtype)], axis=0)
    pos_all = jnp.concatenate([jnp.tile(jnp.arange(seq), n_p), jnp.full((n_s,), past), jnp.zeros((pad,), I32)])
    l = 0
    y_all, o = _layer(cfg, l, x_all, n_p, seq, n_s, pos_all, cache_kv[l], cache_idx_k[l], page_table,
                      state_conv[l], state_h[l], w_in[l], idx_k_norm_g[l], idx_k_norm_b[l],
                      conv_w[l], conv_b[l], lru_w_a[l], lru_b_a[l], lru_w_x[l], lru_b_x[l], lru_lambda[l],
                      w_o_attn[l], w_o_lru[l], w_out[l], ln1_g[l], ln1_b[l], w_router[l], b_router[l],
                      w_gate[l], w_up[l], w_down[l], ws_gate[l], ws_up[l], ws_down[l], ln2_g[l], ln2_b[l])
    y_p = y_all[:t_p].reshape(n_p, seq, dm)
    y_s = y_all[t_p:t_p + n_s].reshape(n_s, 1, dm)
    lead = lambda a: a[None]
    return (y_p, y_s, lead(o["kv_p"]), lead(o["ik_p"]), lead(o["cv_p"]), lead(o["h_p"]),
            lead(o["kv_s"]), lead(o["ik_s"]), lead(o["cv_s"]), lead(o["h_s"]))


def kernel(x_prompt, x_sample, cache_kv, cache_idx_k, page_table, state_conv, state_h, w_in, idx_k_norm_g, idx_k_norm_b, conv_w, conv_b, lru_w_a, lru_b_a, lru_w_x, lru_b_x, lru_lambda, w_o_attn, w_o_lru, w_out, ln1_g, ln1_b, w_router, b_router, w_gate, w_up, w_down, ws_gate, ws_up, ws_down, ln2_g, ln2_b):
    return _forward(Cfg(), x_prompt, x_sample, cache_kv, cache_idx_k, page_table, state_conv, state_h,
                    w_in, idx_k_norm_g, idx_k_norm_b, conv_w, conv_b, lru_w_a, lru_b_a, lru_w_x, lru_b_x,
                    lru_lambda, w_o_attn, w_o_lru, w_out, ln1_g, ln1_b, w_router, b_router,
                    w_gate, w_up, w_down, ws_gate, ws_up, ws_down, ln2_g, ln2_b)
```

```python
import functools
from typing import NamedTuple

import jax
import jax.numpy as jnp
import numpy as np
from jax import lax
from jax.experimental import pallas as pl
from jax.experimental.pallas import tpu as pltpu

F32 = jnp.float32
BF16 = jnp.bfloat16
I32 = jnp.int32

LANE = 128
SUBLANE = 8
VMEM_LIMIT = 56 * 1024 * 1024
MASKED = -1e30
INT_MIN = -2 ** 31


class Cfg(NamedTuple):
    n_heads: int = 16
    n_kv_heads: int = 4
    head_dim: int = 128
    idx_heads: int = 32
    idx_dim: int = 128
    topk_max: int = 256
    q_block: int = 128
    rope_theta: float = 10000.0
    lru_blocks: int = 16
    conv_width: int = 4
    lru_c: float = 8.0
    n_experts: int = 64
    top_k: int = 8
    n_groups: int = 8
    topk_groups: int = 4
    route_scale: float = 2.5
    ln_eps: float = 1e-5
    page_size: int = 128
    depth: int = 1
    key_chunk: int = 512
    moe_tile: int = 512
    row_align: int = 256


def _pick(dim, prefs):
    for p in prefs:
        if p <= dim and dim % p == 0:
            return p
    return dim


def _params(*sem):
    return pltpu.CompilerParams(dimension_semantics=sem, vmem_limit_bytes=VMEM_LIMIT)


def _mm_body(x_ref, w_ref, o_ref):
    o_ref[...] = jnp.dot(x_ref[...], w_ref[...], preferred_element_type=F32).astype(o_ref.dtype)


def _matmul(x, w, col0=0, n=None, out_dtype=F32, name="matmul"):
    m, k = x.shape
    n = w.shape[1] - col0 if n is None else n
    tm = _pick(m, (1408, 1024, 768, 640, 512, 256))
    tn = _pick(n, (512, 384, 256, 128))
    assert col0 % tn == 0
    j0 = col0 // tn
    return pl.pallas_call(
        _mm_body,
        grid=(m // tm, n // tn),
        in_specs=[pl.BlockSpec((tm, k), lambda i, j: (i, 0)),
                  pl.BlockSpec((k, tn), lambda i, j: (0, j0 + j))],
        out_specs=pl.BlockSpec((tm, tn), lambda i, j: (i, j)),
        out_shape=jax.ShapeDtypeStruct((m, n), out_dtype),
        compiler_params=_params("arbitrary", "arbitrary"),
        name=name,
    )(x, w)


def _rope(x, cosf, sinf):
    return x * cosf + pltpu.roll(x, x.shape[-1] // 2, axis=x.ndim - 1) * sinf


def _rope_tables(pos, dim, theta):
    half = dim // 2
    inv = theta ** (-jnp.arange(half, dtype=F32) / half)
    ang = pos.astype(F32)[:, None] * inv[None, :]
    cos, sin = jnp.cos(ang), jnp.sin(ang)
    return jnp.concatenate([cos, cos], -1), jnp.concatenate([-sin, sin], -1)


def _layer_norm(y, g, b, eps):
    mu = jnp.mean(y, axis=-1, keepdims=True)
    yc = y - mu
    var = jnp.mean(yc * yc, axis=-1, keepdims=True)
    return yc * lax.rsqrt(var + eps) * g + b


def _kv_body(kv_ref, ki_ref, cos_ref, sin_ref, g_ref, b_ref,
             kvo_ref, kio_ref, kbf_ref, kibf_ref, *, cfg):
    c_heads, d = cfg.n_kv_heads, cfg.head_dim
    cosf, sinf = cos_ref[...], sin_ref[...]
    kv = kv_ref[...]
    for c in range(c_heads):
        kc = _rope(kv[:, c * d:(c + 1) * d], cosf, sinf)
        kvo_ref[:, c * d:(c + 1) * d] = kc
        kbf_ref[:, c * d:(c + 1) * d] = kc.astype(BF16)
    kvo_ref[:, c_heads * d:] = kv[:, c_heads * d:]
    ki = _rope(_layer_norm(ki_ref[...], g_ref[...], b_ref[...], cfg.ln_eps), cosf, sinf)
    kio_ref[...] = ki
    kibf_ref[...] = ki.astype(BF16)


def _vt_body(v_ref, vt_ref, *, cfg):
    d = cfg.head_dim
    for c in range(cfg.n_kv_heads):
        vt_ref[0, c * d:(c + 1) * d, :] = v_ref[:, c * d:(c + 1) * d].T.astype(BF16)


def _transpose_values(z_qkv, t_p, cfg):
    kc = cfg.key_chunk
    kvw = cfg.n_kv_heads * cfg.head_dim
    v_blk = (cfg.n_heads * cfg.head_dim + kvw) // kvw
    assert t_p % kc == 0
    return pl.pallas_call(
        functools.partial(_vt_body, cfg=cfg),
        grid=(t_p // kc,),
        in_specs=[pl.BlockSpec((kc, kvw), lambda i: (i, v_blk))],
        out_specs=pl.BlockSpec((1, kvw, kc), lambda i: (i, 0, 0)),
        out_shape=jax.ShapeDtypeStruct((t_p // kc, kvw, kc), BF16),
        compiler_params=_params("arbitrary"),
        name="transpose_values",
    )(z_qkv)


def _finalize_keys(z_qkv, z_i, cosf, sinf, ik_g, ik_b, cfg):
    t_all = z_qkv.shape[0]
    c_heads, d = cfg.n_kv_heads, cfg.head_dim
    kvw = c_heads * d
    qw = cfg.n_heads * d
    tr = cfg.row_align
    assert t_all % tr == 0 and qw % (2 * kvw) == 0
    ki_blk = (cfg.idx_heads * cfg.idx_dim) // cfg.idx_dim
    return pl.pallas_call(
        functools.partial(_kv_body, cfg=cfg),
        grid=(t_all // tr,),
        in_specs=[pl.BlockSpec((tr, 2 * kvw), lambda i: (i, qw // (2 * kvw))),
                  pl.BlockSpec((tr, cfg.idx_dim), lambda i: (i, ki_blk)),
                  pl.BlockSpec((tr, d), lambda i: (i, 0)),
                  pl.BlockSpec((tr, d), lambda i: (i, 0)),
                  pl.BlockSpec((1, cfg.idx_dim), lambda i: (0, 0)),
                  pl.BlockSpec((1, cfg.idx_dim), lambda i: (0, 0))],
        out_specs=[pl.BlockSpec((tr, 2 * kvw), lambda i: (i, 0)),
                   pl.BlockSpec((tr, cfg.idx_dim), lambda i: (i, 0)),
                   pl.BlockSpec((tr, kvw), lambda i: (i, 0)),
                   pl.BlockSpec((tr, cfg.idx_dim), lambda i: (i, 0))],
        out_shape=[jax.ShapeDtypeStruct((t_all, 2 * kvw), F32),
                   jax.ShapeDtypeStruct((t_all, cfg.idx_dim), F32),
                   jax.ShapeDtypeStruct((t_all, kvw), BF16),
                   jax.ShapeDtypeStruct((t_all, cfg.idx_dim), BF16)],
        compiler_params=_params("arbitrary"),
        name="finalize_keys",
    )(z_qkv, z_i, cosf, sinf, ik_g, ik_b)


def _sort_key(s):
    b = pltpu.bitcast(s, I32)
    return b ^ ((b >> 31) & jnp.int32(0x7FFFFFFF))


def _kth_largest_key(count_ge, k, width):
    def bit_step(i, ans):
        cand = ans | lax.shift_left(jnp.int32(1), jnp.int32(31) - i)
        cnt = count_ge(cand ^ jnp.int32(INT_MIN))
        return jnp.where(cnt >= k, cand, ans)

    ans = lax.fori_loop(0, 32, bit_step, jnp.zeros((1, width), I32))
    return ans ^ jnp.int32(INT_MIN)


def _attn_body(q_ref, qi_ref, wi_ref, cos_ref, sin_ref, kbf_ref, kibf_ref, vt_ref,
               o_ref, qs_ref, qis_ref, key_ref, m_ref, l_ref, acc_ref, *, cfg, topk):
    qb_idx = pl.program_id(1)
    qb, kc = cfg.q_block, cfg.key_chunk
    d, di = cfg.head_dim, cfg.idx_dim
    c_heads = cfg.n_kv_heads
    g_heads = cfg.n_heads // c_heads
    cosf, sinf = cos_ref[...], sin_ref[...]

    for h in range(cfg.n_heads):
        qs_ref[h * qb:(h + 1) * qb, :] = _rope(q_ref[:, h * d:(h + 1) * d], cosf, sinf).astype(BF16)
    for h in range(cfg.idx_heads):
        qis_ref[h * qb:(h + 1) * qb, :] = _rope(qi_ref[:, h * di:(h + 1) * di], cosf, sinf).astype(BF16)
    w_t = wi_ref[...].T * F32((di * cfg.idx_heads) ** -0.5)

    q0 = qb_idx * qb
    n_chunks = (q0 + qb + kc - 1) // kc
    tpos = q0 + lax.broadcasted_iota(I32, (kc, qb), 1)
    krow = lax.broadcasted_iota(I32, (kc, qb), 0)
    nt = (((1,), (1,)), ((), ()))

    def score_chunk(ci, carry):
        k0 = pl.multiple_of(ci * kc, kc)
        ki_c = kibf_ref[pl.ds(k0, kc), :]
        acc = jnp.zeros((kc, qb), F32)
        for hp in range(cfg.idx_heads // 2):
            z = lax.dot_general(ki_c, qis_ref[hp * 2 * qb:(hp + 1) * 2 * qb, :], nt,
                                preferred_element_type=F32)
            acc = acc + jnp.maximum(z[:, :qb], 0.0) * w_t[2 * hp:2 * hp + 1, :]
            acc = acc + jnp.maximum(z[:, qb:], 0.0) * w_t[2 * hp + 1:2 * hp + 2, :]
        causal = (k0 + krow) <= tpos
        key_ref[pl.ds(k0, kc), :] = _sort_key(jnp.where(causal, acc, -jnp.inf))
        return carry

    lax.fori_loop(0, n_chunks, score_chunk, 0)

    def count_ge(cand):
        def cnt_chunk(ci, acc8):
            k0 = pl.multiple_of(ci * kc, kc)
            hit = jnp.where(key_ref[pl.ds(k0, kc), :] >= cand, 1, 0).astype(I32)
            return acc8 + jnp.sum(hit.reshape(kc // SUBLANE, SUBLANE, qb), axis=0)
        acc8 = lax.fori_loop(0, n_chunks, cnt_chunk, jnp.zeros((SUBLANE, qb), I32))
        return jnp.sum(acc8, axis=0, keepdims=True)

    thr = _kth_largest_key(count_ge, topk, qb)

    m_ref[...] = jnp.full(m_ref.shape, MASKED, F32)
    l_ref[...] = jnp.zeros(l_ref.shape, F32)
    acc_ref[...] = jnp.zeros(acc_ref.shape, F32)
    scale = F32(d ** -0.5)

    def attend_chunk(ci, carry):
        k0 = pl.multiple_of(ci * kc, kc)
        keep = (key_ref[pl.ds(k0, kc), :] >= thr) & ((k0 + krow) <= tpos)
        bias = jnp.where(keep, 0.0, MASKED).astype(F32)
        bias = jnp.concatenate([bias] * g_heads, axis=1)
        for c in range(c_heads):
            k_c = kbf_ref[pl.ds(k0, kc), c * d:(c + 1) * d]
            s = lax.dot_general(k_c, qs_ref[c * g_heads * qb:(c + 1) * g_heads * qb, :], nt,
                                preferred_element_type=F32) * scale + bias
            m_old = m_ref[c:c + 1, :]
            m_new = jnp.maximum(m_old, jnp.max(s, axis=0, keepdims=True))
            alpha = jnp.exp(m_old - m_new)
            p = jnp.exp(s - m_new)
            l_ref[c:c + 1, :] = alpha * l_ref[c:c + 1, :] + jnp.sum(p, axis=0, keepdims=True)
            pv = jnp.dot(vt_ref[ci, c * d:(c + 1) * d, :], p.astype(BF16), preferred_element_type=F32)
            acc_ref[c * d:(c + 1) * d, :] = alpha * acc_ref[c * d:(c + 1) * d, :] + pv
            m_ref[c:c + 1, :] = m_new
        return carry

    lax.fori_loop(0, n_chunks, attend_chunk, 0)

    for c in range(c_heads):
        o_t = acc_ref[c * d:(c + 1) * d, :] / l_ref[c:c + 1, :]
        for g in range(g_heads):
            h = c * g_heads + g
            o_ref[:, h * d:(h + 1) * d] = o_t[:, g * qb:(g + 1) * qb].T.astype(o_ref.dtype)


def _prompt_attention(z_qkv, z_i, z_w, cosf, sinf, k_bf, ki_bf, v_t, n_seq, seq, cfg):
    qb, kc = cfg.q_block, cfg.key_chunk
    d = cfg.head_dim
    qw = cfg.n_heads * d
    kvw = cfg.n_kv_heads * d
    iw = cfg.idx_heads * cfg.idx_dim
    assert seq % kc == 0 and seq % qb == 0 and kc % qb == 0
    assert cfg.head_dim == LANE and cfg.idx_dim == LANE
    nqb = seq // qb
    topk = min(cfg.topk_max, seq // 4)
    row = lambda n, j: (n * nqb + j, 0)
    return pl.pallas_call(
        functools.partial(_attn_body, cfg=cfg, topk=topk),
        grid=(n_seq, nqb),
        in_specs=[pl.BlockSpec((qb, qw), row),
                  pl.BlockSpec((qb, iw), row),
                  pl.BlockSpec((qb, LANE), row),
                  pl.BlockSpec((qb, d), row),
                  pl.BlockSpec((qb, d), row),
                  pl.BlockSpec((seq, kvw), lambda n, j: (n, 0)),
                  pl.BlockSpec((seq, cfg.idx_dim), lambda n, j: (n, 0)),
                  pl.BlockSpec((seq // kc, kvw, kc), lambda n, j: (n, 0, 0))],
        out_specs=pl.BlockSpec((qb, qw), row),
        out_shape=jax.ShapeDtypeStruct((n_seq * seq, qw), BF16),
        scratch_shapes=[pltpu.VMEM((cfg.n_heads * qb, d), BF16),
                        pltpu.VMEM((cfg.idx_heads * qb, cfg.idx_dim), BF16),
                        pltpu.VMEM((seq, qb), I32),
                        pltpu.VMEM((SUBLANE, (cfg.n_heads // cfg.n_kv_heads) * qb), F32),
                        pltpu.VMEM((SUBLANE, (cfg.n_heads // cfg.n_kv_heads) * qb), F32),
                        pltpu.VMEM((kvw, (cfg.n_heads // cfg.n_kv_heads) * qb), F32)],
        compiler_params=_params("arbitrary", "arbitrary"),
        name="prompt_attention",
    )(z_qkv, z_i, z_w, cosf, sinf, k_bf, ki_bf, v_t)


def _softplus(x):
    return jnp.maximum(x, 0.0) + jnp.log1p(jnp.exp(-jnp.abs(x)))


def _lru_gates(xc, wa_ref, wx_ref, ba, bx, lam, cfg):
    w = xc.shape[1]
    bd = w // cfg.lru_blocks
    xb = xc.astype(BF16)
    r_parts, i_parts = [], []
    for k in range(cfg.lru_blocks):
        xk = xb[:, k * bd:(k + 1) * bd]
        r_parts.append(jnp.dot(xk, wa_ref[k], preferred_element_type=F32))
        i_parts.append(jnp.dot(xk, wx_ref[k], preferred_element_type=F32))
    r = jax.nn.sigmoid(jnp.concatenate(r_parts, axis=1) + ba)
    gate_i = jax.nn.sigmoid(jnp.concatenate(i_parts, axis=1) + bx)
    log_a = (-cfg.lru_c * r) * _softplus(-lam)
    a = jnp.exp(log_a)
    u = jnp.sqrt(-jnp.tanh(log_a) * (a * a + 1.0)) * (gate_i * xc)
    return a, u


def _lru_prompt_body(xl_ref, cw_ref, cb_ref, wa_ref, wx_ref, ba_ref, bx_ref, lam_ref,
                     h_ref, hlast_ref, prev_ref, carry_ref, *, cfg):
    j = pl.program_id(1)
    tb = xl_ref.shape[0]
    cw = cfg.conv_width

    @pl.when(j == 0)
    def _():
        prev_ref[...] = jnp.zeros(prev_ref.shape, F32)
        carry_ref[...] = jnp.zeros(carry_ref.shape, F32)

    xl = xl_ref[...]
    ext = jnp.concatenate([prev_ref[...], xl], axis=0)
    off = SUBLANE - (cw - 1)
    xc = cb_ref[...] + ext[off:off + tb] * cw_ref[0:1, :]
    for t in range(1, cw):
        xc = xc + ext[off + t:off + t + tb] * cw_ref[t:t + 1, :]
    prev_ref[...] = xl[tb - SUBLANE:, :]

    a, u = _lru_gates(xc, wa_ref, wx_ref, ba_ref[...], bx_ref[...], lam_ref[...], cfg)

    row = lax.broadcasted_iota(I32, a.shape, 0) & (SUBLANE - 1)
    s = 1
    while s < SUBLANE:
        ok = row >= s
        a_sh = jnp.where(ok, pltpu.roll(a, s, axis=0), 1.0)
        u_sh = jnp.where(ok, pltpu.roll(u, s, axis=0), 0.0)
        u = u + a * u_sh
        a = a * a_sh
        s *= 2
    h_prev = carry_ref[...]
    for gi in range(tb // SUBLANE):
        sl = slice(gi * SUBLANE, (gi + 1) * SUBLANE)
        h_rows = u[sl] + a[sl] * h_prev
        h_ref[sl, :] = h_rows.astype(h_ref.dtype)
        h_prev = h_rows[SUBLANE - 1:SUBLANE, :]
    carry_ref[...] = h_prev
    hlast_ref[0] = h_prev


def _lru_prompt(z_b, conv_w, conv_b, wa, wx, ba, bx, lam, n_seq, seq, cfg):
    w = conv_w.shape[1]
    tb = _pick(seq, (256, 128))
    nb = seq // tb
    bd = w // cfg.lru_blocks
    vec = pl.BlockSpec((1, w), lambda n, j: (0, 0))
    return pl.pallas_call(
        functools.partial(_lru_prompt_body, cfg=cfg),
        grid=(n_seq, nb),
        in_specs=[pl.BlockSpec((tb, w), lambda n, j: (n * nb + j, 0)),
                  pl.BlockSpec((cfg.conv_width, w), lambda n, j: (0, 0)),
                  vec,
                  pl.BlockSpec((cfg.lru_blocks, bd, bd), lambda n, j: (0, 0, 0)),
                  pl.BlockSpec((cfg.lru_blocks, bd, bd), lambda n, j: (0, 0, 0)),
                  vec, vec, vec],
        out_specs=[pl.BlockSpec((tb, w), lambda n, j: (n * nb + j, 0)),
                   pl.BlockSpec((1, 1, w), lambda n, j: (n, 0, 0))],
        out_shape=[jax.ShapeDtypeStruct((n_seq * seq, w), BF16),
                   jax.ShapeDtypeStruct((n_seq, 1, w), F32)],
        scratch_shapes=[pltpu.VMEM((SUBLANE, w), F32), pltpu.VMEM((1, w), F32)],
        compiler_params=_params("arbitrary", "arbitrary"),
        name="lru_prompt",
    )(z_b, conv_w, conv_b, wa, wx, ba, bx, lam)


def _lru_sample_body(xl_ref, hist_ref, h0_ref, cw_ref, cb_ref, wa_ref, wx_ref, ba_ref, bx_ref, lam_ref,
                     h_ref, hbf_ref, *, cfg):
    cw = cfg.conv_width
    xc = cb_ref[...] + hist_ref[0] * cw_ref[0:1, :]
    for t in range(1, cw - 1):
        xc = xc + hist_ref[t] * cw_ref[t:t + 1, :]
    xc = xc + xl_ref[...] * cw_ref[cw - 1:cw, :]
    a, u = _lru_gates(xc, wa_ref, wx_ref, ba_ref[...], bx_ref[...], lam_ref[...], cfg)
    h = a * h0_ref[...] + u
    h_ref[...] = h
    hbf_ref[...] = h.astype(BF16)


def _lru_sample(xl, hist, h0, conv_w, conv_b, wa, wx, ba, bx, lam, cfg):
    n, w = xl.shape
    return pl.pallas_call(
        functools.partial(_lru_sample_body, cfg=cfg),
        out_shape=[jax.ShapeDtypeStruct((n, w), F32), jax.ShapeDtypeStruct((n, w), BF16)],
        compiler_params=pltpu.CompilerParams(vmem_limit_bytes=VMEM_LIMIT),
        name="lru_sample",
    )(xl, hist, h0, conv_w, conv_b, wa, wx, ba, bx, lam)


def _sample_scores_body(pt_ref, qi_ref, w_ref, kin_ref, cos_ref, sin_ref, *rest, cfg, pages_per_step):
    page_refs, o_ref = rest[:pages_per_step], rest[pages_per_step]
    j = pl.program_id(1)
    nj = pl.num_programs(1)
    cosf, sinf = cos_ref[0], sin_ref[0]
    qi = _rope(qi_ref[0], cosf, sinf)
    qi_b = qi.astype(BF16)
    w = w_ref[0] * F32((cfg.idx_dim * cfg.idx_heads) ** -0.5)
    nt = (((1,), (1,)), ((), ()))
    rows = []
    for p in range(pages_per_step):
        z = lax.dot_general(qi_b, page_refs[p][0].astype(BF16), nt, preferred_element_type=F32)
        rows.append(jnp.sum(jnp.maximum(z, 0.0) * w, axis=0, keepdims=True))
    r0 = pl.multiple_of(j * pages_per_step, pages_per_step)
    o_ref[0, pl.ds(r0, pages_per_step), :] = jnp.concatenate(rows, axis=0)

    @pl.when(j == nj - 1)
    def _():
        z_new = jnp.sum(qi * kin_ref[0], axis=1, keepdims=True)
        s_new = jnp.sum(jnp.maximum(z_new, 0.0) * w, axis=0, keepdims=True)
        lane = lax.broadcasted_iota(I32, (SUBLANE, LANE), 1)
        sub = lax.broadcasted_iota(I32, (SUBLANE, LANE), 0)
        tail = jnp.where((lane == 0) & (sub == 0), jnp.broadcast_to(s_new, (SUBLANE, LANE)), -jnp.inf)
        o_ref[0, pl.ds(nj * pages_per_step, SUBLANE), :] = tail


def _sample_scores(page_table, qi, w_rep, ki_new, cos_s, sin_s, cache_idx_k, cfg):
    n, n_pages = page_table.shape
    pps = _pick(n_pages, (SUBLANE,))
    assert cfg.page_size == LANE
    hi, di = cfg.idx_heads, cfg.idx_dim
    per_seq = lambda b, j, pt: (b, 0, 0)
    page_specs = [pl.BlockSpec((1, cfg.page_size, di),
                               functools.partial(lambda b, j, pt, p: (pt[b, j * pps + p], 0, 0), p=p))
                  for p in range(pps)]
    grid_spec = pltpu.PrefetchScalarGridSpec(
        num_scalar_prefetch=1,
        grid=(n, n_pages // pps),
        in_specs=[pl.BlockSpec((1, hi, di), per_seq),
                  pl.BlockSpec((1, hi, LANE), per_seq),
                  pl.BlockSpec((1, 1, di), per_seq),
                  pl.BlockSpec((1, 1, di), per_seq),
                  pl.BlockSpec((1, 1, di), per_seq)] + page_specs,
        out_specs=pl.BlockSpec((1, n_pages + SUBLANE, LANE), per_seq),
    )
    return pl.pallas_call(
        functools.partial(_sample_scores_body, cfg=cfg, pages_per_step=pps),
        grid_spec=grid_spec,
        out_shape=jax.ShapeDtypeStruct((n, n_pages + SUBLANE, LANE), F32),
        compiler_params=_params("arbitrary", "arbitrary"),
        name="sample_scores",
    )(page_table, qi, w_rep, ki_new, cos_s, sin_s, *([cache_idx_k] * pps))


def _sample_attend_body(pt_ref, s_ref, q_ref, kvn_ref, cos_ref, sin_ref, *rest,
                        cfg, pages_per_step, topk):
    page_refs = rest[:pages_per_step]
    o_ref, qs_ref, keep_ref, m_ref, l_ref, acc_ref = rest[pages_per_step:]
    j = pl.program_id(1)
    nj = pl.num_programs(1)
    nh, d = cfg.n_heads, cfg.head_dim
    c_heads = cfg.n_kv_heads
    g_heads = nh // c_heads
    ps = cfg.page_size
    n_rows = s_ref.shape[1]
    scale = F32(d ** -0.5)
    head_c = lax.broadcasted_iota(I32, (nh, 1), 0) // g_heads

    @pl.when(j == 0)
    def _():
        keys = _sort_key(s_ref[0])

        def count_ge(cand):
            hit = jnp.where(keys >= cand, 1, 0).astype(I32)
            return jnp.sum(jnp.sum(hit, axis=0, keepdims=True), axis=1, keepdims=True)

        thr = _kth_largest_key(count_ge, topk, 1)
        finite = s_ref[0] > -jnp.inf
        keep_ref[...] = jnp.where((keys >= thr) & finite, 0.0, MASKED).astype(F32)
        qs_ref[...] = _rope(q_ref[0], cos_ref[0], sin_ref[0]).astype(BF16)
        m_ref[...] = jnp.full(m_ref.shape, MASKED, F32)
        l_ref[...] = jnp.zeros(l_ref.shape, F32)
        acc_ref[...] = jnp.zeros(acc_ref.shape, F32)

    nt = (((1,), (1,)), ((), ()))
    qs = qs_ref[...]
    per = 2 * c_heads
    width = ps * per
    r0 = pl.multiple_of(j * pages_per_step, pages_per_step)
    keep_rows = keep_ref[pl.ds(r0, pages_per_step), :]
    sel_rows = jnp.where(keep_rows == 0.0, 1.0, 0.0).astype(BF16)
    spread = (lax.broadcasted_iota(I32, (ps, width), 1) // per
              == lax.broadcasted_iota(I32, (ps, width), 0))
    sel_cols = jnp.dot(sel_rows, jnp.where(spread, 1.0, 0.0).astype(BF16),
                       preferred_element_type=F32)
    own_key = (lax.broadcasted_iota(I32, (nh, width), 1) % per) == head_c
    m_run, l_run, acc = m_ref[...], l_ref[...], acc_ref[...]
    for p in range(pages_per_step):
        rows = page_refs[p][0].astype(BF16)
        s = lax.dot_general(qs, rows, nt, preferred_element_type=F32) * scale
        s = jnp.where(own_key & (sel_cols[p:p + 1, :] > 0.5), s, MASKED)
        m_new = jnp.maximum(m_run, jnp.max(s, axis=1, keepdims=True))
        alpha = jnp.exp(m_run - m_new)
        p_un = jnp.exp(s - m_new)
        l_run = alpha * l_run + jnp.sum(p_un, axis=1, keepdims=True)
        p_val = pltpu.roll(p_un, c_heads, axis=1).astype(BF16)
        acc = alpha * acc + jnp.dot(p_val, rows, preferred_element_type=F32)
        m_run = m_new
    m_ref[...], l_ref[...], acc_ref[...] = m_run, l_run, acc

    @pl.when(j == nj - 1)
    def _():
        kvn = kvn_ref[0]
        k_new = jnp.zeros((nh, d), F32)
        v_new = jnp.zeros((nh, d), F32)
        for c in range(c_heads):
            k_new = jnp.where(head_c == c, kvn[c:c + 1, :], k_new)
            v_new = jnp.where(head_c == c, kvn[c_heads + c:c_heads + c + 1, :], v_new)
        q_f = _rope(q_ref[0], cos_ref[0], sin_ref[0])
        bias_new = keep_ref[n_rows - SUBLANE:n_rows - SUBLANE + 1, 0:1]
        s_new = jnp.sum(q_f * k_new, axis=1, keepdims=True) * scale + bias_new
        m_o = m_ref[...]
        m_n = jnp.maximum(m_o, s_new)
        al = jnp.exp(m_o - m_n)
        p_new = jnp.exp(s_new - m_n)
        l_fin = al * l_ref[...] + p_new
        acc_fin = al * acc_ref[...] + p_new * v_new
        o_ref[0] = (acc_fin / l_fin).astype(o_ref.dtype)


def _sample_attend(page_table, scores, q, kv_new, cos_s, sin_s, cache_kv, cfg):
    n, n_pages = page_table.shape
    pps = _pick(n_pages, (SUBLANE,))
    nh, d = cfg.n_heads, cfg.head_dim
    kvw2 = 2 * cfg.n_kv_heads * d
    topk = min(cfg.topk_max, (n_pages * cfg.page_size + 1) // 4)
    per_seq = lambda b, j, pt: (b, 0, 0)
    page_specs = [pl.BlockSpec((1, cfg.page_size * 2 * cfg.n_kv_heads, d),
                               functools.partial(lambda b, j, pt, p: (pt[b, j * pps + p], 0, 0), p=p))
                  for p in range(pps)]
    n_rows = scores.shape[1]
    grid_spec = pltpu.PrefetchScalarGridSpec(
        num_scalar_prefetch=1,
        grid=(n, n_pages // pps),
        in_specs=[pl.BlockSpec((1, n_rows, LANE), per_seq),
                  pl.BlockSpec((1, nh, d), per_seq),
                  pl.BlockSpec((1, 2 * cfg.n_kv_heads, d), per_seq),
                  pl.BlockSpec((1, 1, d), per_seq),
                  pl.BlockSpec((1, 1, d), per_seq)] + page_specs,
        out_specs=pl.BlockSpec((1, nh, d), per_seq),
        scratch_shapes=[pltpu.VMEM((nh, d), BF16),
                        pltpu.VMEM((n_rows, LANE), F32),
                        pltpu.VMEM((nh, 1), F32),
                        pltpu.VMEM((nh, 1), F32),
                        pltpu.VMEM((nh, d), F32)],
    )
    return pl.pallas_call(
        functools.partial(_sample_attend_body, cfg=cfg, pages_per_step=pps, topk=topk),
        grid_spec=grid_spec,
        out_shape=jax.ShapeDtypeStruct((n, nh, d), BF16),
        compiler_params=_params("arbitrary", "arbitrary"),
        name="sample_attend",
    )(page_table, scores, q, kv_new, cos_s, sin_s, *([cache_kv] * pps))


def _merge_body(oa_ref, hl_ref, woa_ref, wol_ref, ga_ref, gb_ref, o_ref):
    ya = jnp.dot(oa_ref[...], woa_ref[...], preferred_element_type=F32)
    yl = jnp.dot(hl_ref[...], wol_ref[...], preferred_element_type=F32)
    o_ref[...] = (jax.nn.sigmoid(ga_ref[...]) * ya + jax.nn.sigmoid(gb_ref[...]) * yl).astype(o_ref.dtype)


def _merge(o_attn, h_lru, w_oa, w_ol, z_b, lru_w, cfg):
    m, aw = o_attn.shape
    dm = w_oa.shape[1]
    tm = _pick(m, (1408, 1024, 768, 640, 512, 256))
    tn = _pick(dm, (512, 256, 128))
    assert lru_w % tn == 0
    ga_off = lru_w // tn
    gb_off = (lru_w + dm) // tn
    return pl.pallas_call(
        _merge_body,
        grid=(m // tm, dm // tn),
        in_specs=[pl.BlockSpec((tm, aw), lambda i, j: (i, 0)),
                  pl.BlockSpec((tm, lru_w), lambda i, j: (i, 0)),
                  pl.BlockSpec((aw, tn), lambda i, j: (0, j)),
                  pl.BlockSpec((lru_w, tn), lambda i, j: (0, j)),
                  pl.BlockSpec((tm, tn), lambda i, j: (i, ga_off + j)),
                  pl.BlockSpec((tm, tn), lambda i, j: (i, gb_off + j))],
        out_specs=pl.BlockSpec((tm, tn), lambda i, j: (i, j)),
        out_shape=jax.ShapeDtypeStruct((m, dm), BF16),
        compiler_params=_params("arbitrary", "arbitrary"),
        name="merge_mixers",
    )(o_attn, h_lru, w_oa, w_ol, z_b, z_b)


def _proj_ln_body(a_ref, w_ref, x_ref, g_ref, b_ref, o_ref, obf_ref, *, alpha, eps, tn):
    j = pl.program_id(1)
    c0 = pl.multiple_of(j * tn, tn)
    y = jnp.dot(a_ref[...], w_ref[...], preferred_element_type=F32)
    o_ref[:, pl.ds(c0, tn)] = F32(alpha) * x_ref[...] + y

    @pl.when(j == pl.num_programs(1) - 1)
    def _():
        out = _layer_norm(o_ref[...], g_ref[...], b_ref[...], eps)
        o_ref[...] = out
        obf_ref[...] = out.astype(BF16)


def _proj_residual_ln(a, w, x, g, b, alpha, cfg):
    m, k = a.shape
    dm = w.shape[1]
    tm = _pick(m, (384, 256))
    tn = _pick(dm, (512, 256, 128))
    return pl.pallas_call(
        functools.partial(_proj_ln_body, alpha=alpha, eps=cfg.ln_eps, tn=tn),
        grid=(m // tm, dm // tn),
        in_specs=[pl.BlockSpec((tm, k), lambda i, j: (i, 0)),
                  pl.BlockSpec((k, tn), lambda i, j: (0, j)),
                  pl.BlockSpec((tm, tn), lambda i, j: (i, j)),
                  pl.BlockSpec((1, dm), lambda i, j: (0, 0)),
                  pl.BlockSpec((1, dm), lambda i, j: (0, 0))],
        out_specs=[pl.BlockSpec((tm, dm), lambda i, j: (i, 0)),
                   pl.BlockSpec((tm, dm), lambda i, j: (i, 0))],
        out_shape=[jax.ShapeDtypeStruct((m, dm), F32), jax.ShapeDtypeStruct((m, dm), BF16)],
        compiler_params=_params("arbitrary", "arbitrary"),
        name="proj_residual_ln",
    )(a, w, x, g, b)


def _first_index_of_max(v, idx, big):
    m = jnp.max(v, axis=0, keepdims=True)
    first = jnp.min(jnp.where(v == m, idx, big), axis=0, keepdims=True)
    return m, first


def _router_body(x_ref, wr_ref, br_ref, ids_ref, gate_ref, rank_ref, cnt_ref, *, cfg, n_tok):
    i = pl.program_id(0)
    e, ng = cfg.n_experts, cfg.n_groups
    per = e // ng
    tm = x_ref.shape[0]
    nt = (((1,), (1,)), ((), ()))
    logits = lax.dot_general(wr_ref[...], x_ref[...], nt, preferred_element_type=F32)
    s = jax.nn.sigmoid(logits)
    choice = s + br_ref[...]
    eidx = lax.broadcasted_iota(I32, (e, tm), 0)

    grp_rows = []
    jidx = lax.broadcasted_iota(I32, (per, tm), 0)
    for g in range(ng):
        cg = choice[g * per:(g + 1) * per, :]
        m1, j1 = _first_index_of_max(cg, jidx, per)
        m2 = jnp.max(jnp.where(jidx == j1, -jnp.inf, cg), axis=0, keepdims=True)
        grp_rows.append(m1 + m2)
    grp = jnp.concatenate(grp_rows, axis=0)

    gidx = lax.broadcasted_iota(I32, (ng, tm), 0)
    grp_keep = jnp.zeros((ng, tm), F32)
    work = grp
    for _ in range(cfg.topk_groups):
        _, gsel = _first_index_of_max(work, gidx, ng)
        hit = gidx == gsel
        grp_keep = jnp.where(hit, 1.0, grp_keep)
        work = jnp.where(hit, -jnp.inf, work)

    keep_rows = [jnp.broadcast_to(grp_keep[g:g + 1, :], (per, tm)) for g in range(ng)]
    masked = jnp.where(jnp.concatenate(keep_rows, axis=0) > 0.5, choice, -jnp.inf)

    ids, wts, hits = [], [], []
    for _ in range(cfg.top_k):
        _, esel = _first_index_of_max(masked, eidx, e)
        hit = eidx == esel
        ids.append(esel)
        hits.append(hit)
        wts.append(jnp.sum(jnp.where(hit, s, 0.0), axis=0, keepdims=True))
        masked = jnp.where(hit, -jnp.inf, masked)
    wk = jnp.concatenate(wts, axis=0)
    ids_ref[...] = jnp.concatenate(ids, axis=0)
    gate_ref[...] = F32(cfg.route_scale) * wk / jnp.sum(wk, axis=0, keepdims=True)

    @pl.when(i == 0)
    def _():
        cnt_ref[...] = jnp.zeros(cnt_ref.shape, F32)

    tok = i * tm + lax.broadcasted_iota(I32, (e, tm), 1)
    picked = jnp.zeros((e, tm), F32)
    for h in hits:
        picked = jnp.where(h, 1.0, picked)
    picked = jnp.where(tok < n_tok, picked, 0.0)
    before = (lax.broadcasted_iota(I32, (tm, tm), 0) < lax.broadcasted_iota(I32, (tm, tm), 1))
    earlier = jnp.dot(picked.astype(BF16), jnp.where(before, 1.0, 0.0).astype(BF16),
                      preferred_element_type=F32)
    rank_all = cnt_ref[:, 0:1] + earlier
    rank_ref[...] = jnp.concatenate(
        [jnp.sum(jnp.where(h, rank_all, 0.0), axis=0, keepdims=True) for h in hits], axis=0).astype(I32)
    cnt_ref[...] = cnt_ref[...] + jnp.sum(picked, axis=1, keepdims=True)


def _router(x_bf, w_r_t, b_r, n_tok, cfg):
    m, dm = x_bf.shape
    tm = _pick(m, (256, 128))
    e = cfg.n_experts
    pick = pl.BlockSpec((cfg.top_k, tm), lambda i: (0, i))
    return pl.pallas_call(
        functools.partial(_router_body, cfg=cfg, n_tok=n_tok),
        grid=(m // tm,),
        in_specs=[pl.BlockSpec((tm, dm), lambda i: (i, 0)),
                  pl.BlockSpec((e, dm), lambda i: (0, 0)),
                  pl.BlockSpec((e, 1), lambda i: (0, 0))],
        out_specs=[pick, pick, pick, pl.BlockSpec((e, LANE), lambda i: (0, 0))],
        out_shape=[jax.ShapeDtypeStruct((cfg.top_k, m), I32),
                   jax.ShapeDtypeStruct((cfg.top_k, m), F32),
                   jax.ShapeDtypeStruct((cfg.top_k, m), I32),
                   jax.ShapeDtypeStruct((e, LANE), F32)],
        compiler_params=_params("arbitrary"),
        name="router",
    )(x_bf, w_r_t, b_r)


ROW_PAD = SUBLANE


def _start_row_gather(idx_row, idx_smem, idx_sem, src_hbm, buf, sem, n, sub):
    cp = pltpu.make_async_copy(idx_row, idx_smem, idx_sem)
    cp.start()
    cp.wait()

    def start(r, carry):
        r0 = pl.multiple_of(r * (sub + ROW_PAD), SUBLANE)
        pltpu.make_async_copy(src_hbm.at[idx_smem[0, r]], buf.at[pl.ds(r0, sub)], sem).start()
        return carry
    lax.fori_loop(0, n, start, 0)


def _wait_row_gather(buf, sem, n, sub):
    pltpu.make_async_copy(buf.at[pl.ds(0, n * sub)], buf.at[pl.ds(0, n * sub)], sem).wait()


def _gathered_slab(buf, first, count, j, sub):
    return buf[pl.ds(first * (sub + ROW_PAD) + j, count, stride=sub + ROW_PAD), :]


def _dispatch_body(nrows_ref, src_ref, x_hbm, o_ref, idx_smem, buf_ref, idx_sem, sems, *, tile):
    i = pl.program_id(0)
    sub = x_hbm.shape[1]
    n_valid = (nrows_ref[0] + tile - 1) // tile

    def start(t):
        slot = t % 2
        _start_row_gather(src_ref.at[t], idx_smem.at[pl.ds(slot, 1)], idx_sem, x_hbm,
                          buf_ref.at[slot], sems.at[slot], tile, sub)

    @pl.when(i == 0)
    def _():
        start(0)

    @pl.when(i + 1 < n_valid)
    def _():
        start(i + 1)

    @pl.when(i < n_valid)
    def _():
        slot = i % 2
        _wait_row_gather(buf_ref.at[slot], sems.at[slot], tile, sub)
        for j in range(sub):
            o_ref[:, j * LANE:(j + 1) * LANE] = _gathered_slab(buf_ref.at[slot], 0, tile, j, sub).astype(o_ref.dtype)


def _dispatch(x3, src_rows, n_rows, tile):
    r_max = src_rows.shape[0]
    sub = x3.shape[1]
    n_tiles = r_max // tile
    last = lambda i, nr: jnp.minimum(i, (nr[0] - 1) // tile)
    grid_spec = pltpu.PrefetchScalarGridSpec(
        num_scalar_prefetch=1,
        grid=(n_tiles,),
        in_specs=[pl.BlockSpec((n_tiles, 1, tile), lambda i, nr: (0, 0, 0)),
                  pl.BlockSpec(memory_space=pl.ANY)],
        out_specs=pl.BlockSpec((tile, sub * LANE), lambda i, nr: (last(i, nr), 0)),
        scratch_shapes=[pltpu.SMEM((2, tile), I32), pltpu.VMEM((2, tile * (sub + ROW_PAD), LANE), F32),
                        pltpu.SemaphoreType.DMA(()), pltpu.SemaphoreType.DMA((2,))],
    )
    return pl.pallas_call(
        functools.partial(_dispatch_body, tile=tile),
        grid_spec=grid_spec,
        out_shape=jax.ShapeDtypeStruct((r_max, sub * LANE), BF16),
        compiler_params=_params("arbitrary"),
        name="moe_dispatch",
    )(n_rows, src_rows.reshape(n_tiles, 1, tile), x3)


def _expert_up_body(te_ref, tf_ref, nt_ref, x_ref, wg_ref, wu_ref, o_ref, wg_s, wu_s):
    m = pl.program_id(1)

    @pl.when(tf_ref[m] == 1)
    def _():
        wg_s[...] = wg_ref[0].astype(BF16)
        wu_s[...] = wu_ref[0].astype(BF16)

    @pl.when(m < nt_ref[0])
    def _():
        x = x_ref[...]
        g = jnp.dot(x, wg_s[...], preferred_element_type=F32)
        u = jnp.dot(x, wu_s[...], preferred_element_type=F32)
        o_ref[...] = (jax.nn.silu(g) * u).astype(o_ref.dtype)


def _expert_up(xs, w_gate, w_up, tile_expert, tile_first, n_tiles, tile):
    r_max, dm = xs.shape
    e, _, f = w_gate.shape
    tf = _pick(f, (256, 128))
    n_mt = r_max // tile
    clamp = lambda m, nt: jnp.minimum(m, nt[0] - 1)
    grid_spec = pltpu.PrefetchScalarGridSpec(
        num_scalar_prefetch=3,
        grid=(f // tf, n_mt),
        in_specs=[pl.BlockSpec((tile, dm), lambda fi, m, te, tfst, nt: (clamp(m, nt), 0)),
                  pl.BlockSpec((1, dm, tf), lambda fi, m, te, tfst, nt: (te[m], 0, fi)),
                  pl.BlockSpec((1, dm, tf), lambda fi, m, te, tfst, nt: (te[m], 0, fi))],
        out_specs=pl.BlockSpec((tile, tf), lambda fi, m, te, tfst, nt: (clamp(m, nt), fi)),
        scratch_shapes=[pltpu.VMEM((dm, tf), BF16), pltpu.VMEM((dm, tf), BF16)],
    )
    return pl.pallas_call(
        _expert_up_body,
        grid_spec=grid_spec,
        out_shape=jax.ShapeDtypeStruct((r_max, f), BF16),
        compiler_params=_params("arbitrary", "arbitrary"),
        name="expert_up",
    )(tile_expert, tile_first, n_tiles, xs, w_gate, w_up)


def _expert_down_body(te_ref, tf_ref, nt_ref, h_ref, wd_ref, o_ref, wd_s):
    m = pl.program_id(1)

    @pl.when(tf_ref[m] == 1)
    def _():
        wd_s[...] = wd_ref[0].astype(BF16)

    @pl.when(m < nt_ref[0])
    def _():
        y = jnp.dot(h_ref[...], wd_s[...], preferred_element_type=F32)
        for j in range(o_ref.shape[1]):
            o_ref[:, j, :] = y[:, j * LANE:(j + 1) * LANE]


def _expert_down(hid, w_down, tile_expert, tile_first, n_tiles, tile):
    r_max, f = hid.shape
    dm = w_down.shape[2]
    tn = _pick(dm, (SUBLANE * LANE,))
    n_mt = r_max // tile
    clamp = lambda m, nt: jnp.minimum(m, nt[0] - 1)
    grid_spec = pltpu.PrefetchScalarGridSpec(
        num_scalar_prefetch=3,
        grid=(dm // tn, n_mt),
        in_specs=[pl.BlockSpec((tile, f), lambda ni, m, te, tfst, nt: (clamp(m, nt), 0)),
                  pl.BlockSpec((1, f, tn), lambda ni, m, te, tfst, nt: (te[m], 0, ni))],
        out_specs=pl.BlockSpec((tile, tn // LANE, LANE), lambda ni, m, te, tfst, nt: (clamp(m, nt), ni, 0)),
        scratch_shapes=[pltpu.VMEM((f, tn), BF16)],
    )
    return pl.pallas_call(
        _expert_down_body,
        grid_spec=grid_spec,
        out_shape=jax.ShapeDtypeStruct((r_max, dm // LANE, LANE), F32),
        compiler_params=_params("arbitrary", "arbitrary"),
        name="expert_down",
    )(tile_expert, tile_first, n_tiles, hid, w_down)


def _shared_up_body(x_ref, wg_ref, wu_ref, o_ref):
    x = x_ref[...]
    g = jnp.dot(x, wg_ref[...], preferred_element_type=F32)
    u = jnp.dot(x, wu_ref[...], preferred_element_type=F32)
    o_ref[...] = (jax.nn.silu(g) * u).astype(o_ref.dtype)


def _shared_up(x_bf, wg, wu):
    m, dm = x_bf.shape
    f = wg.shape[1]
    tm = _pick(m, (1408, 1024, 768, 640, 512, 256))
    tf = _pick(f, (256, 128))
    return pl.pallas_call(
        _shared_up_body,
        grid=(m // tm, f // tf),
        in_specs=[pl.BlockSpec((tm, dm), lambda i, j: (i, 0)),
                  pl.BlockSpec((dm, tf), lambda i, j: (0, j)),
                  pl.BlockSpec((dm, tf), lambda i, j: (0, j))],
        out_specs=pl.BlockSpec((tm, tf), lambda i, j: (i, j)),
        out_shape=jax.ShapeDtypeStruct((m, f), BF16),
        compiler_params=_params("arbitrary", "arbitrary"),
        name="shared_up",
    )(x_bf, wg, wu)


def _combine_body(pos_ref, gate_ref, ys_hbm, x_ref, sh_ref, wsd_ref, g_ref, b_ref, o_ref,
                  idx_smem, buf_ref, idx_sem, sems, *, tb, top_k, alpha, eps):
    i = pl.program_id(0)
    n = tb * top_k
    sub = ys_hbm.shape[1]

    def start(t):
        slot = t % 2
        _start_row_gather(pos_ref.at[t], idx_smem.at[pl.ds(slot, 1)], idx_sem, ys_hbm,
                          buf_ref.at[slot], sems.at[slot], n, sub)

    @pl.when(i == 0)
    def _():
        start(0)

    @pl.when(i + 1 < pl.num_programs(0))
    def _():
        start(i + 1)

    shared = jnp.dot(sh_ref[...], wsd_ref[...], preferred_element_type=F32)
    slot = i % 2
    buf = buf_ref.at[slot]
    _wait_row_gather(buf, sems.at[slot], n, sub)
    for j in range(sub):
        cols = slice(j * LANE, (j + 1) * LANE)
        routed = gate_ref[0:tb, :] * _gathered_slab(buf, 0, tb, j, sub)
        for k in range(1, top_k):
            routed = routed + gate_ref[k * tb:(k + 1) * tb, :] * _gathered_slab(buf, k * tb, tb, j, sub)
        o_ref[:, cols] = F32(alpha) * x_ref[:, cols] + (routed + shared[:, cols])
    o_ref[...] = _layer_norm(o_ref[...], g_ref[...], b_ref[...], eps)


def _combine(pos, gate_rows, ys3, x, sh_hid, ws_down, g, b, alpha, tb, cfg):
    m, dm = x.shape
    f = sh_hid.shape[1]
    sub = ys3.shape[1]
    n = tb * cfg.top_k
    nblk = m // tb
    return pl.pallas_call(
        functools.partial(_combine_body, tb=tb, top_k=cfg.top_k, alpha=alpha, eps=cfg.ln_eps),
        grid=(nblk,),
        in_specs=[pl.BlockSpec((nblk, 1, n), lambda i: (0, 0, 0)),
                  pl.BlockSpec((n, LANE), lambda i: (i, 0)),
                  pl.BlockSpec(memory_space=pl.ANY),
                  pl.BlockSpec((tb, dm), lambda i: (i, 0)),
                  pl.BlockSpec((tb, f), lambda i: (i, 0)),
                  pl.BlockSpec((f, dm), lambda i: (0, 0)),
                  pl.BlockSpec((1, dm), lambda i: (0, 0)),
                  pl.BlockSpec((1, dm), lambda i: (0, 0))],
        out_specs=pl.BlockSpec((tb, dm), lambda i: (i, 0)),
        out_shape=jax.ShapeDtypeStruct((m, dm), F32),
        scratch_shapes=[pltpu.SMEM((2, n), I32), pltpu.VMEM((2, n * (sub + ROW_PAD), LANE), F32),
                        pltpu.SemaphoreType.DMA(()), pltpu.SemaphoreType.DMA((2,))],
        compiler_params=_params("arbitrary"),
        name="moe_combine",
    )(pos, gate_rows, ys3, x, sh_hid, ws_down, g, b)


def _routing_tables(ids, gates, ranks, counts, n_tok, tile, tb, cfg):
    e, k = cfg.n_experts, cfg.top_k
    t_all = ids.shape[1]
    r_max = ((n_tok * k + e * (tile - 1)) // tile + 1) * tile
    counts = counts[:, 0].astype(I32)
    padded = ((counts + tile - 1) // tile) * tile
    ends = jnp.cumsum(padded)
    start_pad = ends - padded
    eye = ids[:, :, None] == jnp.arange(e, dtype=I32)[None, None, :]
    dest = jnp.sum(jnp.where(eye, start_pad[None, None, :], 0), axis=2) + ranks
    tok = jnp.broadcast_to(jnp.arange(t_all, dtype=I32)[None, :], (k, t_all))
    real = tok < n_tok
    src_rows = jnp.zeros((r_max,), I32).at[jnp.where(real, dest, r_max).reshape(-1)].set(
        tok.reshape(-1), mode="drop", unique_indices=True)
    blocks = lambda a: a.reshape(k, t_all // tb, tb).transpose(1, 0, 2).reshape(t_all // tb, 1, k * tb)
    pos = blocks(jnp.where(real, dest, 0))
    gate_rows = jnp.broadcast_to(blocks(jnp.where(real, gates, 0.0)).reshape(-1, 1), (t_all * k, LANE))
    n_rows = ends[-1:]
    tile_start = jnp.arange(r_max // tile, dtype=I32) * tile
    tile_expert = jnp.minimum(jnp.sum((ends[None, :] <= tile_start[:, None]).astype(I32), axis=1), e - 1)
    prev = jnp.concatenate([jnp.full((1,), -1, I32), tile_expert[:-1]])
    tile_first = (tile_expert != prev).astype(I32)
    return src_rows, pos, gate_rows, n_rows, n_rows // tile, tile_expert, tile_first


def _layer(cfg, l, x_all, n_p, seq, n_s, pos_all, cache_kv, cache_idx_k, page_table, state_conv, state_h,
           w_in, ik_g, ik_b, conv_w, conv_b, lru_w_a, lru_b_a, lru_w_x, lru_b_x, lru_lambda,
           w_o_attn, w_o_lru, w_out, ln1_g, ln1_b, w_router, b_router,
           w_gate, w_up, w_down, ws_gate, ws_up, ws_down, ln2_g, ln2_b):
    t_all, dm = x_all.shape
    t_p = n_p * seq
    d, di = cfg.head_dim, cfg.idx_dim
    qw, kvw, iw = cfg.n_heads * d, cfg.n_kv_heads * d, cfg.idx_heads * di
    lw = conv_w.shape[1]
    alpha = (2.0 * cfg.depth) ** 0.25
    c_qkv = qw + 2 * kvw
    c_i = c_qkv + iw + di
    c_w = c_i + cfg.idx_heads

    x_bf = x_all.astype(BF16)
    w_in_bf = w_in.astype(BF16)
    z_qkv = _matmul(x_bf, w_in_bf, 0, c_qkv, name="in_proj_qkv")
    z_i = _matmul(x_bf, w_in_bf, c_qkv, c_i - c_qkv, name="in_proj_idx")
    z_w = _matmul(x_bf, w_in_bf, c_i, LANE, name="in_proj_idx_w")
    z_b = _matmul(x_bf, w_in_bf[:, c_w:], name="in_proj_lru_gates")

    cosf, sinf = _rope_tables(pos_all, d, cfg.rope_theta)
    kv_all, ki_all, k_bf, ki_bf = _finalize_keys(
        z_qkv, z_i, cosf, sinf, ik_g.reshape(1, di), ik_b.reshape(1, di), cfg)
    v_t = _transpose_values(z_qkv, t_p, cfg)

    o_attn_p = _prompt_attention(z_qkv, z_i, z_w, cosf, sinf, k_bf, ki_bf, v_t, n_p, seq, cfg)
    wa_bf, wx_bf = lru_w_a.astype(BF16), lru_w_x.astype(BF16)
    row = lambda v: v.reshape(1, -1)
    h_lru_p, h_last_p = _lru_prompt(z_b, conv_w, row(conv_b), wa_bf, wx_bf, row(lru_b_a), row(lru_b_x),
                                    row(lru_lambda), n_p, seq, cfg)

    sl = slice(t_p, t_p + n_s)
    cos_s, sin_s = cosf[sl][:, None, :], sinf[sl][:, None, :]
    qi_s = z_i[sl, :iw].reshape(n_s, cfg.idx_heads, di)
    w_rep = jnp.broadcast_to(z_w[sl, :cfg.idx_heads][:, :, None], (n_s, cfg.idx_heads, LANE))
    n_phys = cache_idx_k.shape[0]
    scores = _sample_scores(page_table, qi_s, w_rep, ki_all[sl][:, None, :], cos_s, sin_s, cache_idx_k, cfg)
    q_s = z_qkv[sl, :qw].reshape(n_s, cfg.n_heads, d)
    kv_new = kv_all[sl].reshape(n_s, 2 * cfg.n_kv_heads, d)
    o_attn_s = _sample_attend(page_table, scores, q_s, kv_new, cos_s, sin_s,
                              cache_kv.reshape(n_phys, cfg.page_size * 2 * cfg.n_kv_heads, d), cfg)
    xl_s = z_b[sl, :lw]
    hist_s = jnp.moveaxis(state_conv, 1, 0)
    h_s, h_s_bf = _lru_sample(xl_s, hist_s, state_h, conv_w, row(conv_b), wa_bf, wx_bf,
                              row(lru_b_a), row(lru_b_x), row(lru_lambda), cfg)

    pad = t_all - t_p - n_s
    o_attn = jnp.concatenate([o_attn_p, o_attn_s.reshape(n_s, qw), jnp.zeros((pad, qw), BF16)], axis=0)
    h_lru = jnp.concatenate([h_lru_p, h_s_bf, jnp.zeros((pad, lw), BF16)], axis=0)

    merged = _merge(o_attn, h_lru, w_o_attn.astype(BF16), w_o_lru.astype(BF16), z_b, lw, cfg)
    x1, x1_bf = _proj_residual_ln(merged, w_out.astype(BF16), x_all, row(ln1_g), row(ln1_b), alpha, cfg)

    n_tok = t_p + n_s
    ids, gates, ranks, counts = _router(x1_bf, w_router.T.astype(BF16), b_router.reshape(-1, 1), n_tok, cfg)
    tile = cfg.moe_tile
    tb = _pick(t_all, (64, 32, 16, 8))
    src_rows, pos, gate_rows, n_rows, n_tiles, tile_expert, tile_first = _routing_tables(
        ids, gates, ranks, counts, n_tok, tile, tb, cfg)
    xs = _dispatch(x1.reshape(t_all, dm // LANE, LANE), src_rows, n_rows, tile)
    hid = _expert_up(xs, w_gate, w_up, tile_expert, tile_first, n_tiles, tile)
    ys3 = _expert_down(hid, w_down, tile_expert, tile_first, n_tiles, tile)
    sh_hid = _shared_up(x1_bf, ws_gate.astype(BF16), ws_up.astype(BF16))
    y_all = _combine(pos, gate_rows, ys3, x1, sh_hid, ws_down.astype(BF16), row(ln2_g), row(ln2_b),
                     alpha, tb, cfg)

    conv_p = z_b[:t_p, :lw].reshape(n_p, seq, lw)[:, seq - (cfg.conv_width - 1):, :]
    conv_s = jnp.concatenate([state_conv[:, 1:, :], xl_s[:, None, :]], axis=1)
    outs = dict(
        kv_p=kv_all[:t_p].reshape(n_p, seq, 2, cfg.n_kv_heads, d),
        ik_p=ki_all[:t_p].reshape(n_p, seq, di),
        cv_p=conv_p, h_p=h_last_p.reshape(n_p, lw),
        kv_s=kv_all[sl].reshape(n_s, 1, 2, cfg.n_kv_heads, d),
        ik_s=ki_all[sl].reshape(n_s, 1, di),
        cv_s=conv_s, h_s=h_s)
    return y_all, outs


def _forward(cfg, x_prompt, x_sample, cache_kv, cache_idx_k, page_table, state_conv, state_h,
             w_in, idx_k_norm_g, idx_k_norm_b, conv_w, conv_b, lru_w_a, lru_b_a, lru_w_x, lru_b_x,
             lru_lambda, w_o_attn, w_o_lru, w_out, ln1_g, ln1_b, w_router, b_router,
             w_gate, w_up, w_down, ws_gate, ws_up, ws_down, ln2_g, ln2_b):
    n_p, seq, dm = x_prompt.shape
    n_s, t_s, _ = x_sample.shape
    assert t_s == 1 and cfg.depth == 1 and w_in.shape[0] == 1
    past = page_table.shape[1] * cfg.page_size
    t_p = n_p * seq
    t_all = -(-(t_p + n_s) // cfg.row_align) * cfg.row_align
    pad = t_all - t_p - n_s
    x_all = jnp.concatenate([x_prompt.reshape(t_p, dm), x_sample.reshape(n_s, dm),
                             jnp.zeros((pad, dm), x_prompt.dtype)], axis=0)
    pos_all = jnp.concatenate([jnp.tile(jnp.arange(seq), n_p), jnp.full((n_s,), past), jnp.zeros((pad,), I32)])
    l = 0
    y_all, o = _layer(cfg, l, x_all, n_p, seq, n_s, pos_all, cache_kv[l], cache_idx_k[l], page_table,
                      state_conv[l], state_h[l], w_in[l], idx_k_norm_g[l], idx_k_norm_b[l],
                      conv_w[l], conv_b[l], lru_w_a[l], lru_b_a[l], lru_w_x[l], lru_b_x[l], lru_lambda[l],
                      w_o_attn[l], w_o_lru[l], w_out[l], ln1_g[l], ln1_b[l], w_router[l], b_router[l],
                      w_gate[l], w_up[l], w_down[l], ws_gate[l], ws_up[l], ws_down[l], ln2_g[l], ln2_b[l])
    y_p = y_all[:t_p].reshape(n_p, seq, dm)
    y_s = y_all[t_p:t_p + n_s].reshape(n_s, 1, dm)
    lead = lambda a: a[None]
    return (y_p, y_s, lead(o["kv_p"]), lead(o["ik_p"]), lead(o["cv_p"]), lead(o["h_p"]),
            lead(o["kv_s"]), lead(o["ik_s"]), lead(o["cv_s"]), lead(o["h_s"]))


def kernel(x_prompt, x_sample, cache_kv, cache_idx_k, page_table, state_conv, state_h, w_in, idx_k_norm_g, idx_k_norm_b, conv_w, conv_b, lru_w_a, lru_b_a, lru_w_x, lru_b_x, lru_lambda, w_o_attn, w_o_lru, w_out, ln1_g, ln1_b, w_router, b_router, w_gate, w_up, w_down, ws_gate, ws_up, ws_down, ln2_g, ln2_b):
    return _forward(Cfg(), x_prompt, x_sample, cache_kv, cache_idx_k, page_table, state_conv, state_h,
                    w_in, idx_k_norm_g, idx_k_norm_b, conv_w, conv_b, lru_w_a, lru_b_a, lru_w_x, lru_b_x,
                    lru_lambda, w_o_attn, w_o_lru, w_out, ln1_g, ln1_b, w_router, b_router,
                    w_gate, w_up, w_down, ws_gate, ws_up, ws_down, ln2_g, ln2_b)
```

```python
import functools
from typing import NamedTuple

import jax
import jax.numpy as jnp
import numpy as np
from jax import lax
from jax.experimental import pallas as pl
from jax.experimental.pallas import tpu as pltpu

F32 = jnp.float32
BF16 = jnp.bfloat16
I32 = jnp.int32
U32 = jnp.uint32

LANE = 128
SUBLANE = 8
VMEM_LIMIT = 56 * 1024 * 1024
MASKED = -1e30
INT_MIN = -2 ** 31


class Cfg(NamedTuple):
    n_heads: int = 16
    n_kv_heads: int = 4
    head_dim: int = 128
    idx_heads: int = 32
    idx_dim: int = 128
    topk_max: int = 256
    q_block: int = 128
    rope_theta: float = 10000.0
    lru_blocks: int = 16
    conv_width: int = 4
    lru_c: float = 8.0
    n_experts: int = 64
    top_k: int = 8
    n_groups: int = 8
    topk_groups: int = 4
    route_scale: float = 2.5
    ln_eps: float = 1e-5
    page_size: int = 128
    depth: int = 1
    key_chunk: int = 512
    moe_tile: int = 512
    row_align: int = 256


def _pick(dim, prefs):
    for p in prefs:
        if p <= dim and dim % p == 0:
            return p
    return dim


def _params(*sem):
    return pltpu.CompilerParams(dimension_semantics=sem, vmem_limit_bytes=VMEM_LIMIT)


def _mm_body(x_ref, w_ref, o_ref):
    o_ref[...] = jnp.dot(x_ref[...], w_ref[...], preferred_element_type=F32).astype(o_ref.dtype)


def _matmul(x, w, col0=0, n=None, out_dtype=F32, name="matmul"):
    m, k = x.shape
    n = w.shape[1] - col0 if n is None else n
    tm = _pick(m, (1408, 1024, 768, 640, 512, 256))
    tn = _pick(n, (512, 384, 256, 128))
    assert col0 % tn == 0
    j0 = col0 // tn
    return pl.pallas_call(
        _mm_body,
        grid=(m // tm, n // tn),
        in_specs=[pl.BlockSpec((tm, k), lambda i, j: (i, 0)),
                  pl.BlockSpec((k, tn), lambda i, j: (0, j0 + j))],
        out_specs=pl.BlockSpec((tm, tn), lambda i, j: (i, j)),
        out_shape=jax.ShapeDtypeStruct((m, n), out_dtype),
        compiler_params=_params("arbitrary", "arbitrary"),
        name=name,
    )(x, w)


def _rope(x, cosf, sinf):
    return x * cosf + pltpu.roll(x, x.shape[-1] // 2, axis=x.ndim - 1) * sinf


def _rope_tables(pos, dim, theta):
    half = dim // 2
    inv = theta ** (-jnp.arange(half, dtype=F32) / half)
    ang = pos.astype(F32)[:, None] * inv[None, :]
    cos, sin = jnp.cos(ang), jnp.sin(ang)
    return jnp.concatenate([cos, cos], -1), jnp.concatenate([-sin, sin], -1)


def _pack_halves(y):
    w = y.shape[1] // 2
    lo = pltpu.bitcast(y[:, :w].astype(BF16).astype(F32), U32)
    hi = pltpu.bitcast(y[:, w:].astype(BF16).astype(F32), U32)
    return (hi & jnp.uint32(0xFFFF0000)) | (lo >> 16)


def _unpack_halves(p):
    lo = pltpu.bitcast(p << 16, F32)
    hi = pltpu.bitcast(p & jnp.uint32(0xFFFF0000), F32)
    return lo, hi


def _layer_norm(y, g, b, eps):
    mu = jnp.mean(y, axis=-1, keepdims=True)
    yc = y - mu
    var = jnp.mean(yc * yc, axis=-1, keepdims=True)
    return yc * lax.rsqrt(var + eps) * g + b


def _kv_body(kv_ref, ki_ref, cos_ref, sin_ref, g_ref, b_ref,
             kvo_ref, kio_ref, kbf_ref, kibf_ref, *, cfg):
    c_heads, d = cfg.n_kv_heads, cfg.head_dim
    cosf, sinf = cos_ref[...], sin_ref[...]
    kv = kv_ref[...]
    for c in range(c_heads):
        kc = _rope(kv[:, c * d:(c + 1) * d], cosf, sinf)
        kvo_ref[:, c * d:(c + 1) * d] = kc
        kbf_ref[:, c * d:(c + 1) * d] = kc.astype(BF16)
    kvo_ref[:, c_heads * d:] = kv[:, c_heads * d:]
    ki = _rope(_layer_norm(ki_ref[...], g_ref[...], b_ref[...], cfg.ln_eps), cosf, sinf)
    kio_ref[...] = ki
    kibf_ref[...] = ki.astype(BF16)


def _vt_body(v_ref, vt_ref, *, cfg):
    d = cfg.head_dim
    for c in range(cfg.n_kv_heads):
        vt_ref[0, c * d:(c + 1) * d, :] = v_ref[:, c * d:(c + 1) * d].T.astype(BF16)


def _transpose_values(z_qkv, t_p, cfg):
    kc = cfg.key_chunk
    kvw = cfg.n_kv_heads * cfg.head_dim
    v_blk = (cfg.n_heads * cfg.head_dim + kvw) // kvw
    assert t_p % kc == 0
    return pl.pallas_call(
        functools.partial(_vt_body, cfg=cfg),
        grid=(t_p // kc,),
        in_specs=[pl.BlockSpec((kc, kvw), lambda i: (i, v_blk))],
        out_specs=pl.BlockSpec((1, kvw, kc), lambda i: (i, 0, 0)),
        out_shape=jax.ShapeDtypeStruct((t_p // kc, kvw, kc), BF16),
        compiler_params=_params("arbitrary"),
        name="transpose_values",
    )(z_qkv)


def _finalize_keys(z_qkv, z_i, cosf, sinf, ik_g, ik_b, cfg):
    t_all = z_qkv.shape[0]
    c_heads, d = cfg.n_kv_heads, cfg.head_dim
    kvw = c_heads * d
    qw = cfg.n_heads * d
    tr = cfg.row_align
    assert t_all % tr == 0 and qw % (2 * kvw) == 0
    return pl.pallas_call(
        functools.partial(_kv_body, cfg=cfg),
        grid=(t_all // tr,),
        in_specs=[pl.BlockSpec((tr, 2 * kvw), lambda i: (i, qw // (2 * kvw))),
                  pl.BlockSpec((tr, cfg.idx_dim), lambda i: (i, 0)),
                  pl.BlockSpec((tr, d), lambda i: (i, 0)),
                  pl.BlockSpec((tr, d), lambda i: (i, 0)),
                  pl.BlockSpec((1, cfg.idx_dim), lambda i: (0, 0)),
                  pl.BlockSpec((1, cfg.idx_dim), lambda i: (0, 0))],
        out_specs=[pl.BlockSpec((tr, 2 * kvw), lambda i: (i, 0)),
                   pl.BlockSpec((tr, cfg.idx_dim), lambda i: (i, 0)),
                   pl.BlockSpec((tr, kvw), lambda i: (i, 0)),
                   pl.BlockSpec((tr, cfg.idx_dim), lambda i: (i, 0))],
        out_shape=[jax.ShapeDtypeStruct((t_all, 2 * kvw), F32),
                   jax.ShapeDtypeStruct((t_all, cfg.idx_dim), F32),
                   jax.ShapeDtypeStruct((t_all, kvw), BF16),
                   jax.ShapeDtypeStruct((t_all, cfg.idx_dim), BF16)],
        compiler_params=_params("arbitrary"),
        name="finalize_keys",
    )(z_qkv, z_i, cosf, sinf, ik_g, ik_b)


def _sort_key(s):
    b = pltpu.bitcast(s, I32)
    return b ^ ((b >> 31) & jnp.int32(0x7FFFFFFF))


def _kth_largest_key(count_ge, k, width):
    def bit_step(i, ans):
        cand = ans | lax.shift_left(jnp.int32(1), jnp.int32(31) - i)
        cnt = count_ge(cand ^ jnp.int32(INT_MIN))
        return jnp.where(cnt >= k, cand, ans)

    ans = lax.fori_loop(0, 32, bit_step, jnp.zeros((1, width), I32))
    return ans ^ jnp.int32(INT_MIN)


def _attn_body(q_ref, qi_ref, wi_ref, cos_ref, sin_ref, kbf_ref, kibf_ref, vt_ref,
               o_ref, qs_ref, qis_ref, key_ref, m_ref, l_ref, acc_ref, *, cfg, topk):
    qb_idx = pl.program_id(1)
    qb, kc = cfg.q_block, cfg.key_chunk
    d, di = cfg.head_dim, cfg.idx_dim
    c_heads = cfg.n_kv_heads
    g_heads = cfg.n_heads // c_heads
    cosf, sinf = cos_ref[...], sin_ref[...]

    for h in range(cfg.n_heads):
        qs_ref[h * qb:(h + 1) * qb, :] = _rope(q_ref[:, h * d:(h + 1) * d], cosf, sinf).astype(BF16)
    for h in range(cfg.idx_heads):
        qis_ref[h * qb:(h + 1) * qb, :] = _rope(qi_ref[:, h * di:(h + 1) * di], cosf, sinf).astype(BF16)
    w_t = wi_ref[...].T * F32((di * cfg.idx_heads) ** -0.5)

    q0 = qb_idx * qb
    n_chunks = (q0 + qb + kc - 1) // kc
    tpos = q0 + lax.broadcasted_iota(I32, (kc, qb), 1)
    krow = lax.broadcasted_iota(I32, (kc, qb), 0)
    nt = (((1,), (1,)), ((), ()))

    def score_chunk(ci, carry):
        k0 = pl.multiple_of(ci * kc, kc)
        ki_c = kibf_ref[pl.ds(k0, kc), :]
        acc = jnp.zeros((kc, qb), F32)
        for hp in range(cfg.idx_heads // 2):
            z = lax.dot_general(ki_c, qis_ref[hp * 2 * qb:(hp + 1) * 2 * qb, :], nt,
                                preferred_element_type=F32)
            acc = acc + jnp.maximum(z[:, :qb], 0.0) * w_t[2 * hp:2 * hp + 1, :]
            acc = acc + jnp.maximum(z[:, qb:], 0.0) * w_t[2 * hp + 1:2 * hp + 2, :]
        causal = (k0 + krow) <= tpos
        key_ref[pl.ds(k0, kc), :] = _sort_key(jnp.where(causal, acc, -jnp.inf))
        return carry

    lax.fori_loop(0, n_chunks, score_chunk, 0)

    def count_ge(cand):
        def cnt_chunk(ci, acc8):
            k0 = pl.multiple_of(ci * kc, kc)
            hit = jnp.where(key_ref[pl.ds(k0, kc), :] >= cand, 1, 0).astype(I32)
            return acc8 + jnp.sum(hit.reshape(kc // SUBLANE, SUBLANE, qb), axis=0)
        acc8 = lax.fori_loop(0, n_chunks, cnt_chunk, jnp.zeros((SUBLANE, qb), I32))
        return jnp.sum(acc8, axis=0, keepdims=True)

    thr = _kth_largest_key(count_ge, topk, qb)

    m_ref[...] = jnp.full(m_ref.shape, MASKED, F32)
    l_ref[...] = jnp.zeros(l_ref.shape, F32)
    acc_ref[...] = jnp.zeros(acc_ref.shape, F32)
    scale = F32(d ** -0.5)

    def attend_chunk(ci, carry):
        k0 = pl.multiple_of(ci * kc, kc)
        keep = (key_ref[pl.ds(k0, kc), :] >= thr) & ((k0 + krow) <= tpos)
        bias = jnp.where(keep, 0.0, MASKED).astype(F32)
        bias = jnp.concatenate([bias] * g_heads, axis=1)
        for c in range(c_heads):
            k_c = kbf_ref[pl.ds(k0, kc), c * d:(c + 1) * d]
            s = lax.dot_general(k_c, qs_ref[c * g_heads * qb:(c + 1) * g_heads * qb, :], nt,
                                preferred_element_type=F32) * scale + bias
            m_old = m_ref[c:c + 1, :]
            m_new = jnp.maximum(m_old, jnp.max(s, axis=0, keepdims=True))
            alpha = jnp.exp(m_old - m_new)
            p = jnp.exp(s - m_new)
            l_ref[c:c + 1, :] = alpha * l_ref[c:c + 1, :] + jnp.sum(p, axis=0, keepdims=True)
            pv = jnp.dot(vt_ref[ci, c * d:(c + 1) * d, :], p.astype(BF16), preferred_element_type=F32)
            acc_ref[c * d:(c + 1) * d, :] = alpha * acc_ref[c * d:(c + 1) * d, :] + pv
            m_ref[c:c + 1, :] = m_new
        return carry

    lax.fori_loop(0, n_chunks, attend_chunk, 0)

    for c in range(c_heads):
        o_t = acc_ref[c * d:(c + 1) * d, :] / l_ref[c:c + 1, :]
        for g in range(g_heads):
            h = c * g_heads + g
            o_ref[:, h * d:(h + 1) * d] = o_t[:, g * qb:(g + 1) * qb].T.astype(o_ref.dtype)


def _prompt_attention(z_qkv, z_i, z_w, cosf, sinf, k_bf, ki_bf, v_t, n_seq, seq, cfg):
    qb, kc = cfg.q_block, cfg.key_chunk
    d = cfg.head_dim
    qw = cfg.n_heads * d
    kvw = cfg.n_kv_heads * d
    iw = cfg.idx_heads * cfg.idx_dim
    assert seq % kc == 0 and seq % qb == 0 and kc % qb == 0
    assert cfg.head_dim == LANE and cfg.idx_dim == LANE
    nqb = seq // qb
    topk = min(cfg.topk_max, seq // 4)
    row = lambda n, j: (n * nqb + j, 0)
    return pl.pallas_call(
        functools.partial(_attn_body, cfg=cfg, topk=topk),
        grid=(n_seq, nqb),
        in_specs=[pl.BlockSpec((qb, qw), row),
                  pl.BlockSpec((qb, iw), row),
                  pl.BlockSpec((qb, LANE), lambda n, j: (n * nqb + j, 1)),
                  pl.BlockSpec((qb, d), row),
                  pl.BlockSpec((qb, d), row),
                  pl.BlockSpec((seq, kvw), lambda n, j: (n, 0)),
                  pl.BlockSpec((seq, cfg.idx_dim), lambda n, j: (n, 0)),
                  pl.BlockSpec((seq // kc, kvw, kc), lambda n, j: (n, 0, 0))],
        out_specs=pl.BlockSpec((qb, qw), row),
        out_shape=jax.ShapeDtypeStruct((n_seq * seq, qw), BF16),
        scratch_shapes=[pltpu.VMEM((cfg.n_heads * qb, d), BF16),
                        pltpu.VMEM((cfg.idx_heads * qb, cfg.idx_dim), BF16),
                        pltpu.VMEM((seq, qb), I32),
                        pltpu.VMEM((SUBLANE, (cfg.n_heads // cfg.n_kv_heads) * qb), F32),
                        pltpu.VMEM((SUBLANE, (cfg.n_heads // cfg.n_kv_heads) * qb), F32),
                        pltpu.VMEM((kvw, (cfg.n_heads // cfg.n_kv_heads) * qb), F32)],
        compiler_params=_params("arbitrary", "arbitrary"),
        name="prompt_attention",
    )(z_qkv, z_i, z_w, cosf, sinf, k_bf, ki_bf, v_t)


def _softplus(x):
    return jnp.maximum(x, 0.0) + jnp.log1p(jnp.exp(-jnp.abs(x)))


def _lru_gates(xc, wa_ref, wx_ref, ba, bx, lam, cfg):
    w = xc.shape[1]
    bd = w // cfg.lru_blocks
    xb = xc.astype(BF16)
    r_parts, i_parts = [], []
    for k in range(cfg.lru_blocks):
        xk = xb[:, k * bd:(k + 1) * bd]
        r_parts.append(jnp.dot(xk, wa_ref[k], preferred_element_type=F32))
        i_parts.append(jnp.dot(xk, wx_ref[k], preferred_element_type=F32))
    r = jax.nn.sigmoid(jnp.concatenate(r_parts, axis=1) + ba)
    gate_i = jax.nn.sigmoid(jnp.concatenate(i_parts, axis=1) + bx)
    log_a = (-cfg.lru_c * r) * _softplus(-lam)
    a = jnp.exp(log_a)
    u = jnp.sqrt(-jnp.tanh(log_a) * (a * a + 1.0)) * (gate_i * xc)
    return a, u


def _lru_prompt_body(xl_ref, cw_ref, cb_ref, wa_ref, wx_ref, ba_ref, bx_ref, lam_ref,
                     h_ref, hlast_ref, prev_ref, carry_ref, *, cfg):
    j = pl.program_id(1)
    tb = xl_ref.shape[0]
    cw = cfg.conv_width

    @pl.when(j == 0)
    def _():
        prev_ref[...] = jnp.zeros(prev_ref.shape, F32)
        carry_ref[...] = jnp.zeros(carry_ref.shape, F32)

    xl = xl_ref[...]
    ext = jnp.concatenate([prev_ref[...], xl], axis=0)
    off = SUBLANE - (cw - 1)
    xc = cb_ref[...] + ext[off:off + tb] * cw_ref[0:1, :]
    for t in range(1, cw):
        xc = xc + ext[off + t:off + t + tb] * cw_ref[t:t + 1, :]
    prev_ref[...] = xl[tb - SUBLANE:, :]

    a, u = _lru_gates(xc, wa_ref, wx_ref, ba_ref[...], bx_ref[...], lam_ref[...], cfg)

    row = lax.broadcasted_iota(I32, a.shape, 0) & (SUBLANE - 1)
    s = 1
    while s < SUBLANE:
        ok = row >= s
        a_sh = jnp.where(ok, pltpu.roll(a, s, axis=0), 1.0)
        u_sh = jnp.where(ok, pltpu.roll(u, s, axis=0), 0.0)
        u = u + a * u_sh
        a = a * a_sh
        s *= 2
    h_prev = carry_ref[...]
    for gi in range(tb // SUBLANE):
        sl = slice(gi * SUBLANE, (gi + 1) * SUBLANE)
        h_rows = u[sl] + a[sl] * h_prev
        h_ref[sl, :] = h_rows.astype(h_ref.dtype)
        h_prev = h_rows[SUBLANE - 1:SUBLANE, :]
    carry_ref[...] = h_prev
    hlast_ref[0] = h_prev


def _lru_prompt(z_b, conv_w, conv_b, wa, wx, ba, bx, lam, n_seq, seq, cfg):
    w = conv_w.shape[1]
    tb = _pick(seq, (256, 128))
    nb = seq // tb
    bd = w // cfg.lru_blocks
    vec = pl.BlockSpec((1, w), lambda n, j: (0, 0))
    return pl.pallas_call(
        functools.partial(_lru_prompt_body, cfg=cfg),
        grid=(n_seq, nb),
        in_specs=[pl.BlockSpec((tb, w), lambda n, j: (n * nb + j, 0)),
                  pl.BlockSpec((cfg.conv_width, w), lambda n, j: (0, 0)),
                  vec,
                  pl.BlockSpec((cfg.lru_blocks, bd, bd), lambda n, j: (0, 0, 0)),
                  pl.BlockSpec((cfg.lru_blocks, bd, bd), lambda n, j: (0, 0, 0)),
                  vec, vec, vec],
        out_specs=[pl.BlockSpec((tb, w), lambda n, j: (n * nb + j, 0)),
                   pl.BlockSpec((1, 1, w), lambda n, j: (n, 0, 0))],
        out_shape=[jax.ShapeDtypeStruct((n_seq * seq, w), BF16),
                   jax.ShapeDtypeStruct((n_seq, 1, w), F32)],
        scratch_shapes=[pltpu.VMEM((SUBLANE, w), F32), pltpu.VMEM((1, w), F32)],
        compiler_params=_params("arbitrary", "arbitrary"),
        name="lru_prompt",
    )(z_b, conv_w, conv_b, wa, wx, ba, bx, lam)


def _lru_sample_body(xl_ref, hist_ref, h0_ref, cw_ref, cb_ref, wa_ref, wx_ref, ba_ref, bx_ref, lam_ref,
                     h_ref, hbf_ref, *, cfg):
    cw = cfg.conv_width
    xc = cb_ref[...] + hist_ref[0] * cw_ref[0:1, :]
    for t in range(1, cw - 1):
        xc = xc + hist_ref[t] * cw_ref[t:t + 1, :]
    xc = xc + xl_ref[...] * cw_ref[cw - 1:cw, :]
    a, u = _lru_gates(xc, wa_ref, wx_ref, ba_ref[...], bx_ref[...], lam_ref[...], cfg)
    h = a * h0_ref[...] + u
    h_ref[...] = h
    hbf_ref[...] = h.astype(BF16)


def _lru_sample(xl, hist, h0, conv_w, conv_b, wa, wx, ba, bx, lam, cfg):
    n, w = xl.shape
    return pl.pallas_call(
        functools.partial(_lru_sample_body, cfg=cfg),
        out_shape=[jax.ShapeDtypeStruct((n, w), F32), jax.ShapeDtypeStruct((n, w), BF16)],
        compiler_params=pltpu.CompilerParams(vmem_limit_bytes=VMEM_LIMIT),
        name="lru_sample",
    )(xl, hist, h0, conv_w, conv_b, wa, wx, ba, bx, lam)


def _sample_scores_body(pt_ref, qi_ref, w_ref, kin_ref, cos_ref, sin_ref, *rest, cfg, pages_per_step):
    page_refs, o_ref = rest[:pages_per_step], rest[pages_per_step]
    j = pl.program_id(1)
    nj = pl.num_programs(1)
    cosf, sinf = cos_ref[0], sin_ref[0]
    qi = _rope(qi_ref[0], cosf, sinf)
    qi_b = qi.astype(BF16)
    w = w_ref[0] * F32((cfg.idx_dim * cfg.idx_heads) ** -0.5)
    nt = (((1,), (1,)), ((), ()))
    rows = []
    for p in range(pages_per_step):
        z = lax.dot_general(qi_b, page_refs[p][0].astype(BF16), nt, preferred_element_type=F32)
        rows.append(jnp.sum(jnp.maximum(z, 0.0) * w, axis=0, keepdims=True))
    r0 = pl.multiple_of(j * pages_per_step, pages_per_step)
    o_ref[0, pl.ds(r0, pages_per_step), :] = jnp.concatenate(rows, axis=0)

    @pl.when(j == nj - 1)
    def _():
        z_new = jnp.sum(qi * kin_ref[0], axis=1, keepdims=True)
        s_new = jnp.sum(jnp.maximum(z_new, 0.0) * w, axis=0, keepdims=True)
        lane = lax.broadcasted_iota(I32, (SUBLANE, LANE), 1)
        sub = lax.broadcasted_iota(I32, (SUBLANE, LANE), 0)
        tail = jnp.where((lane == 0) & (sub == 0), jnp.broadcast_to(s_new, (SUBLANE, LANE)), -jnp.inf)
        o_ref[0, pl.ds(nj * pages_per_step, SUBLANE), :] = tail


def _sample_scores(page_table, qi, w_rep, ki_new, cos_s, sin_s, cache_idx_k, cfg):
    n, n_pages = page_table.shape
    pps = _pick(n_pages, (SUBLANE,))
    assert cfg.page_size == LANE
    hi, di = cfg.idx_heads, cfg.idx_dim
    per_seq = lambda b, j, pt: (b, 0, 0)
    page_specs = [pl.BlockSpec((1, cfg.page_size, di),
                               functools.partial(lambda b, j, pt, p: (pt[b, j * pps + p], 0, 0), p=p))
                  for p in range(pps)]
    grid_spec = pltpu.PrefetchScalarGridSpec(
        num_scalar_prefetch=1,
        grid=(n, n_pages // pps),
        in_specs=[pl.BlockSpec((1, hi, di), per_seq),
                  pl.BlockSpec((1, hi, LANE), per_seq),
                  pl.BlockSpec((1, 1, di), per_seq),
                  pl.BlockSpec((1, 1, di), per_seq),
                  pl.BlockSpec((1, 1, di), per_seq)] + page_specs,
        out_specs=pl.BlockSpec((1, n_pages + SUBLANE, LANE), per_seq),
    )
    return pl.pallas_call(
        functools.partial(_sample_scores_body, cfg=cfg, pages_per_step=pps),
        grid_spec=grid_spec,
        out_shape=jax.ShapeDtypeStruct((n, n_pages + SUBLANE, LANE), F32),
        compiler_params=_params("arbitrary", "arbitrary"),
        name="sample_scores",
    )(page_table, qi, w_rep, ki_new, cos_s, sin_s, *([cache_idx_k] * pps))


def _sample_attend_body(pt_ref, s_ref, q_ref, kvn_ref, cos_ref, sin_ref, *rest,
                        cfg, pages_per_step, topk):
    page_refs = rest[:pages_per_step]
    o_ref, qs_ref, keep_ref, m_ref, l_ref, acc_ref = rest[pages_per_step:]
    j = pl.program_id(1)
    nj = pl.num_programs(1)
    nh, d = cfg.n_heads, cfg.head_dim
    c_heads = cfg.n_kv_heads
    g_heads = nh // c_heads
    ps = cfg.page_size
    n_rows = s_ref.shape[1]
    scale = F32(d ** -0.5)
    head_c = lax.broadcasted_iota(I32, (nh, 1), 0) // g_heads

    @pl.when(j == 0)
    def _():
        keys = _sort_key(s_ref[0])

        def count_ge(cand):
            hit = jnp.where(keys >= cand, 1, 0).astype(I32)
            return jnp.sum(jnp.sum(hit, axis=0, keepdims=True), axis=1, keepdims=True)

        thr = _kth_largest_key(count_ge, topk, 1)
        finite = s_ref[0] > -jnp.inf
        keep_ref[...] = jnp.where((keys >= thr) & finite, 0.0, MASKED).astype(F32)
        qs_ref[...] = _rope(q_ref[0], cos_ref[0], sin_ref[0]).astype(BF16)
        m_ref[...] = jnp.full(m_ref.shape, MASKED, F32)
        l_ref[...] = jnp.zeros(l_ref.shape, F32)
        acc_ref[...] = jnp.zeros(acc_ref.shape, F32)

    nt = (((1,), (1,)), ((), ()))
    qs = qs_ref[...]
    per = 2 * c_heads
    width = ps * per
    r0 = pl.multiple_of(j * pages_per_step, pages_per_step)
    keep_rows = keep_ref[pl.ds(r0, pages_per_step), :]
    sel_rows = jnp.where(keep_rows == 0.0, 1.0, 0.0).astype(BF16)
    spread = (lax.broadcasted_iota(I32, (ps, width), 1) // per
              == lax.broadcasted_iota(I32, (ps, width), 0))
    sel_cols = jnp.dot(sel_rows, jnp.where(spread, 1.0, 0.0).astype(BF16),
                       preferred_element_type=F32)
    own_key = (lax.broadcasted_iota(I32, (nh, width), 1) % per) == head_c
    m_run, l_run, acc = m_ref[...], l_ref[...], acc_ref[...]
    for p in range(pages_per_step):
        rows = page_refs[p][0].astype(BF16)
        s = lax.dot_general(qs, rows, nt, preferred_element_type=F32) * scale
        s = jnp.where(own_key & (sel_cols[p:p + 1, :] > 0.5), s, MASKED)
        m_new = jnp.maximum(m_run, jnp.max(s, axis=1, keepdims=True))
        alpha = jnp.exp(m_run - m_new)
        p_un = jnp.exp(s - m_new)
        l_run = alpha * l_run + jnp.sum(p_un, axis=1, keepdims=True)
        p_val = pltpu.roll(p_un, c_heads, axis=1).astype(BF16)
        acc = alpha * acc + jnp.dot(p_val, rows, preferred_element_type=F32)
        m_run = m_new
    m_ref[...], l_ref[...], acc_ref[...] = m_run, l_run, acc

    @pl.when(j == nj - 1)
    def _():
        kvn = kvn_ref[0]
        k_new = jnp.zeros((nh, d), F32)
        v_new = jnp.zeros((nh, d), F32)
        for c in range(c_heads):
            k_new = jnp.where(head_c == c, kvn[c:c + 1, :], k_new)
            v_new = jnp.where(head_c == c, kvn[c_heads + c:c_heads + c + 1, :], v_new)
        q_f = _rope(q_ref[0], cos_ref[0], sin_ref[0])
        bias_new = keep_ref[n_rows - SUBLANE:n_rows - SUBLANE + 1, 0:1]
        s_new = jnp.sum(q_f * k_new, axis=1, keepdims=True) * scale + bias_new
        m_o = m_ref[...]
        m_n = jnp.maximum(m_o, s_new)
        al = jnp.exp(m_o - m_n)
        p_new = jnp.exp(s_new - m_n)
        l_fin = al * l_ref[...] + p_new
        acc_fin = al * acc_ref[...] + p_new * v_new
        o_ref[0] = (acc_fin / l_fin).astype(o_ref.dtype)


def _sample_attend(page_table, scores, q, kv_new, cos_s, sin_s, cache_kv, cfg):
    n, n_pages = page_table.shape
    pps = _pick(n_pages, (SUBLANE,))
    nh, d = cfg.n_heads, cfg.head_dim
    kvw2 = 2 * cfg.n_kv_heads * d
    topk = min(cfg.topk_max, (n_pages * cfg.page_size + 1) // 4)
    per_seq = lambda b, j, pt: (b, 0, 0)
    page_specs = [pl.BlockSpec((1, cfg.page_size * 2 * cfg.n_kv_heads, d),
                               functools.partial(lambda b, j, pt, p: (pt[b, j * pps + p], 0, 0), p=p))
                  for p in range(pps)]
    n_rows = scores.shape[1]
    grid_spec = pltpu.PrefetchScalarGridSpec(
        num_scalar_prefetch=1,
        grid=(n, n_pages // pps),
        in_specs=[pl.BlockSpec((1, n_rows, LANE), per_seq),
                  pl.BlockSpec((1, nh, d), per_seq),
                  pl.BlockSpec((1, 2 * cfg.n_kv_heads, d), per_seq),
                  pl.BlockSpec((1, 1, d), per_seq),
                  pl.BlockSpec((1, 1, d), per_seq)] + page_specs,
        out_specs=pl.BlockSpec((1, nh, d), per_seq),
        scratch_shapes=[pltpu.VMEM((nh, d), BF16),
                        pltpu.VMEM((n_rows, LANE), F32),
                        pltpu.VMEM((nh, 1), F32),
                        pltpu.VMEM((nh, 1), F32),
                        pltpu.VMEM((nh, d), F32)],
    )
    return pl.pallas_call(
        functools.partial(_sample_attend_body, cfg=cfg, pages_per_step=pps, topk=topk),
        grid_spec=grid_spec,
        out_shape=jax.ShapeDtypeStruct((n, nh, d), BF16),
        compiler_params=_params("arbitrary", "arbitrary"),
        name="sample_attend",
    )(page_table, scores, q, kv_new, cos_s, sin_s, *([cache_kv] * pps))


def _merge_body(oa_ref, hl_ref, woa_ref, wol_ref, ga_ref, gb_ref, o_ref):
    ya = jnp.dot(oa_ref[...], woa_ref[...], preferred_element_type=F32)
    yl = jnp.dot(hl_ref[...], wol_ref[...], preferred_element_type=F32)
    o_ref[...] = (jax.nn.sigmoid(ga_ref[...]) * ya + jax.nn.sigmoid(gb_ref[...]) * yl).astype(o_ref.dtype)


def _merge(o_attn, h_lru, w_oa, w_ol, z_b, lru_w, cfg):
    m, aw = o_attn.shape
    dm = w_oa.shape[1]
    tm = _pick(m, (1408, 1024, 768, 640, 512, 256))
    tn = _pick(dm, (512, 256, 128))
    assert lru_w % tn == 0
    ga_off = lru_w // tn
    gb_off = (lru_w + dm) // tn
    return pl.pallas_call(
        _merge_body,
        grid=(m // tm, dm // tn),
        in_specs=[pl.BlockSpec((tm, aw), lambda i, j: (i, 0)),
                  pl.BlockSpec((tm, lru_w), lambda i, j: (i, 0)),
                  pl.BlockSpec((aw, tn), lambda i, j: (0, j)),
                  pl.BlockSpec((lru_w, tn), lambda i, j: (0, j)),
                  pl.BlockSpec((tm, tn), lambda i, j: (i, ga_off + j)),
                  pl.BlockSpec((tm, tn), lambda i, j: (i, gb_off + j))],
        out_specs=pl.BlockSpec((tm, tn), lambda i, j: (i, j)),
        out_shape=jax.ShapeDtypeStruct((m, dm), BF16),
        compiler_params=_params("arbitrary", "arbitrary"),
        name="merge_mixers",
    )(o_attn, h_lru, w_oa, w_ol, z_b, z_b)


def _proj_ln_body(a_ref, w_ref, x_ref, g_ref, b_ref, o_ref, obf_ref, opk_ref, *, alpha, eps, tn):
    j = pl.program_id(1)
    c0 = pl.multiple_of(j * tn, tn)
    y = jnp.dot(a_ref[...], w_ref[...], preferred_element_type=F32)
    o_ref[:, pl.ds(c0, tn)] = F32(alpha) * x_ref[...] + y

    @pl.when(j == pl.num_programs(1) - 1)
    def _():
        out = _layer_norm(o_ref[...], g_ref[...], b_ref[...], eps)
        o_ref[...] = out
        obf_ref[...] = out.astype(BF16)
        opk_ref[...] = _pack_halves(out)


def _proj_residual_ln(a, w, x, g, b, alpha, cfg):
    m, k = a.shape
    dm = w.shape[1]
    tm = _pick(m, (384, 256))
    tn = _pick(dm, (512, 256, 128))
    return pl.pallas_call(
        functools.partial(_proj_ln_body, alpha=alpha, eps=cfg.ln_eps, tn=tn),
        grid=(m // tm, dm // tn),
        in_specs=[pl.BlockSpec((tm, k), lambda i, j: (i, 0)),
                  pl.BlockSpec((k, tn), lambda i, j: (0, j)),
                  pl.BlockSpec((tm, tn), lambda i, j: (i, j)),
                  pl.BlockSpec((1, dm), lambda i, j: (0, 0)),
                  pl.BlockSpec((1, dm), lambda i, j: (0, 0))],
        out_specs=[pl.BlockSpec((tm, dm), lambda i, j: (i, 0)),
                   pl.BlockSpec((tm, dm), lambda i, j: (i, 0)),
                   pl.BlockSpec((tm, dm // 2), lambda i, j: (i, 0))],
        out_shape=[jax.ShapeDtypeStruct((m, dm), F32), jax.ShapeDtypeStruct((m, dm), BF16),
                   jax.ShapeDtypeStruct((m, dm // 2), U32)],
        compiler_params=_params("arbitrary", "arbitrary"),
        name="proj_residual_ln",
    )(a, w, x, g, b)


def _first_index_of_max(v, idx, big):
    m = jnp.max(v, axis=0, keepdims=True)
    first = jnp.min(jnp.where(v == m, idx, big), axis=0, keepdims=True)
    return m, first


def _router_body(x_ref, wr_ref, br_ref, ids_ref, gate_ref, rank_ref, cnt_ref, *, cfg, n_tok):
    i = pl.program_id(0)
    e, ng = cfg.n_experts, cfg.n_groups
    per = e // ng
    tm = x_ref.shape[0]
    nt = (((1,), (1,)), ((), ()))
    logits = lax.dot_general(wr_ref[...], x_ref[...], nt, preferred_element_type=F32)
    s = jax.nn.sigmoid(logits)
    choice = s + br_ref[...]
    eidx = lax.broadcasted_iota(I32, (e, tm), 0)

    grp_rows = []
    jidx = lax.broadcasted_iota(I32, (per, tm), 0)
    for g in range(ng):
        cg = choice[g * per:(g + 1) * per, :]
        m1, j1 = _first_index_of_max(cg, jidx, per)
        m2 = jnp.max(jnp.where(jidx == j1, -jnp.inf, cg), axis=0, keepdims=True)
        grp_rows.append(m1 + m2)
    grp = jnp.concatenate(grp_rows, axis=0)

    gidx = lax.broadcasted_iota(I32, (ng, tm), 0)
    grp_keep = jnp.zeros((ng, tm), F32)
    work = grp
    for _ in range(cfg.topk_groups):
        _, gsel = _first_index_of_max(work, gidx, ng)
        hit = gidx == gsel
        grp_keep = jnp.where(hit, 1.0, grp_keep)
        work = jnp.where(hit, -jnp.inf, work)

    keep_rows = [jnp.broadcast_to(grp_keep[g:g + 1, :], (per, tm)) for g in range(ng)]
    masked = jnp.where(jnp.concatenate(keep_rows, axis=0) > 0.5, choice, -jnp.inf)

    ids, wts, hits = [], [], []
    for _ in range(cfg.top_k):
        _, esel = _first_index_of_max(masked, eidx, e)
        hit = eidx == esel
        ids.append(esel)
        hits.append(hit)
        wts.append(jnp.sum(jnp.where(hit, s, 0.0), axis=0, keepdims=True))
        masked = jnp.where(hit, -jnp.inf, masked)
    wk = jnp.concatenate(wts, axis=0)
    ids_ref[...] = jnp.concatenate(ids, axis=0)
    gate_ref[...] = F32(cfg.route_scale) * wk / jnp.sum(wk, axis=0, keepdims=True)

    @pl.when(i == 0)
    def _():
        cnt_ref[...] = jnp.zeros(cnt_ref.shape, F32)

    tok = i * tm + lax.broadcasted_iota(I32, (e, tm), 1)
    picked = jnp.zeros((e, tm), F32)
    for h in hits:
        picked = jnp.where(h, 1.0, picked)
    picked = jnp.where(tok < n_tok, picked, 0.0)
    before = (lax.broadcasted_iota(I32, (tm, tm), 0) < lax.broadcasted_iota(I32, (tm, tm), 1))
    earlier = jnp.dot(picked.astype(BF16), jnp.where(before, 1.0, 0.0).astype(BF16),
                      preferred_element_type=F32)
    rank_all = cnt_ref[:, 0:1] + earlier
    rank_ref[...] = jnp.concatenate(
        [jnp.sum(jnp.where(h, rank_all, 0.0), axis=0, keepdims=True) for h in hits], axis=0).astype(I32)
    cnt_ref[...] = cnt_ref[...] + jnp.sum(picked, axis=1, keepdims=True)


def _router(x_bf, w_r_t, b_r, n_tok, cfg):
    m, dm = x_bf.shape
    tm = _pick(m, (256, 128))
    e = cfg.n_experts
    pick = pl.BlockSpec((cfg.top_k, tm), lambda i: (0, i))
    return pl.pallas_call(
        functools.partial(_router_body, cfg=cfg, n_tok=n_tok),
        grid=(m // tm,),
        in_specs=[pl.BlockSpec((tm, dm), lambda i: (i, 0)),
                  pl.BlockSpec((e, dm), lambda i: (0, 0)),
                  pl.BlockSpec((e, 1), lambda i: (0, 0))],
        out_specs=[pick, pick, pick, pl.BlockSpec((e, LANE), lambda i: (0, 0))],
        out_shape=[jax.ShapeDtypeStruct((cfg.top_k, m), I32),
                   jax.ShapeDtypeStruct((cfg.top_k, m), F32),
                   jax.ShapeDtypeStruct((cfg.top_k, m), I32),
                   jax.ShapeDtypeStruct((e, LANE), F32)],
        compiler_params=_params("arbitrary"),
        name="router",
    )(x_bf, w_r_t, b_r)


DMA_THREADS = 2


def _row_pitch(sub):
    return -(-sub // SUBLANE) * SUBLANE + SUBLANE


def _start_row_gather(idx_row, idx_smem, idx_sem, src_hbm, buf, sem, n, sub):
    cp = pltpu.make_async_copy(idx_row, idx_smem, idx_sem)
    cp.start()
    cp.wait()

    def start(g, carry):
        for u in range(DMA_THREADS):
            r = g * DMA_THREADS + u
            r0 = pl.multiple_of(r * _row_pitch(sub), SUBLANE)
            pltpu.make_async_copy(src_hbm.at[idx_smem[0, r]], buf.at[pl.ds(r0, sub)], sem).start(priority=u)
        return carry
    lax.fori_loop(0, n // DMA_THREADS, start, 0)


def _wait_row_gather(buf, sem, n, sub):
    pltpu.make_async_copy(buf.at[pl.ds(0, n * sub)], buf.at[pl.ds(0, n * sub)], sem).wait()


def _gathered_slab(buf, first, count, j, sub):
    return buf[pl.ds(first * _row_pitch(sub) + j, count, stride=_row_pitch(sub)), :]


def _dispatch_body(nrows_ref, src_ref, x_hbm, o_ref, idx_smem, buf_ref, idx_sem, sems, *, tile):
    i = pl.program_id(0)
    sub = x_hbm.shape[1]
    n_valid = (nrows_ref[0] + tile - 1) // tile

    def start(t):
        slot = t % 2
        _start_row_gather(src_ref.at[t], idx_smem.at[pl.ds(slot, 1)], idx_sem, x_hbm,
                          buf_ref.at[slot], sems.at[slot], tile, sub)

    @pl.when(i == 0)
    def _():
        start(0)

    @pl.when(i + 1 < n_valid)
    def _():
        start(i + 1)

    @pl.when(i < n_valid)
    def _():
        slot = i % 2
        _wait_row_gather(buf_ref.at[slot], sems.at[slot], tile, sub)
        half = sub * LANE
        for j in range(sub):
            lo, hi = _unpack_halves(_gathered_slab(buf_ref.at[slot], 0, tile, j, sub))
            o_ref[:, j * LANE:(j + 1) * LANE] = lo.astype(o_ref.dtype)
            o_ref[:, half + j * LANE:half + (j + 1) * LANE] = hi.astype(o_ref.dtype)


def _dispatch(x3, src_rows, n_rows, tile):
    r_max = src_rows.shape[0]
    sub = x3.shape[1]
    n_tiles = r_max // tile
    last = lambda i, nr: jnp.minimum(i, (nr[0] - 1) // tile)
    grid_spec = pltpu.PrefetchScalarGridSpec(
        num_scalar_prefetch=1,
        grid=(n_tiles,),
        in_specs=[pl.BlockSpec((n_tiles, 1, tile), lambda i, nr: (0, 0, 0)),
                  pl.BlockSpec(memory_space=pl.ANY)],
        out_specs=pl.BlockSpec((tile, 2 * sub * LANE), lambda i, nr: (last(i, nr), 0)),
        scratch_shapes=[pltpu.SMEM((2, tile), I32), pltpu.VMEM((2, tile * _row_pitch(sub), LANE), x3.dtype),
                        pltpu.SemaphoreType.DMA(()), pltpu.SemaphoreType.DMA((2,))],
    )
    return pl.pallas_call(
        functools.partial(_dispatch_body, tile=tile),
        grid_spec=grid_spec,
        out_shape=jax.ShapeDtypeStruct((r_max, 2 * sub * LANE), BF16),
        compiler_params=_params("arbitrary"),
        name="moe_dispatch",
    )(n_rows, src_rows.reshape(n_tiles, 1, tile), x3)


def _expert_up_body(te_ref, tf_ref, nt_ref, x_ref, wg_ref, wu_ref, o_ref, wg_s, wu_s):
    m = pl.program_id(1)

    @pl.when(tf_ref[m] == 1)
    def _():
        wg_s[...] = wg_ref[0].astype(BF16)
        wu_s[...] = wu_ref[0].astype(BF16)

    @pl.when(m < nt_ref[0])
    def _():
        x = x_ref[...]
        g = jnp.dot(x, wg_s[...], preferred_element_type=F32)
        u = jnp.dot(x, wu_s[...], preferred_element_type=F32)
        o_ref[...] = (jax.nn.silu(g) * u).astype(o_ref.dtype)


def _expert_up(xs, w_gate, w_up, tile_expert, tile_first, n_tiles, tile):
    r_max, dm = xs.shape
    e, _, f = w_gate.shape
    tf = _pick(f, (512, 256, 128))
    n_mt = r_max // tile
    clamp = lambda m, nt: jnp.minimum(m, nt[0] - 1)
    grid_spec = pltpu.PrefetchScalarGridSpec(
        num_scalar_prefetch=3,
        grid=(f // tf, n_mt),
        in_specs=[pl.BlockSpec((tile, dm), lambda fi, m, te, tfst, nt: (clamp(m, nt), 0)),
                  pl.BlockSpec((1, dm, tf), lambda fi, m, te, tfst, nt: (te[m], 0, fi)),
                  pl.BlockSpec((1, dm, tf), lambda fi, m, te, tfst, nt: (te[m], 0, fi))],
        out_specs=pl.BlockSpec((tile, tf), lambda fi, m, te, tfst, nt: (clamp(m, nt), fi)),
        scratch_shapes=[pltpu.VMEM((dm, tf), BF16), pltpu.VMEM((dm, tf), BF16)],
    )
    return pl.pallas_call(
        _expert_up_body,
        grid_spec=grid_spec,
        out_shape=jax.ShapeDtypeStruct((r_max, f), BF16),
        compiler_params=_params("arbitrary", "arbitrary"),
        name="expert_up",
    )(tile_expert, tile_first, n_tiles, xs, w_gate, w_up)


def _expert_down_body(te_ref, tf_ref, nt_ref, h_ref, wd_ref, o_ref, wd_s):
    m = pl.program_id(1)

    @pl.when(tf_ref[m] == 1)
    def _():
        wd_s[...] = wd_ref[0].astype(BF16)

    @pl.when(m < nt_ref[0])
    def _():
        o_ref[...] = jnp.dot(h_ref[...], wd_s[...], preferred_element_type=F32)


def _expert_down(hid, w_down, tile_expert, tile_first, n_tiles, tile):
    r_max, f = hid.shape
    dm = w_down.shape[2]
    tn = _pick(dm, (2048, 1024, 512, 256, 128))
    n_mt = r_max // tile
    clamp = lambda m, nt: jnp.minimum(m, nt[0] - 1)
    grid_spec = pltpu.PrefetchScalarGridSpec(
        num_scalar_prefetch=3,
        grid=(dm // tn, n_mt),
        in_specs=[pl.BlockSpec((tile, f), lambda ni, m, te, tfst, nt: (clamp(m, nt), 0)),
                  pl.BlockSpec((1, f, tn), lambda ni, m, te, tfst, nt: (te[m], 0, ni))],
        out_specs=pl.BlockSpec((tile, tn), lambda ni, m, te, tfst, nt: (clamp(m, nt), ni)),
        scratch_shapes=[pltpu.VMEM((f, tn), BF16)],
    )
    return pl.pallas_call(
        _expert_down_body,
        grid_spec=grid_spec,
        out_shape=jax.ShapeDtypeStruct((r_max, dm), F32),
        compiler_params=_params("arbitrary", "arbitrary"),
        name="expert_down",
    )(tile_expert, tile_first, n_tiles, hid, w_down)


def _shared_up_body(x_ref, wg_ref, wu_ref, o_ref):
    x = x_ref[...]
    g = jnp.dot(x, wg_ref[...], preferred_element_type=F32)
    u = jnp.dot(x, wu_ref[...], preferred_element_type=F32)
    o_ref[...] = (jax.nn.silu(g) * u).astype(o_ref.dtype)


def _shared_up(x_bf, wg, wu):
    m, dm = x_bf.shape
    f = wg.shape[1]
    tm = _pick(m, (1408, 1024, 768, 640, 512, 256))
    tf = _pick(f, (256, 128))
    return pl.pallas_call(
        _shared_up_body,
        grid=(m // tm, f // tf),
        in_specs=[pl.BlockSpec((tm, dm), lambda i, j: (i, 0)),
                  pl.BlockSpec((dm, tf), lambda i, j: (0, j)),
                  pl.BlockSpec((dm, tf), lambda i, j: (0, j))],
        out_specs=pl.BlockSpec((tm, tf), lambda i, j: (i, j)),
        out_shape=jax.ShapeDtypeStruct((m, f), BF16),
        compiler_params=_params("arbitrary", "arbitrary"),
        name="shared_up",
    )(x_bf, wg, wu)


def _combine_body(pos_ref, gate_ref, ys_hbm, x_ref, sh_ref, wsd_ref, g_ref, b_ref, o_ref,
                  idx_smem, buf_ref, idx_sem, sems, *, tb, top_k, alpha, eps):
    i = pl.program_id(0)
    n = tb * top_k
    dm = x_ref.shape[1]

    def start(t):
        slot = t % 2
        cp = pltpu.make_async_copy(pos_ref.at[t], idx_smem.at[pl.ds(slot, 1)], idx_sem)
        cp.start()
        cp.wait()

        def issue(g, carry):
            for u in range(DMA_THREADS):
                r = g * DMA_THREADS + u
                pltpu.make_async_copy(ys_hbm.at[pl.ds(idx_smem[slot, r], 1), :],
                                      buf_ref.at[slot, pl.ds(r, 1), :], sems.at[slot]).start(priority=u)
            return carry
        lax.fori_loop(0, n // DMA_THREADS, issue, 0)

    @pl.when(i == 0)
    def _():
        start(0)

    @pl.when(i + 1 < pl.num_programs(0))
    def _():
        start(i + 1)

    shared = jnp.dot(sh_ref[...], wsd_ref[...], preferred_element_type=F32)
    slot = i % 2
    pltpu.make_async_copy(buf_ref.at[slot], buf_ref.at[slot], sems.at[slot]).wait()
    routed = shared
    for k in range(top_k):
        gate = jnp.concatenate([gate_ref[k * tb:(k + 1) * tb, :]] * (dm // LANE), axis=1)
        routed = routed + gate * buf_ref[slot, k * tb:(k + 1) * tb, :]
    o_ref[...] = _layer_norm(F32(alpha) * x_ref[...] + routed, g_ref[...], b_ref[...], eps)


def _combine(pos, gate_rows, ys, x, sh_hid, ws_down, g, b, alpha, tb, cfg):
    m, dm = x.shape
    f = sh_hid.shape[1]
    n = tb * cfg.top_k
    nblk = m // tb
    return pl.pallas_call(
        functools.partial(_combine_body, tb=tb, top_k=cfg.top_k, alpha=alpha, eps=cfg.ln_eps),
        grid=(nblk,),
        in_specs=[pl.BlockSpec((nblk, 1, n), lambda i: (0, 0, 0)),
                  pl.BlockSpec((n, LANE), lambda i: (i, 0)),
                  pl.BlockSpec(memory_space=pl.ANY),
                  pl.BlockSpec((tb, dm), lambda i: (i, 0)),
                  pl.BlockSpec((tb, f), lambda i: (i, 0)),
                  pl.BlockSpec((f, dm), lambda i: (0, 0)),
                  pl.BlockSpec((1, dm), lambda i: (0, 0)),
                  pl.BlockSpec((1, dm), lambda i: (0, 0))],
        out_specs=pl.BlockSpec((tb, dm), lambda i: (i, 0)),
        out_shape=jax.ShapeDtypeStruct((m, dm), F32),
        scratch_shapes=[pltpu.SMEM((2, n), I32), pltpu.VMEM((2, n, dm), F32),
                        pltpu.SemaphoreType.DMA(()), pltpu.SemaphoreType.DMA((2,))],
        compiler_params=_params("arbitrary"),
        name="moe_combine",
    )(pos, gate_rows, ys, x, sh_hid, ws_down, g, b)


def _routing_tables(ids, gates, ranks, counts, n_tok, tile, tb, cfg):
    e, k = cfg.n_experts, cfg.top_k
    t_all = ids.shape[1]
    r_max = ((n_tok * k + e * (tile - 1)) // tile + 1) * tile
    counts = counts[:, 0].astype(I32)
    padded = ((counts + tile - 1) // tile) * tile
    ends = jnp.cumsum(padded)
    start_pad = ends - padded
    eye = ids[:, :, None] == jnp.arange(e, dtype=I32)[None, None, :]
    dest = jnp.sum(jnp.where(eye, start_pad[None, None, :], 0), axis=2) + ranks
    tok = jnp.broadcast_to(jnp.arange(t_all, dtype=I32)[None, :], (k, t_all))
    real = tok < n_tok
    src_rows = jnp.zeros((r_max,), I32).at[jnp.where(real, dest, r_max).reshape(-1)].set(
        tok.reshape(-1), mode="drop", unique_indices=True)
    blocks = lambda a: a.reshape(k, t_all // tb, tb).transpose(1, 0, 2).reshape(t_all // tb, 1, k * tb)
    pos = blocks(jnp.where(real, dest, 0))
    gate_rows = jnp.broadcast_to(blocks(jnp.where(real, gates, 0.0)).reshape(-1, 1), (t_all * k, LANE))
    n_rows = ends[-1:]
    tile_start = jnp.arange(r_max // tile, dtype=I32) * tile
    tile_expert = jnp.minimum(jnp.sum((ends[None, :] <= tile_start[:, None]).astype(I32), axis=1), e - 1)
    prev = jnp.concatenate([jnp.full((1,), -1, I32), tile_expert[:-1]])
    tile_first = (tile_expert != prev).astype(I32)
    return src_rows, pos, gate_rows, n_rows, n_rows // tile, tile_expert, tile_first


def _layer(cfg, l, x_all, n_p, seq, n_s, pos_all, cache_kv, cache_idx_k, page_table, state_conv, state_h,
           w_in, ik_g, ik_b, conv_w, conv_b, lru_w_a, lru_b_a, lru_w_x, lru_b_x, lru_lambda,
           w_o_attn, w_o_lru, w_out, ln1_g, ln1_b, w_router, b_router,
           w_gate, w_up, w_down, ws_gate, ws_up, ws_down, ln2_g, ln2_b):
    t_all, dm = x_all.shape
    t_p = n_p * seq
    d, di = cfg.head_dim, cfg.idx_dim
    qw, kvw, iw = cfg.n_heads * d, cfg.n_kv_heads * d, cfg.idx_heads * di
    lw = conv_w.shape[1]
    alpha = (2.0 * cfg.depth) ** 0.25
    c_qkv = qw + 2 * kvw
    c_i = c_qkv + iw + di
    c_w = c_i + cfg.idx_heads

    x_bf = x_all.astype(BF16)
    w_in_bf = w_in.astype(BF16)
    z_qkv = _matmul(x_bf, w_in_bf, 0, c_qkv, name="in_proj_qkv")
    z_i = _matmul(x_bf, w_in_bf, c_qkv, iw, name="in_proj_idx_q")
    z_kw = _matmul(x_bf, w_in_bf, c_qkv + iw, 2 * LANE, name="in_proj_idx_kw")
    z_b = _matmul(x_bf, w_in_bf[:, c_w:], name="in_proj_lru_gates")

    cosf, sinf = _rope_tables(pos_all, d, cfg.rope_theta)
    kv_all, ki_all, k_bf, ki_bf = _finalize_keys(
        z_qkv, z_kw, cosf, sinf, ik_g.reshape(1, di), ik_b.reshape(1, di), cfg)
    v_t = _transpose_values(z_qkv, t_p, cfg)

    o_attn_p = _prompt_attention(z_qkv, z_i, z_kw, cosf, sinf, k_bf, ki_bf, v_t, n_p, seq, cfg)
    wa_bf, wx_bf = lru_w_a.astype(BF16), lru_w_x.astype(BF16)
    row = lambda v: v.reshape(1, -1)
    h_lru_p, h_last_p = _lru_prompt(z_b, conv_w, row(conv_b), wa_bf, wx_bf, row(lru_b_a), row(lru_b_x),
                                    row(lru_lambda), n_p, seq, cfg)

    sl = slice(t_p, t_p + n_s)
    cos_s, sin_s = cosf[sl][:, None, :], sinf[sl][:, None, :]
    qi_s = z_i[sl].reshape(n_s, cfg.idx_heads, di)
    w_rep = jnp.broadcast_to(z_kw[sl, di:di + cfg.idx_heads][:, :, None], (n_s, cfg.idx_heads, LANE))
    n_phys = cache_idx_k.shape[0]
    scores = _sample_scores(page_table, qi_s, w_rep, ki_all[sl][:, None, :], cos_s, sin_s, cache_idx_k, cfg)
    q_s = z_qkv[sl, :qw].reshape(n_s, cfg.n_heads, d)
    kv_new = kv_all[sl].reshape(n_s, 2 * cfg.n_kv_heads, d)
    o_attn_s = _sample_attend(page_table, scores, q_s, kv_new, cos_s, sin_s,
                              cache_kv.reshape(n_phys, cfg.page_size * 2 * cfg.n_kv_heads, d), cfg)
    xl_s = z_b[sl, :lw]
    hist_s = jnp.moveaxis(state_conv, 1, 0)
    h_s, h_s_bf = _lru_sample(xl_s, hist_s, state_h, conv_w, row(conv_b), wa_bf, wx_bf,
                              row(lru_b_a), row(lru_b_x), row(lru_lambda), cfg)

    pad = t_all - t_p - n_s
    o_attn = jnp.concatenate([o_attn_p, o_attn_s.reshape(n_s, qw), jnp.zeros((pad, qw), BF16)], axis=0)
    h_lru = jnp.concatenate([h_lru_p, h_s_bf, jnp.zeros((pad, lw), BF16)], axis=0)

    merged = _merge(o_attn, h_lru, w_o_attn.astype(BF16), w_o_lru.astype(BF16), z_b, lw, cfg)
    x1, x1_bf, x1_pk = _proj_residual_ln(merged, w_out.astype(BF16), x_all, row(ln1_g), row(ln1_b), alpha, cfg)

    n_tok = t_p + n_s
    ids, gates, ranks, counts = _router(x1_bf, w_router.T.astype(BF16), b_router.reshape(-1, 1), n_tok, cfg)
    tile = cfg.moe_tile
    tb = _pick(t_all, (64, 32, 16, 8))
    src_rows, pos, gate_rows, n_rows, n_tiles, tile_expert, tile_first = _routing_tables(
        ids, gates, ranks, counts, n_tok, tile, tb, cfg)
    xs = _dispatch(x1_pk.reshape(t_all, dm // (2 * LANE), LANE), src_rows, n_rows, tile)
    hid = _expert_up(xs, w_gate, w_up, tile_expert, tile_first, n_tiles, tile)
    ys = _expert_down(hid, w_down, tile_expert, tile_first, n_tiles, tile)
    sh_hid = _shared_up(x1_bf, ws_gate.astype(BF16), ws_up.astype(BF16))
    y_all = _combine(pos, gate_rows, ys, x1, sh_hid, ws_down.astype(BF16), row(ln2_g), row(ln2_b),
                     alpha, tb, cfg)

    conv_p = z_b[:t_p, :lw].reshape(n_p, seq, lw)[:, seq - (cfg.conv_width - 1):, :]
    conv_s = jnp.concatenate([state_conv[:, 1:, :], xl_s[:, None, :]], axis=1)
    outs = dict(
        kv_p=kv_all[:t_p].reshape(n_p, seq, 2, cfg.n_kv_heads, d),
        ik_p=ki_all[:t_p].reshape(n_p, seq, di),
        cv_p=conv_p, h_p=h_last_p.reshape(n_p, lw),
        kv_s=kv_all[sl].reshape(n_s, 1, 2, cfg.n_kv_heads, d),
        ik_s=ki_all[sl].reshape(n_s, 1, di),
        cv_s=conv_s, h_s=h_s)
    return y_all, outs


def _forward(cfg, x_prompt, x_sample, cache_kv, cache_idx_k, page_table, state_conv, state_h,
             w_in, idx_k_norm_g, idx_k_norm_b, conv_w, conv_b, lru_w_a, lru_b_a, lru_w_x, lru_b_x,
             lru_lambda, w_o_attn, w_o_lru, w_out, ln1_g, ln1_b, w_router, b_router,
             w_gate, w_up, w_down, ws_gate, ws_up, ws_down, ln2_g, ln2_b):
    n_p, seq, dm = x_prompt.shape
    n_s, t_s, _ = x_sample.shape
    assert t_s == 1 and cfg.depth == 1 and w_in.shape[0] == 1
    past = page_table.shape[1] * cfg.page_size
    t_p = n_p * seq
    t_all = -(-(t_p + n_s) // cfg.row_align) * cfg.row_align
    pad = t_all - t_p - n_s
    x_all = jnp.concatenate([x_prompt.reshape(t_p, dm), x_sample.reshape(n_s, dm),
                             jnp.zeros((pad, dm), x_prompt.dtype)], axis=0)
    pos_all = jnp.concatenate([jnp.tile(jnp.arange(seq), n_p), jnp.full((n_s,), past), jnp.zeros((pad,), I32)])
    l = 0
    y_all, o = _layer(cfg, l, x_all, n_p, seq, n_s, pos_all, cache_kv[l], cache_idx_k[l], page_table,
                      state_conv[l], state_h[l], w_in[l], idx_k_norm_g[l], idx_k_norm_b[l],
                      conv_w[l], conv_b[l], lru_w_a[l], lru_b_a[l], lru_w_x[l], lru_b_x[l], lru_lambda[l],
                      w_o_attn[l], w_o_lru[l], w_out[l], ln1_g[l], ln1_b[l], w_router[l], b_router[l],
                      w_gate[l], w_up[l], w_down[l], ws_gate[l], ws_up[l], ws_down[l], ln2_g[l], ln2_b[l])
    y_p = y_all[:t_p].reshape(n_p, seq, dm)
    y_s = y_all[t_p:t_p + n_s].reshape(n_s, 1, dm)
    lead = lambda a: a[None]
    return (y_p, y_s, lead(o["kv_p"]), lead(o["ik_p"]), lead(o["cv_p"]), lead(o["h_p"]),
            lead(o["kv_s"]), lead(o["ik_s"]), lead(o["cv_s"]), lead(o["h_s"]))


def kernel(x_prompt, x_sample, cache_kv, cache_idx_k, page_table, state_conv, state_h, w_in, idx_k_norm_g, idx_k_norm_b, conv_w, conv_b, lru_w_a, lru_b_a, lru_w_x, lru_b_x, lru_lambda, w_o_attn, w_o_lru, w_out, ln1_g, ln1_b, w_router, b_router, w_gate, w_up, w_down, ws_gate, ws_up, ws_down, ln2_g, ln2_b):
    return _forward(Cfg(), x_prompt, x_sample, cache_kv, cache_idx_k, page_table, state_conv, state_h,
                    w_in, idx_k_norm_g, idx_k_norm_b, conv_w, conv_b, lru_w_a, lru_b_a, lru_w_x, lru_b_x,
                    lru_lambda, w_o_attn, w_o_lru, w_out, ln1_g, ln1_b, w_router, b_router,
                    w_gate, w_up, w_down, ws_gate, ws_up, ws_down, ln2_g, ln2_b)
```

```python
import functools
from typing import NamedTuple

import jax
import jax.numpy as jnp
import numpy as np
from jax import lax
from jax.experimental import pallas as pl
from jax.experimental.pallas import tpu as pltpu

F32 = jnp.float32
BF16 = jnp.bfloat16
I32 = jnp.int32
U32 = jnp.uint32

LANE = 128
SUBLANE = 8
VMEM_LIMIT = 56 * 1024 * 1024
MASKED = -1e30
INT_MIN = -2 ** 31


class Cfg(NamedTuple):
    n_heads: int = 16
    n_kv_heads: int = 4
    head_dim: int = 128
    idx_heads: int = 32
    idx_dim: int = 128
    topk_max: int = 256
    q_block: int = 128
    rope_theta: float = 10000.0
    lru_blocks: int = 16
    conv_width: int = 4
    lru_c: float = 8.0
    n_experts: int = 64
    top_k: int = 8
    n_groups: int = 8
    topk_groups: int = 4
    route_scale: float = 2.5
    ln_eps: float = 1e-5
    page_size: int = 128
    depth: int = 1
    key_chunk: int = 512
    moe_tile: int = 512
    row_align: int = 256


def _pick(dim, prefs):
    for p in prefs:
        if p <= dim and dim % p == 0:
            return p
    return dim


def _params(*sem):
    return pltpu.CompilerParams(dimension_semantics=sem, vmem_limit_bytes=VMEM_LIMIT)


def _mm_body(x_ref, w_ref, o_ref):
    o_ref[...] = jnp.dot(x_ref[...], w_ref[...], preferred_element_type=F32).astype(o_ref.dtype)


def _matmul(x, w, col0=0, n=None, out_dtype=F32, name="matmul"):
    m, k = x.shape
    n = w.shape[1] - col0 if n is None else n
    tm = _pick(m, (1408, 1024, 768, 640, 512, 256))
    tn = _pick(n, (512, 384, 256, 128))
    assert col0 % tn == 0
    j0 = col0 // tn
    return pl.pallas_call(
        _mm_body,
        grid=(m // tm, n // tn),
        in_specs=[pl.BlockSpec((tm, k), lambda i, j: (i, 0)),
                  pl.BlockSpec((k, tn), lambda i, j: (0, j0 + j))],
        out_specs=pl.BlockSpec((tm, tn), lambda i, j: (i, j)),
        out_shape=jax.ShapeDtypeStruct((m, n), out_dtype),
        compiler_params=_params("arbitrary", "arbitrary"),
        name=name,
    )(x, w)


def _rope(x, cosf, sinf):
    return x * cosf + pltpu.roll(x, x.shape[-1] // 2, axis=x.ndim - 1) * sinf


def _rope_tables(pos, dim, theta):
    half = dim // 2
    inv = theta ** (-jnp.arange(half, dtype=F32) / half)
    ang = pos.astype(F32)[:, None] * inv[None, :]
    cos, sin = jnp.cos(ang), jnp.sin(ang)
    return jnp.concatenate([cos, cos], -1), jnp.concatenate([-sin, sin], -1)


def _pack_halves(y):
    w = y.shape[1] // 2
    lo = pltpu.bitcast(y[:, :w].astype(BF16).astype(F32), U32)
    hi = pltpu.bitcast(y[:, w:].astype(BF16).astype(F32), U32)
    return (hi & jnp.uint32(0xFFFF0000)) | (lo >> 16)


def _unpack_halves(p):
    lo = pltpu.bitcast(p << 16, F32)
    hi = pltpu.bitcast(p & jnp.uint32(0xFFFF0000), F32)
    return lo, hi


def _layer_norm(y, g, b, eps):
    mu = jnp.mean(y, axis=-1, keepdims=True)
    yc = y - mu
    var = jnp.mean(yc * yc, axis=-1, keepdims=True)
    return yc * lax.rsqrt(var + eps) * g + b


def _kv_body(kv_ref, ki_ref, cos_ref, sin_ref, g_ref, b_ref,
             kvo_ref, kio_ref, kbf_ref, kibf_ref, *, cfg):
    c_heads, d = cfg.n_kv_heads, cfg.head_dim
    cosf, sinf = cos_ref[...], sin_ref[...]
    kv = kv_ref[...]
    for c in range(c_heads):
        kc = _rope(kv[:, c * d:(c + 1) * d], cosf, sinf)
        kvo_ref[:, c * d:(c + 1) * d] = kc
        kbf_ref[:, c * d:(c + 1) * d] = kc.astype(BF16)
    kvo_ref[:, c_heads * d:] = kv[:, c_heads * d:]
    ki = _rope(_layer_norm(ki_ref[...], g_ref[...], b_ref[...], cfg.ln_eps), cosf, sinf)
    kio_ref[...] = ki
    kibf_ref[...] = ki.astype(BF16)


def _vt_body(v_ref, vt_ref, *, cfg):
    d = cfg.head_dim
    for c in range(cfg.n_kv_heads):
        vt_ref[0, c * d:(c + 1) * d, :] = v_ref[:, c * d:(c + 1) * d].T.astype(BF16)


def _transpose_values(z_qkv, t_p, cfg):
    kc = cfg.key_chunk
    kvw = cfg.n_kv_heads * cfg.head_dim
    v_blk = (cfg.n_heads * cfg.head_dim + kvw) // kvw
    assert t_p % kc == 0
    return pl.pallas_call(
        functools.partial(_vt_body, cfg=cfg),
        grid=(t_p // kc,),
        in_specs=[pl.BlockSpec((kc, kvw), lambda i: (i, v_blk))],
        out_specs=pl.BlockSpec((1, kvw, kc), lambda i: (i, 0, 0)),
        out_shape=jax.ShapeDtypeStruct((t_p // kc, kvw, kc), BF16),
        compiler_params=_params("arbitrary"),
        name="transpose_values",
    )(z_qkv)


def _finalize_keys(z_qkv, z_i, cosf, sinf, ik_g, ik_b, cfg):
    t_all = z_qkv.shape[0]
    c_heads, d = cfg.n_kv_heads, cfg.head_dim
    kvw = c_heads * d
    qw = cfg.n_heads * d
    tr = cfg.row_align
    assert t_all % tr == 0 and qw % (2 * kvw) == 0
    return pl.pallas_call(
        functools.partial(_kv_body, cfg=cfg),
        grid=(t_all // tr,),
        in_specs=[pl.BlockSpec((tr, 2 * kvw), lambda i: (i, qw // (2 * kvw))),
                  pl.BlockSpec((tr, cfg.idx_dim), lambda i: (i, 0)),
                  pl.BlockSpec((tr, d), lambda i: (i, 0)),
                  pl.BlockSpec((tr, d), lambda i: (i, 0)),
                  pl.BlockSpec((1, cfg.idx_dim), lambda i: (0, 0)),
                  pl.BlockSpec((1, cfg.idx_dim), lambda i: (0, 0))],
        out_specs=[pl.BlockSpec((tr, 2 * kvw), lambda i: (i, 0)),
                   pl.BlockSpec((tr, cfg.idx_dim), lambda i: (i, 0)),
                   pl.BlockSpec((tr, kvw), lambda i: (i, 0)),
                   pl.BlockSpec((tr, cfg.idx_dim), lambda i: (i, 0))],
        out_shape=[jax.ShapeDtypeStruct((t_all, 2 * kvw), F32),
                   jax.ShapeDtypeStruct((t_all, cfg.idx_dim), F32),
                   jax.ShapeDtypeStruct((t_all, kvw), BF16),
                   jax.ShapeDtypeStruct((t_all, cfg.idx_dim), BF16)],
        compiler_params=_params("arbitrary"),
        name="finalize_keys",
    )(z_qkv, z_i, cosf, sinf, ik_g, ik_b)


def _sort_key(s):
    b = pltpu.bitcast(s, I32)
    return b ^ ((b >> 31) & jnp.int32(0x7FFFFFFF))


def _kth_largest_key(count_ge, k, width):
    def bit_step(i, ans):
        cand = ans | lax.shift_left(jnp.int32(1), jnp.int32(31) - i)
        cnt = count_ge(cand ^ jnp.int32(INT_MIN))
        return jnp.where(cnt >= k, cand, ans)

    ans = lax.fori_loop(0, 32, bit_step, jnp.zeros((1, width), I32))
    return ans ^ jnp.int32(INT_MIN)


def _attn_body(q_ref, qi_ref, wi_ref, cos_ref, sin_ref, kbf_ref, kibf_ref, vt_ref,
               o_ref, qs_ref, qis_ref, key_ref, m_ref, l_ref, acc_ref, *, cfg, topk):
    qb_idx = pl.program_id(1)
    qb, kc = cfg.q_block, cfg.key_chunk
    d, di = cfg.head_dim, cfg.idx_dim
    c_heads = cfg.n_kv_heads
    g_heads = cfg.n_heads // c_heads
    cosf, sinf = cos_ref[...], sin_ref[...]

    for h in range(cfg.n_heads):
        qs_ref[h * qb:(h + 1) * qb, :] = _rope(q_ref[:, h * d:(h + 1) * d], cosf, sinf).astype(BF16)
    for h in range(cfg.idx_heads):
        qis_ref[h * qb:(h + 1) * qb, :] = _rope(qi_ref[:, h * di:(h + 1) * di], cosf, sinf).astype(BF16)
    w_t = wi_ref[...].T * F32((di * cfg.idx_heads) ** -0.5)

    q0 = qb_idx * qb
    n_chunks = (q0 + qb + kc - 1) // kc
    tpos = q0 + lax.broadcasted_iota(I32, (kc, qb), 1)
    krow = lax.broadcasted_iota(I32, (kc, qb), 0)
    nt = (((1,), (1,)), ((), ()))

    def score_chunk(ci, carry):
        k0 = pl.multiple_of(ci * kc, kc)
        ki_c = kibf_ref[pl.ds(k0, kc), :]
        acc = jnp.zeros((kc, qb), F32)
        for hp in range(cfg.idx_heads // 2):
            z = lax.dot_general(ki_c, qis_ref[hp * 2 * qb:(hp + 1) * 2 * qb, :], nt,
                                preferred_element_type=F32)
            acc = acc + jnp.maximum(z[:, :qb], 0.0) * w_t[2 * hp:2 * hp + 1, :]
            acc = acc + jnp.maximum(z[:, qb:], 0.0) * w_t[2 * hp + 1:2 * hp + 2, :]
        causal = (k0 + krow) <= tpos
        key_ref[pl.ds(k0, kc), :] = _sort_key(jnp.where(causal, acc, -jnp.inf))
        return carry

    lax.fori_loop(0, n_chunks, score_chunk, 0)

    def count_ge(cand):
        def cnt_chunk(ci, acc8):
            k0 = pl.multiple_of(ci * kc, kc)
            hit = jnp.where(key_ref[pl.ds(k0, kc), :] >= cand, 1, 0).astype(I32)
            return acc8 + jnp.sum(hit.reshape(kc // SUBLANE, SUBLANE, qb), axis=0)
        acc8 = lax.fori_loop(0, n_chunks, cnt_chunk, jnp.zeros((SUBLANE, qb), I32))
        return jnp.sum(acc8, axis=0, keepdims=True)

    thr = _kth_largest_key(count_ge, topk, qb)

    m_ref[...] = jnp.full(m_ref.shape, MASKED, F32)
    l_ref[...] = jnp.zeros(l_ref.shape, F32)
    acc_ref[...] = jnp.zeros(acc_ref.shape, F32)
    scale = F32(d ** -0.5 * np.log2(np.e))

    def attend_chunk(ci, carry):
        k0 = pl.multiple_of(ci * kc, kc)
        keep = (key_ref[pl.ds(k0, kc), :] >= thr) & ((k0 + krow) <= tpos)
        bias = jnp.where(keep, 0.0, MASKED).astype(F32)
        bias = jnp.concatenate([bias] * g_heads, axis=1)
        for c in range(c_heads):
            k_c = kbf_ref[pl.ds(k0, kc), c * d:(c + 1) * d]
            s = lax.dot_general(k_c, qs_ref[c * g_heads * qb:(c + 1) * g_heads * qb, :], nt,
                                preferred_element_type=F32) * scale + bias
            m_old = m_ref[c:c + 1, :]
            m_new = jnp.maximum(m_old, jnp.max(s, axis=0, keepdims=True))
            alpha = jnp.exp2(m_old - m_new)
            p = jnp.exp2(s - m_new)
            l_ref[c:c + 1, :] = alpha * l_ref[c:c + 1, :] + jnp.sum(p, axis=0, keepdims=True)
            pv = jnp.dot(vt_ref[ci, c * d:(c + 1) * d, :], p.astype(BF16), preferred_element_type=F32)
            acc_ref[c * d:(c + 1) * d, :] = alpha * acc_ref[c * d:(c + 1) * d, :] + pv
            m_ref[c:c + 1, :] = m_new
        return carry

    lax.fori_loop(0, n_chunks, attend_chunk, 0)

    for c in range(c_heads):
        o_t = acc_ref[c * d:(c + 1) * d, :] / l_ref[c:c + 1, :]
        for g in range(g_heads):
            h = c * g_heads + g
            o_ref[:, h * d:(h + 1) * d] = o_t[:, g * qb:(g + 1) * qb].T.astype(o_ref.dtype)


def _prompt_attention(z_qkv, z_i, z_w, cosf, sinf, k_bf, ki_bf, v_t, n_seq, seq, cfg):
    qb, kc = cfg.q_block, cfg.key_chunk
    d = cfg.head_dim
    qw = cfg.n_heads * d
    kvw = cfg.n_kv_heads * d
    iw = cfg.idx_heads * cfg.idx_dim
    assert seq % kc == 0 and seq % qb == 0 and kc % qb == 0
    assert cfg.head_dim == LANE and cfg.idx_dim == LANE
    nqb = seq // qb
    topk = min(cfg.topk_max, seq // 4)
    row = lambda n, j: (n * nqb + j, 0)
    return pl.pallas_call(
        functools.partial(_attn_body, cfg=cfg, topk=topk),
        grid=(n_seq, nqb),
        in_specs=[pl.BlockSpec((qb, qw), row),
                  pl.BlockSpec((qb, iw), row),
                  pl.BlockSpec((qb, LANE), lambda n, j: (n * nqb + j, 1)),
                  pl.BlockSpec((qb, d), row),
                  pl.BlockSpec((qb, d), row),
                  pl.BlockSpec((seq, kvw), lambda n, j: (n, 0)),
                  pl.BlockSpec((seq, cfg.idx_dim), lambda n, j: (n, 0)),
                  pl.BlockSpec((seq // kc, kvw, kc), lambda n, j: (n, 0, 0))],
        out_specs=pl.BlockSpec((qb, qw), row),
        out_shape=jax.ShapeDtypeStruct((n_seq * seq, qw), BF16),
        scratch_shapes=[pltpu.VMEM((cfg.n_heads * qb, d), BF16),
                        pltpu.VMEM((cfg.idx_heads * qb, cfg.idx_dim), BF16),
                        pltpu.VMEM((seq, qb), I32),
                        pltpu.VMEM((SUBLANE, (cfg.n_heads // cfg.n_kv_heads) * qb), F32),
                        pltpu.VMEM((SUBLANE, (cfg.n_heads // cfg.n_kv_heads) * qb), F32),
                        pltpu.VMEM((kvw, (cfg.n_heads // cfg.n_kv_heads) * qb), F32)],
        compiler_params=_params("arbitrary", "arbitrary"),
        name="prompt_attention",
    )(z_qkv, z_i, z_w, cosf, sinf, k_bf, ki_bf, v_t)


def _softplus(x):
    return jnp.maximum(x, 0.0) + jnp.log1p(jnp.exp(-jnp.abs(x)))


def _lru_gates(xc, wa_ref, wx_ref, ba, bx, lam, cfg):
    w = xc.shape[1]
    bd = w // cfg.lru_blocks
    xb = xc.astype(BF16)
    r_parts, i_parts = [], []
    for k in range(cfg.lru_blocks):
        xk = xb[:, k * bd:(k + 1) * bd]
        r_parts.append(jnp.dot(xk, wa_ref[k], preferred_element_type=F32))
        i_parts.append(jnp.dot(xk, wx_ref[k], preferred_element_type=F32))
    r = jax.nn.sigmoid(jnp.concatenate(r_parts, axis=1) + ba)
    gate_i = jax.nn.sigmoid(jnp.concatenate(i_parts, axis=1) + bx)
    log_a = (-cfg.lru_c * r) * _softplus(-lam)
    a = jnp.exp(log_a)
    u = jnp.sqrt(-jnp.tanh(log_a) * (a * a + 1.0)) * (gate_i * xc)
    return a, u


def _lru_prompt_body(xl_ref, cw_ref, cb_ref, wa_ref, wx_ref, ba_ref, bx_ref, lam_ref,
                     h_ref, hlast_ref, prev_ref, carry_ref, *, cfg):
    j = pl.program_id(1)
    tb = xl_ref.shape[0]
    cw = cfg.conv_width

    @pl.when(j == 0)
    def _():
        prev_ref[...] = jnp.zeros(prev_ref.shape, F32)
        carry_ref[...] = jnp.zeros(carry_ref.shape, F32)

    xl = xl_ref[...]
    ext = jnp.concatenate([prev_ref[...], xl], axis=0)
    off = SUBLANE - (cw - 1)
    xc = cb_ref[...] + ext[off:off + tb] * cw_ref[0:1, :]
    for t in range(1, cw):
        xc = xc + ext[off + t:off + t + tb] * cw_ref[t:t + 1, :]
    prev_ref[...] = xl[tb - SUBLANE:, :]

    a, u = _lru_gates(xc, wa_ref, wx_ref, ba_ref[...], bx_ref[...], lam_ref[...], cfg)

    row = lax.broadcasted_iota(I32, a.shape, 0) & (SUBLANE - 1)
    s = 1
    while s < SUBLANE:
        ok = row >= s
        a_sh = jnp.where(ok, pltpu.roll(a, s, axis=0), 1.0)
        u_sh = jnp.where(ok, pltpu.roll(u, s, axis=0), 0.0)
        u = u + a * u_sh
        a = a * a_sh
        s *= 2
    h_prev = carry_ref[...]
    for gi in range(tb // SUBLANE):
        sl = slice(gi * SUBLANE, (gi + 1) * SUBLANE)
        h_rows = u[sl] + a[sl] * h_prev
        h_ref[sl, :] = h_rows.astype(h_ref.dtype)
        h_prev = h_rows[SUBLANE - 1:SUBLANE, :]
    carry_ref[...] = h_prev
    hlast_ref[0] = h_prev


def _lru_prompt(z_b, conv_w, conv_b, wa, wx, ba, bx, lam, n_seq, seq, cfg):
    w = conv_w.shape[1]
    tb = _pick(seq, (256, 128))
    nb = seq // tb
    bd = w // cfg.lru_blocks
    vec = pl.BlockSpec((1, w), lambda n, j: (0, 0))
    return pl.pallas_call(
        functools.partial(_lru_prompt_body, cfg=cfg),
        grid=(n_seq, nb),
        in_specs=[pl.BlockSpec((tb, w), lambda n, j: (n * nb + j, 0)),
                  pl.BlockSpec((cfg.conv_width, w), lambda n, j: (0, 0)),
                  vec,
                  pl.BlockSpec((cfg.lru_blocks, bd, bd), lambda n, j: (0, 0, 0)),
                  pl.BlockSpec((cfg.lru_blocks, bd, bd), lambda n, j: (0, 0, 0)),
                  vec, vec, vec],
        out_specs=[pl.BlockSpec((tb, w), lambda n, j: (n * nb + j, 0)),
                   pl.BlockSpec((1, 1, w), lambda n, j: (n, 0, 0))],
        out_shape=[jax.ShapeDtypeStruct((n_seq * seq, w), BF16),
                   jax.ShapeDtypeStruct((n_seq, 1, w), F32)],
        scratch_shapes=[pltpu.VMEM((SUBLANE, w), F32), pltpu.VMEM((1, w), F32)],
        compiler_params=_params("arbitrary", "arbitrary"),
        name="lru_prompt",
    )(z_b, conv_w, conv_b, wa, wx, ba, bx, lam)


def _lru_sample_body(xl_ref, hist_ref, h0_ref, cw_ref, cb_ref, wa_ref, wx_ref, ba_ref, bx_ref, lam_ref,
                     h_ref, hbf_ref, *, cfg):
    cw = cfg.conv_width
    xc = cb_ref[...] + hist_ref[0] * cw_ref[0:1, :]
    for t in range(1, cw - 1):
        xc = xc + hist_ref[t] * cw_ref[t:t + 1, :]
    xc = xc + xl_ref[...] * cw_ref[cw - 1:cw, :]
    a, u = _lru_gates(xc, wa_ref, wx_ref, ba_ref[...], bx_ref[...], lam_ref[...], cfg)
    h = a * h0_ref[...] + u
    h_ref[...] = h
    hbf_ref[...] = h.astype(BF16)


def _lru_sample(xl, hist, h0, conv_w, conv_b, wa, wx, ba, bx, lam, cfg):
    n, w = xl.shape
    return pl.pallas_call(
        functools.partial(_lru_sample_body, cfg=cfg),
        out_shape=[jax.ShapeDtypeStruct((n, w), F32), jax.ShapeDtypeStruct((n, w), BF16)],
        compiler_params=pltpu.CompilerParams(vmem_limit_bytes=VMEM_LIMIT),
        name="lru_sample",
    )(xl, hist, h0, conv_w, conv_b, wa, wx, ba, bx, lam)


def _sample_scores_body(pt_ref, qi_ref, w_ref, kin_ref, cos_ref, sin_ref, *rest, cfg, pages_per_step):
    page_refs, o_ref = rest[:pages_per_step], rest[pages_per_step]
    j = pl.program_id(1)
    nj = pl.num_programs(1)
    cosf, sinf = cos_ref[0], sin_ref[0]
    qi = _rope(qi_ref[0], cosf, sinf)
    qi_b = qi.astype(BF16)
    w = w_ref[0] * F32((cfg.idx_dim * cfg.idx_heads) ** -0.5)
    nt = (((1,), (1,)), ((), ()))
    rows = []
    for p in range(pages_per_step):
        z = lax.dot_general(qi_b, page_refs[p][0].astype(BF16), nt, preferred_element_type=F32)
        rows.append(jnp.sum(jnp.maximum(z, 0.0) * w, axis=0, keepdims=True))
    r0 = pl.multiple_of(j * pages_per_step, pages_per_step)
    o_ref[0, pl.ds(r0, pages_per_step), :] = jnp.concatenate(rows, axis=0)

    @pl.when(j == nj - 1)
    def _():
        z_new = jnp.sum(qi * kin_ref[0], axis=1, keepdims=True)
        s_new = jnp.sum(jnp.maximum(z_new, 0.0) * w, axis=0, keepdims=True)
        lane = lax.broadcasted_iota(I32, (SUBLANE, LANE), 1)
        sub = lax.broadcasted_iota(I32, (SUBLANE, LANE), 0)
        tail = jnp.where((lane == 0) & (sub == 0), jnp.broadcast_to(s_new, (SUBLANE, LANE)), -jnp.inf)
        o_ref[0, pl.ds(nj * pages_per_step, SUBLANE), :] = tail


def _sample_scores(page_table, qi, w_rep, ki_new, cos_s, sin_s, cache_idx_k, cfg):
    n, n_pages = page_table.shape
    pps = _pick(n_pages, (SUBLANE,))
    assert cfg.page_size == LANE
    hi, di = cfg.idx_heads, cfg.idx_dim
    per_seq = lambda b, j, pt: (b, 0, 0)
    page_specs = [pl.BlockSpec((1, cfg.page_size, di),
                               functools.partial(lambda b, j, pt, p: (pt[b, j * pps + p], 0, 0), p=p))
                  for p in range(pps)]
    grid_spec = pltpu.PrefetchScalarGridSpec(
        num_scalar_prefetch=1,
        grid=(n, n_pages // pps),
        in_specs=[pl.BlockSpec((1, hi, di), per_seq),
                  pl.BlockSpec((1, hi, LANE), per_seq),
                  pl.BlockSpec((1, 1, di), per_seq),
                  pl.BlockSpec((1, 1, di), per_seq),
                  pl.BlockSpec((1, 1, di), per_seq)] + page_specs,
        out_specs=pl.BlockSpec((1, n_pages + SUBLANE, LANE), per_seq),
    )
    return pl.pallas_call(
        functools.partial(_sample_scores_body, cfg=cfg, pages_per_step=pps),
        grid_spec=grid_spec,
        out_shape=jax.ShapeDtypeStruct((n, n_pages + SUBLANE, LANE), F32),
        compiler_params=_params("arbitrary", "arbitrary"),
        name="sample_scores",
    )(page_table, qi, w_rep, ki_new, cos_s, sin_s, *([cache_idx_k] * pps))


def _sample_attend_body(pt_ref, s_ref, q_ref, kvn_ref, cos_ref, sin_ref, *rest,
                        cfg, pages_per_step, topk):
    page_refs = rest[:pages_per_step]
    o_ref, qs_ref, keep_ref, m_ref, l_ref, acc_ref = rest[pages_per_step:]
    j = pl.program_id(1)
    nj = pl.num_programs(1)
    nh, d = cfg.n_heads, cfg.head_dim
    c_heads = cfg.n_kv_heads
    g_heads = nh // c_heads
    ps = cfg.page_size
    n_rows = s_ref.shape[1]
    scale = F32(d ** -0.5)
    head_c = lax.broadcasted_iota(I32, (nh, 1), 0) // g_heads

    @pl.when(j == 0)
    def _():
        keys = _sort_key(s_ref[0])

        def count_ge(cand):
            hit = jnp.where(keys >= cand, 1, 0).astype(I32)
            return jnp.sum(jnp.sum(hit, axis=0, keepdims=True), axis=1, keepdims=True)

        thr = _kth_largest_key(count_ge, topk, 1)
        finite = s_ref[0] > -jnp.inf
        keep_ref[...] = jnp.where((keys >= thr) & finite, 0.0, MASKED).astype(F32)
        qs_ref[...] = _rope(q_ref[0], cos_ref[0], sin_ref[0]).astype(BF16)
        m_ref[...] = jnp.full(m_ref.shape, MASKED, F32)
        l_ref[...] = jnp.zeros(l_ref.shape, F32)
        acc_ref[...] = jnp.zeros(acc_ref.shape, F32)

    nt = (((1,), (1,)), ((), ()))
    qs = qs_ref[...]
    per = 2 * c_heads
    width = ps * per
    r0 = pl.multiple_of(j * pages_per_step, pages_per_step)
    keep_rows = keep_ref[pl.ds(r0, pages_per_step), :]
    sel_rows = jnp.where(keep_rows == 0.0, 1.0, 0.0).astype(BF16)
    spread = (lax.broadcasted_iota(I32, (ps, width), 1) // per
              == lax.broadcasted_iota(I32, (ps, width), 0))
    sel_cols = jnp.dot(sel_rows, jnp.where(spread, 1.0, 0.0).astype(BF16),
                       preferred_element_type=F32)
    own_key = (lax.broadcasted_iota(I32, (nh, width), 1) % per) == head_c
    m_run, l_run, acc = m_ref[...], l_ref[...], acc_ref[...]
    for p in range(pages_per_step):
        rows = page_refs[p][0].astype(BF16)
        s = lax.dot_general(qs, rows, nt, preferred_element_type=F32) * scale
        s = jnp.where(own_key & (sel_cols[p:p + 1, :] > 0.5), s, MASKED)
        m_new = jnp.maximum(m_run, jnp.max(s, axis=1, keepdims=True))
        alpha = jnp.exp(m_run - m_new)
        p_un = jnp.exp(s - m_new)
        l_run = alpha * l_run + jnp.sum(p_un, axis=1, keepdims=True)
        p_val = pltpu.roll(p_un, c_heads, axis=1).astype(BF16)
        acc = alpha * acc + jnp.dot(p_val, rows, preferred_element_type=F32)
        m_run = m_new
    m_ref[...], l_ref[...], acc_ref[...] = m_run, l_run, acc

    @pl.when(j == nj - 1)
    def _():
        kvn = kvn_ref[0]
        k_new = jnp.zeros((nh, d), F32)
        v_new = jnp.zeros((nh, d), F32)
        for c in range(c_heads):
            k_new = jnp.where(head_c == c, kvn[c:c + 1, :], k_new)
            v_new = jnp.where(head_c == c, kvn[c_heads + c:c_heads + c + 1, :], v_new)
        q_f = _rope(q_ref[0], cos_ref[0], sin_ref[0])
        bias_new = keep_ref[n_rows - SUBLANE:n_rows - SUBLANE + 1, 0:1]
        s_new = jnp.sum(q_f * k_new, axis=1, keepdims=True) * scale + bias_new
        m_o = m_ref[...]
        m_n = jnp.maximum(m_o, s_new)
        al = jnp.exp(m_o - m_n)
        p_new = jnp.exp(s_new - m_n)
        l_fin = al * l_ref[...] + p_new
        acc_fin = al * acc_ref[...] + p_new * v_new
        o_ref[0] = (acc_fin / l_fin).astype(o_ref.dtype)


def _sample_attend(page_table, scores, q, kv_new, cos_s, sin_s, cache_kv, cfg):
    n, n_pages = page_table.shape
    pps = _pick(n_pages, (SUBLANE,))
    nh, d = cfg.n_heads, cfg.head_dim
    kvw2 = 2 * cfg.n_kv_heads * d
    topk = min(cfg.topk_max, (n_pages * cfg.page_size + 1) // 4)
    per_seq = lambda b, j, pt: (b, 0, 0)
    page_specs = [pl.BlockSpec((1, cfg.page_size * 2 * cfg.n_kv_heads, d),
                               functools.partial(lambda b, j, pt, p: (pt[b, j * pps + p], 0, 0), p=p))
                  for p in range(pps)]
    n_rows = scores.shape[1]
    grid_spec = pltpu.PrefetchScalarGridSpec(
        num_scalar_prefetch=1,
        grid=(n, n_pages // pps),
        in_specs=[pl.BlockSpec((1, n_rows, LANE), per_seq),
                  pl.BlockSpec((1, nh, d), per_seq),
                  pl.BlockSpec((1, 2 * cfg.n_kv_heads, d), per_seq),
                  pl.BlockSpec((1, 1, d), per_seq),
                  pl.BlockSpec((1, 1, d), per_seq)] + page_specs,
        out_specs=pl.BlockSpec((1, nh, d), per_seq),
        scratch_shapes=[pltpu.VMEM((nh, d), BF16),
                        pltpu.VMEM((n_rows, LANE), F32),
                        pltpu.VMEM((nh, 1), F32),
                        pltpu.VMEM((nh, 1), F32),
                        pltpu.VMEM((nh, d), F32)],
    )
    return pl.pallas_call(
        functools.partial(_sample_attend_body, cfg=cfg, pages_per_step=pps, topk=topk),
        grid_spec=grid_spec,
        out_shape=jax.ShapeDtypeStruct((n, nh, d), BF16),
        compiler_params=_params("arbitrary", "arbitrary"),
        name="sample_attend",
    )(page_table, scores, q, kv_new, cos_s, sin_s, *([cache_kv] * pps))


def _merge_body(oa_ref, hl_ref, woa_ref, wol_ref, ga_ref, gb_ref, o_ref):
    ya = jnp.dot(oa_ref[...], woa_ref[...], preferred_element_type=F32)
    yl = jnp.dot(hl_ref[...], wol_ref[...], preferred_element_type=F32)
    o_ref[...] = (jax.nn.sigmoid(ga_ref[...]) * ya + jax.nn.sigmoid(gb_ref[...]) * yl).astype(o_ref.dtype)


def _merge(o_attn, h_lru, w_oa, w_ol, z_b, lru_w, cfg):
    m, aw = o_attn.shape
    dm = w_oa.shape[1]
    tm = _pick(m, (1408, 1024, 768, 640, 512, 256))
    tn = _pick(dm, (512, 256, 128))
    assert lru_w % tn == 0
    ga_off = lru_w // tn
    gb_off = (lru_w + dm) // tn
    return pl.pallas_call(
        _merge_body,
        grid=(m // tm, dm // tn),
        in_specs=[pl.BlockSpec((tm, aw), lambda i, j: (i, 0)),
                  pl.BlockSpec((tm, lru_w), lambda i, j: (i, 0)),
                  pl.BlockSpec((aw, tn), lambda i, j: (0, j)),
                  pl.BlockSpec((lru_w, tn), lambda i, j: (0, j)),
                  pl.BlockSpec((tm, tn), lambda i, j: (i, ga_off + j)),
                  pl.BlockSpec((tm, tn), lambda i, j: (i, gb_off + j))],
        out_specs=pl.BlockSpec((tm, tn), lambda i, j: (i, j)),
        out_shape=jax.ShapeDtypeStruct((m, dm), BF16),
        compiler_params=_params("arbitrary", "arbitrary"),
        name="merge_mixers",
    )(o_attn, h_lru, w_oa, w_ol, z_b, z_b)


def _proj_ln_body(a_ref, w_ref, x_ref, g_ref, b_ref, o_ref, obf_ref, opk_ref, *, alpha, eps, tn):
    j = pl.program_id(1)
    c0 = pl.multiple_of(j * tn, tn)
    y = jnp.dot(a_ref[...], w_ref[...], preferred_element_type=F32)
    o_ref[:, pl.ds(c0, tn)] = F32(alpha) * x_ref[...] + y

    @pl.when(j == pl.num_programs(1) - 1)
    def _():
        out = _layer_norm(o_ref[...], g_ref[...], b_ref[...], eps)
        o_ref[...] = out
        obf_ref[...] = out.astype(BF16)
        opk_ref[...] = _pack_halves(out)


def _proj_residual_ln(a, w, x, g, b, alpha, cfg):
    m, k = a.shape
    dm = w.shape[1]
    tm = _pick(m, (384, 256))
    tn = _pick(dm, (512, 256, 128))
    return pl.pallas_call(
        functools.partial(_proj_ln_body, alpha=alpha, eps=cfg.ln_eps, tn=tn),
        grid=(m // tm, dm // tn),
        in_specs=[pl.BlockSpec((tm, k), lambda i, j: (i, 0)),
                  pl.BlockSpec((k, tn), lambda i, j: (0, j)),
                  pl.BlockSpec((tm, tn), lambda i, j: (i, j)),
                  pl.BlockSpec((1, dm), lambda i, j: (0, 0)),
                  pl.BlockSpec((1, dm), lambda i, j: (0, 0))],
        out_specs=[pl.BlockSpec((tm, dm), lambda i, j: (i, 0)),
                   pl.BlockSpec((tm, dm), lambda i, j: (i, 0)),
                   pl.BlockSpec((tm, dm // 2), lambda i, j: (i, 0))],
        out_shape=[jax.ShapeDtypeStruct((m, dm), F32), jax.ShapeDtypeStruct((m, dm), BF16),
                   jax.ShapeDtypeStruct((m, dm // 2), U32)],
        compiler_params=_params("arbitrary", "arbitrary"),
        name="proj_residual_ln",
    )(a, w, x, g, b)


def _first_index_of_max(v, idx, big):
    m = jnp.max(v, axis=0, keepdims=True)
    first = jnp.min(jnp.where(v == m, idx, big), axis=0, keepdims=True)
    return m, first


def _router_body(x_ref, wr_ref, br_ref, ids_ref, gate_ref, rank_ref, cnt_ref, *, cfg, n_tok):
    i = pl.program_id(0)
    e, ng = cfg.n_experts, cfg.n_groups
    per = e // ng
    tm = x_ref.shape[0]
    nt = (((1,), (1,)), ((), ()))
    logits = lax.dot_general(wr_ref[...], x_ref[...], nt, preferred_element_type=F32)
    s = jax.nn.sigmoid(logits)
    choice = s + br_ref[...]
    eidx = lax.broadcasted_iota(I32, (e, tm), 0)

    grp_rows = []
    jidx = lax.broadcasted_iota(I32, (per, tm), 0)
    for g in range(ng):
        cg = choice[g * per:(g + 1) * per, :]
        m1, j1 = _first_index_of_max(cg, jidx, per)
        m2 = jnp.max(jnp.where(jidx == j1, -jnp.inf, cg), axis=0, keepdims=True)
        grp_rows.append(m1 + m2)
    grp = jnp.concatenate(grp_rows, axis=0)

    gidx = lax.broadcasted_iota(I32, (ng, tm), 0)
    grp_keep = jnp.zeros((ng, tm), F32)
    work = grp
    for _ in range(cfg.topk_groups):
        _, gsel = _first_index_of_max(work, gidx, ng)
        hit = gidx == gsel
        grp_keep = jnp.where(hit, 1.0, grp_keep)
        work = jnp.where(hit, -jnp.inf, work)

    keep_rows = [jnp.broadcast_to(grp_keep[g:g + 1, :], (per, tm)) for g in range(ng)]
    masked = jnp.where(jnp.concatenate(keep_rows, axis=0) > 0.5, choice, -jnp.inf)

    ids, wts, hits = [], [], []
    for _ in range(cfg.top_k):
        _, esel = _first_index_of_max(masked, eidx, e)
        hit = eidx == esel
        ids.append(esel)
        hits.append(hit)
        wts.append(jnp.sum(jnp.where(hit, s, 0.0), axis=0, keepdims=True))
        masked = jnp.where(hit, -jnp.inf, masked)
    wk = jnp.concatenate(wts, axis=0)
    ids_ref[...] = jnp.concatenate(ids, axis=0)
    gate_ref[...] = F32(cfg.route_scale) * wk / jnp.sum(wk, axis=0, keepdims=True)

    @pl.when(i == 0)
    def _():
        cnt_ref[...] = jnp.zeros(cnt_ref.shape, F32)

    tok = i * tm + lax.broadcasted_iota(I32, (e, tm), 1)
    picked = jnp.zeros((e, tm), F32)
    for h in hits:
        picked = jnp.where(h, 1.0, picked)
    picked = jnp.where(tok < n_tok, picked, 0.0)
    before = (lax.broadcasted_iota(I32, (tm, tm), 0) < lax.broadcasted_iota(I32, (tm, tm), 1))
    earlier = jnp.dot(picked.astype(BF16), jnp.where(before, 1.0, 0.0).astype(BF16),
                      preferred_element_type=F32)
    rank_all = cnt_ref[:, 0:1] + earlier
    rank_ref[...] = jnp.concatenate(
        [jnp.sum(jnp.where(h, rank_all, 0.0), axis=0, keepdims=True) for h in hits], axis=0).astype(I32)
    cnt_ref[...] = cnt_ref[...] + jnp.sum(picked, axis=1, keepdims=True)


def _router(x_bf, w_r_t, b_r, n_tok, cfg):
    m, dm = x_bf.shape
    tm = _pick(m, (256, 128))
    e = cfg.n_experts
    pick = pl.BlockSpec((cfg.top_k, tm), lambda i: (0, i))
    return pl.pallas_call(
        functools.partial(_router_body, cfg=cfg, n_tok=n_tok),
        grid=(m // tm,),
        in_specs=[pl.BlockSpec((tm, dm), lambda i: (i, 0)),
                  pl.BlockSpec((e, dm), lambda i: (0, 0)),
                  pl.BlockSpec((e, 1), lambda i: (0, 0))],
        out_specs=[pick, pick, pick, pl.BlockSpec((e, LANE), lambda i: (0, 0))],
        out_shape=[jax.ShapeDtypeStruct((cfg.top_k, m), I32),
                   jax.ShapeDtypeStruct((cfg.top_k, m), F32),
                   jax.ShapeDtypeStruct((cfg.top_k, m), I32),
                   jax.ShapeDtypeStruct((e, LANE), F32)],
        compiler_params=_params("arbitrary"),
        name="router",
    )(x_bf, w_r_t, b_r)


DMA_THREADS = 2


def _row_pitch(sub):
    return -(-sub // SUBLANE) * SUBLANE + SUBLANE


def _start_row_gather(idx_row, idx_smem, idx_sem, src_hbm, buf, sem, n, sub):
    cp = pltpu.make_async_copy(idx_row, idx_smem, idx_sem)
    cp.start()
    cp.wait()

    def start(g, carry):
        for u in range(DMA_THREADS):
            r = g * DMA_THREADS + u
            r0 = pl.multiple_of(r * _row_pitch(sub), SUBLANE)
            pltpu.make_async_copy(src_hbm.at[idx_smem[0, r]], buf.at[pl.ds(r0, sub)], sem).start(priority=u)
        return carry
    lax.fori_loop(0, n // DMA_THREADS, start, 0)


def _wait_row_gather(buf, sem, n, sub):
    pltpu.make_async_copy(buf.at[pl.ds(0, n * sub)], buf.at[pl.ds(0, n * sub)], sem).wait()


def _gathered_slab(buf, first, count, j, sub):
    return buf[pl.ds(first * _row_pitch(sub) + j, count, stride=_row_pitch(sub)), :]


def _dispatch_body(nrows_ref, src_ref, x_hbm, o_ref, idx_smem, buf_ref, idx_sem, sems, *, tile):
    i = pl.program_id(0)
    sub = x_hbm.shape[1]
    n_valid = (nrows_ref[0] + tile - 1) // tile

    def start(t):
        slot = t % 2
        _start_row_gather(src_ref.at[t], idx_smem.at[pl.ds(slot, 1)], idx_sem, x_hbm,
                          buf_ref.at[slot], sems.at[slot], tile, sub)

    @pl.when(i == 0)
    def _():
        start(0)

    @pl.when(i + 1 < n_valid)
    def _():
        start(i + 1)

    @pl.when(i < n_valid)
    def _():
        slot = i % 2
        _wait_row_gather(buf_ref.at[slot], sems.at[slot], tile, sub)
        half = sub * LANE
        for j in range(sub):
            lo, hi = _unpack_halves(_gathered_slab(buf_ref.at[slot], 0, tile, j, sub))
            o_ref[:, j * LANE:(j + 1) * LANE] = lo.astype(o_ref.dtype)
            o_ref[:, half + j * LANE:half + (j + 1) * LANE] = hi.astype(o_ref.dtype)


def _dispatch(x3, src_rows, n_rows, tile):
    r_max = src_rows.shape[0]
    sub = x3.shape[1]
    n_tiles = r_max // tile
    last = lambda i, nr: jnp.minimum(i, (nr[0] - 1) // tile)
    grid_spec = pltpu.PrefetchScalarGridSpec(
        num_scalar_prefetch=1,
        grid=(n_tiles,),
        in_specs=[pl.BlockSpec((n_tiles, 1, tile), lambda i, nr: (0, 0, 0)),
                  pl.BlockSpec(memory_space=pl.ANY)],
        out_specs=pl.BlockSpec((tile, 2 * sub * LANE), lambda i, nr: (last(i, nr), 0)),
        scratch_shapes=[pltpu.SMEM((2, tile), I32), pltpu.VMEM((2, tile * _row_pitch(sub), LANE), x3.dtype),
                        pltpu.SemaphoreType.DMA(()), pltpu.SemaphoreType.DMA((2,))],
    )
    return pl.pallas_call(
        functools.partial(_dispatch_body, tile=tile),
        grid_spec=grid_spec,
        out_shape=jax.ShapeDtypeStruct((r_max, 2 * sub * LANE), BF16),
        compiler_params=_params("arbitrary"),
        name="moe_dispatch",
    )(n_rows, src_rows.reshape(n_tiles, 1, tile), x3)


def _stream_expert_weights(te_ref, tf_ref, nx_ref, nt_ref, w_hbms, w_casts, wbuf, sems, cnt):
    p, m = pl.program_id(0), pl.program_id(1)
    width = w_casts[0].shape[1]

    def copies(expert, blk, slot):
        col = pl.multiple_of(blk * width, width)
        return [pltpu.make_async_copy(w.at[expert, :, pl.ds(col, width)], wbuf.at[slot, i], sems.at[slot])
                for i, w in enumerate(w_hbms)]

    @pl.when((p == 0) & (m == 0))
    def _():
        cnt[0] = 0
        for c in copies(te_ref[0], 0, 0):
            c.start()

    @pl.when((tf_ref[m] == 1) & (m < nt_ref[0]))
    def _():
        slot = cnt[0] % 2
        for c in copies(te_ref[m], p, slot):
            c.wait()
        nxt = nx_ref[m]

        @pl.when(nxt >= 0)
        def _():
            for c in copies(nxt, p, 1 - slot):
                c.start()

        @pl.when((nxt < 0) & (p + 1 < pl.num_programs(0)))
        def _():
            for c in copies(te_ref[0], p + 1, 1 - slot):
                c.start()

        for i, dst in enumerate(w_casts):
            dst[...] = wbuf[slot, i].astype(BF16)
        cnt[0] = cnt[0] + 1


def _expert_up_body(te_ref, tf_ref, nx_ref, nt_ref, x_ref, wg_hbm, wu_hbm, o_ref, wg_s, wu_s, wbuf, sems, cnt):
    _stream_expert_weights(te_ref, tf_ref, nx_ref, nt_ref, (wg_hbm, wu_hbm), (wg_s, wu_s), wbuf, sems, cnt)

    @pl.when(pl.program_id(1) < nt_ref[0])
    def _():
        x = x_ref[...]
        g = jnp.dot(x, wg_s[...], preferred_element_type=F32)
        u = jnp.dot(x, wu_s[...], preferred_element_type=F32)
        o_ref[...] = (jax.nn.silu(g) * u).astype(o_ref.dtype)


def _expert_up(xs, w_gate, w_up, tile_expert, tile_first, tile_next, n_tiles, tile):
    r_max, dm = xs.shape
    e, _, f = w_gate.shape
    tf = _pick(f, (512, 256, 128))
    n_mt = r_max // tile
    clamp = lambda m, nt: jnp.minimum(m, nt[0] - 1)
    grid_spec = pltpu.PrefetchScalarGridSpec(
        num_scalar_prefetch=4,
        grid=(f // tf, n_mt),
        in_specs=[pl.BlockSpec((tile, dm), lambda fi, m, te, tfst, nx, nt: (clamp(m, nt), 0)),
                  pl.BlockSpec(memory_space=pl.ANY),
                  pl.BlockSpec(memory_space=pl.ANY)],
        out_specs=pl.BlockSpec((tile, tf), lambda fi, m, te, tfst, nx, nt: (clamp(m, nt), fi)),
        scratch_shapes=[pltpu.VMEM((dm, tf), BF16), pltpu.VMEM((dm, tf), BF16),
                        pltpu.VMEM((2, 2, dm, tf), F32), pltpu.SemaphoreType.DMA((2,)),
                        pltpu.SMEM((1,), I32)],
    )
    return pl.pallas_call(
        _expert_up_body,
        grid_spec=grid_spec,
        out_shape=jax.ShapeDtypeStruct((r_max, f), BF16),
        compiler_params=_params("arbitrary", "arbitrary"),
        name="expert_up",
    )(tile_expert, tile_first, tile_next, n_tiles, xs, w_gate, w_up)


def _expert_down_body(te_ref, tf_ref, nx_ref, nt_ref, h_ref, wd_hbm, o_ref, wd_s, wbuf, sems, cnt):
    _stream_expert_weights(te_ref, tf_ref, nx_ref, nt_ref, (wd_hbm,), (wd_s,), wbuf, sems, cnt)

    @pl.when(pl.program_id(1) < nt_ref[0])
    def _():
        o_ref[...] = jnp.dot(h_ref[...], wd_s[...], preferred_element_type=F32)


def _expert_down(hid, w_down, tile_expert, tile_first, tile_next, n_tiles, tile):
    r_max, f = hid.shape
    dm = w_down.shape[2]
    tn = _pick(dm, (2048, 1024, 512, 256, 128))
    n_mt = r_max // tile
    clamp = lambda m, nt: jnp.minimum(m, nt[0] - 1)
    grid_spec = pltpu.PrefetchScalarGridSpec(
        num_scalar_prefetch=4,
        grid=(dm // tn, n_mt),
        in_specs=[pl.BlockSpec((tile, f), lambda ni, m, te, tfst, nx, nt: (clamp(m, nt), 0)),
                  pl.BlockSpec(memory_space=pl.ANY)],
        out_specs=pl.BlockSpec((tile, tn), lambda ni, m, te, tfst, nx, nt: (clamp(m, nt), ni)),
        scratch_shapes=[pltpu.VMEM((f, tn), BF16), pltpu.VMEM((2, 1, f, tn), F32),
                        pltpu.SemaphoreType.DMA((2,)), pltpu.SMEM((1,), I32)],
    )
    return pl.pallas_call(
        _expert_down_body,
        grid_spec=grid_spec,
        out_shape=jax.ShapeDtypeStruct((r_max, dm), F32),
        compiler_params=_params("arbitrary", "arbitrary"),
        name="expert_down",
    )(tile_expert, tile_first, tile_next, n_tiles, hid, w_down)


def _shared_up_body(x_ref, wg_ref, wu_ref, o_ref):
    x = x_ref[...]
    g = jnp.dot(x, wg_ref[...], preferred_element_type=F32)
    u = jnp.dot(x, wu_ref[...], preferred_element_type=F32)
    o_ref[...] = (jax.nn.silu(g) * u).astype(o_ref.dtype)


def _shared_up(x_bf, wg, wu):
    m, dm = x_bf.shape
    f = wg.shape[1]
    tm = _pick(m, (1408, 1024, 768, 640, 512, 256))
    tf = _pick(f, (256, 128))
    return pl.pallas_call(
        _shared_up_body,
        grid=(m // tm, f // tf),
        in_specs=[pl.BlockSpec((tm, dm), lambda i, j: (i, 0)),
                  pl.BlockSpec((dm, tf), lambda i, j: (0, j)),
                  pl.BlockSpec((dm, tf), lambda i, j: (0, j))],
        out_specs=pl.BlockSpec((tm, tf), lambda i, j: (i, j)),
        out_shape=jax.ShapeDtypeStruct((m, f), BF16),
        compiler_params=_params("arbitrary", "arbitrary"),
        name="shared_up",
    )(x_bf, wg, wu)


def _combine_body(pos_ref, gate_ref, ys_hbm, x_ref, sh_ref, wsd_ref, g_ref, b_ref, o_ref,
                  idx_smem, buf_ref, idx_sem, sems, *, tb, top_k, alpha, eps):
    i = pl.program_id(0)
    n = tb * top_k
    dm = x_ref.shape[1]

    def start(t):
        slot = t % 2
        cp = pltpu.make_async_copy(pos_ref.at[t], idx_smem.at[pl.ds(slot, 1)], idx_sem)
        cp.start()
        cp.wait()

        def issue(g, carry):
            for u in range(DMA_THREADS):
                r = g * DMA_THREADS + u
                pltpu.make_async_copy(ys_hbm.at[pl.ds(idx_smem[slot, r], 1), :],
                                      buf_ref.at[slot, pl.ds(r, 1), :], sems.at[slot]).start(priority=u)
            return carry
        lax.fori_loop(0, n // DMA_THREADS, issue, 0)

    @pl.when(i == 0)
    def _():
        start(0)

    @pl.when(i + 1 < pl.num_programs(0))
    def _():
        start(i + 1)

    shared = jnp.dot(sh_ref[...], wsd_ref[...], preferred_element_type=F32)
    slot = i % 2
    pltpu.make_async_copy(buf_ref.at[slot], buf_ref.at[slot], sems.at[slot]).wait()
    routed = shared
    for k in range(top_k):
        gate = jnp.concatenate([gate_ref[k * tb:(k + 1) * tb, :]] * (dm // LANE), axis=1)
        routed = routed + gate * buf_ref[slot, k * tb:(k + 1) * tb, :]
    o_ref[...] = _layer_norm(F32(alpha) * x_ref[...] + routed, g_ref[...], b_ref[...], eps)


def _combine(pos, gate_rows, ys, x, sh_hid, ws_down, g, b, alpha, tb, cfg):
    m, dm = x.shape
    f = sh_hid.shape[1]
    n = tb * cfg.top_k
    nblk = m // tb
    return pl.pallas_call(
        functools.partial(_combine_body, tb=tb, top_k=cfg.top_k, alpha=alpha, eps=cfg.ln_eps),
        grid=(nblk,),
        in_specs=[pl.BlockSpec((nblk, 1, n), lambda i: (0, 0, 0)),
                  pl.BlockSpec((n, LANE), lambda i: (i, 0)),
                  pl.BlockSpec(memory_space=pl.ANY),
                  pl.BlockSpec((tb, dm), lambda i: (i, 0)),
                  pl.BlockSpec((tb, f), lambda i: (i, 0)),
                  pl.BlockSpec((f, dm), lambda i: (0, 0)),
                  pl.BlockSpec((1, dm), lambda i: (0, 0)),
                  pl.BlockSpec((1, dm), lambda i: (0, 0))],
        out_specs=pl.BlockSpec((tb, dm), lambda i: (i, 0)),
        out_shape=jax.ShapeDtypeStruct((m, dm), F32),
        scratch_shapes=[pltpu.SMEM((2, n), I32), pltpu.VMEM((2, n, dm), F32),
                        pltpu.SemaphoreType.DMA(()), pltpu.SemaphoreType.DMA((2,))],
        compiler_params=_params("arbitrary"),
        name="moe_combine",
    )(pos, gate_rows, ys, x, sh_hid, ws_down, g, b)


def _routing_tables(ids, gates, ranks, counts, n_tok, tile, tb, cfg):
    e, k = cfg.n_experts, cfg.top_k
    t_all = ids.shape[1]
    r_max = ((n_tok * k + e * (tile - 1)) // tile + 1) * tile
    counts = counts[:, 0].astype(I32)
    padded = ((counts + tile - 1) // tile) * tile
    ends = jnp.cumsum(padded)
    start_pad = ends - padded
    eye = ids[:, :, None] == jnp.arange(e, dtype=I32)[None, None, :]
    dest = jnp.sum(jnp.where(eye, start_pad[None, None, :], 0), axis=2) + ranks
    tok = jnp.broadcast_to(jnp.arange(t_all, dtype=I32)[None, :], (k, t_all))
    real = tok < n_tok
    src_rows = jnp.zeros((r_max,), I32).at[jnp.where(real, dest, r_max).reshape(-1)].set(
        tok.reshape(-1), mode="drop", unique_indices=True)
    blocks = lambda a: a.reshape(k, t_all // tb, tb).transpose(1, 0, 2).reshape(t_all // tb, 1, k * tb)
    pos = blocks(jnp.where(real, dest, 0))
    gate_rows = jnp.broadcast_to(blocks(jnp.where(real, gates, 0.0)).reshape(-1, 1), (t_all * k, LANE))
    n_rows = ends[-1:]
    tile_start = jnp.arange(r_max // tile, dtype=I32) * tile
    tile_expert = jnp.minimum(jnp.sum((ends[None, :] <= tile_start[:, None]).astype(I32), axis=1), e - 1)
    prev = jnp.concatenate([jnp.full((1,), -1, I32), tile_expert[:-1]])
    tile_first = (tile_expert != prev).astype(I32)
    ar = jnp.arange(e, dtype=I32)
    later = jnp.where((padded > 0)[None, :] & (ar[None, :] > ar[:, None]), ar[None, :], e)
    next_expert = jnp.min(later, axis=1)
    next_expert = jnp.where(next_expert == e, -1, next_expert)
    tile_next = jnp.sum(jnp.where(tile_expert[:, None] == ar[None, :], next_expert[None, :], 0), axis=1)
    return src_rows, pos, gate_rows, n_rows, n_rows // tile, tile_expert, tile_first, tile_next


def _layer(cfg, l, x_all, n_p, seq, n_s, pos_all, cache_kv, cache_idx_k, page_table, state_conv, state_h,
           w_in, ik_g, ik_b, conv_w, conv_b, lru_w_a, lru_b_a, lru_w_x, lru_b_x, lru_lambda,
           w_o_attn, w_o_lru, w_out, ln1_g, ln1_b, w_router, b_router,
           w_gate, w_up, w_down, ws_gate, ws_up, ws_down, ln2_g, ln2_b):
    t_all, dm = x_all.shape
    t_p = n_p * seq
    d, di = cfg.head_dim, cfg.idx_dim
    qw, kvw, iw = cfg.n_heads * d, cfg.n_kv_heads * d, cfg.idx_heads * di
    lw = conv_w.shape[1]
    alpha = (2.0 * cfg.depth) ** 0.25
    c_qkv = qw + 2 * kvw
    c_i = c_qkv + iw + di
    c_w = c_i + cfg.idx_heads

    x_bf = x_all.astype(BF16)
    w_in_bf = w_in.astype(BF16)
    z_qkv = _matmul(x_bf, w_in_bf, 0, c_qkv, name="in_proj_qkv")
    z_i = _matmul(x_bf, w_in_bf, c_qkv, iw, name="in_proj_idx_q")
    z_kw = _matmul(x_bf, w_in_bf, c_qkv + iw, 2 * LANE, name="in_proj_idx_kw")
    z_b = _matmul(x_bf, w_in_bf[:, c_w:], name="in_proj_lru_gates")

    cosf, sinf = _rope_tables(pos_all, d, cfg.rope_theta)
    kv_all, ki_all, k_bf, ki_bf = _finalize_keys(
        z_qkv, z_kw, cosf, sinf, ik_g.reshape(1, di), ik_b.reshape(1, di), cfg)
    v_t = _transpose_values(z_qkv, t_p, cfg)

    o_attn_p = _prompt_attention(z_qkv, z_i, z_kw, cosf, sinf, k_bf, ki_bf, v_t, n_p, seq, cfg)
    wa_bf, wx_bf = lru_w_a.astype(BF16), lru_w_x.astype(BF16)
    row = lambda v: v.reshape(1, -1)
    h_lru_p, h_last_p = _lru_prompt(z_b, conv_w, row(conv_b), wa_bf, wx_bf, row(lru_b_a), row(lru_b_x),
                                    row(lru_lambda), n_p, seq, cfg)

    sl = slice(t_p, t_p + n_s)
    cos_s, sin_s = cosf[sl][:, None, :], sinf[sl][:, None, :]
    qi_s = z_i[sl].reshape(n_s, cfg.idx_heads, di)
    w_rep = jnp.broadcast_to(z_kw[sl, di:di + cfg.idx_heads][:, :, None], (n_s, cfg.idx_heads, LANE))
    n_phys = cache_idx_k.shape[0]
    scores = _sample_scores(page_table, qi_s, w_rep, ki_all[sl][:, None, :], cos_s, sin_s, cache_idx_k, cfg)
    q_s = z_qkv[sl, :qw].reshape(n_s, cfg.n_heads, d)
    kv_new = kv_all[sl].reshape(n_s, 2 * cfg.n_kv_heads, d)
    o_attn_s = _sample_attend(page_table, scores, q_s, kv_new, cos_s, sin_s,
                              cache_kv.reshape(n_phys, cfg.page_size * 2 * cfg.n_kv_heads, d), cfg)
    xl_s = z_b[sl, :lw]
    hist_s = jnp.moveaxis(state_conv, 1, 0)
    h_s, h_s_bf = _lru_sample(xl_s, hist_s, state_h, conv_w, row(conv_b), wa_bf, wx_bf,
                              row(lru_b_a), row(lru_b_x), row(lru_lambda), cfg)

    pad = t_all - t_p - n_s
    o_attn = jnp.concatenate([o_attn_p, o_attn_s.reshape(n_s, qw), jnp.zeros((pad, qw), BF16)], axis=0)
    h_lru = jnp.concatenate([h_lru_p, h_s_bf, jnp.zeros((pad, lw), BF16)], axis=0)

    merged = _merge(o_attn, h_lru, w_o_attn.astype(BF16), w_o_lru.astype(BF16), z_b, lw, cfg)
    x1, x1_bf, x1_pk = _proj_residual_ln(merged, w_out.astype(BF16), x_all, row(ln1_g), row(ln1_b), alpha, cfg)

    n_tok = t_p + n_s
    ids, gates, ranks, counts = _router(x1_bf, w_router.T.astype(BF16), b_router.reshape(-1, 1), n_tok, cfg)
    tile = cfg.moe_tile
    tb = _pick(t_all, (64, 32, 16, 8))
    src_rows, pos, gate_rows, n_rows, n_tiles, tile_expert, tile_first, tile_next = _routing_tables(
        ids, gates, ranks, counts, n_tok, tile, tb, cfg)
    xs = _dispatch(x1_pk.reshape(t_all, dm // (2 * LANE), LANE), src_rows, n_rows, tile)
    hid = _expert_up(xs, w_gate, w_up, tile_expert, tile_first, tile_next, n_tiles, tile)
    ys = _expert_down(hid, w_down, tile_expert, tile_first, tile_next, n_tiles, tile)
    sh_hid = _shared_up(x1_bf, ws_gate.astype(BF16), ws_up.astype(BF16))
    y_all = _combine(pos, gate_rows, ys, x1, sh_hid, ws_down.astype(BF16), row(ln2_g), row(ln2_b),
                     alpha, tb, cfg)

    conv_p = z_b[:t_p, :lw].reshape(n_p, seq, lw)[:, seq - (cfg.conv_width - 1):, :]
    conv_s = jnp.concatenate([state_conv[:, 1:, :], xl_s[:, None, :]], axis=1)
    outs = dict(
        kv_p=kv_all[:t_p].reshape(n_p, seq, 2, cfg.n_kv_heads, d),
        ik_p=ki_all[:t_p].reshape(n_p, seq, di),
        cv_p=conv_p, h_p=h_last_p.reshape(n_p, lw),
        kv_s=kv_all[sl].reshape(n_s, 1, 2, cfg.n_kv_heads, d),
        ik_s=ki_all[sl].reshape(n_s, 1, di),
        cv_s=conv_s, h_s=h_s)
    return y_all, outs


def _forward(cfg, x_prompt, x_sample, cache_kv, cache_idx_k, page_table, state_conv, state_h,
             w_in, idx_k_norm_g, idx_k_norm_b, conv_w, conv_b, lru_w_a, lru_b_a, lru_w_x, lru_b_x,
             lru_lambda, w_o_attn, w_o_lru, w_out, ln1_g, ln1_b, w_router, b_router,
             w_gate, w_up, w_down, ws_gate, ws_up, ws_down, ln2_g, ln2_b):
    n_p, seq, dm = x_prompt.shape
    n_s, t_s, _ = x_sample.shape
    assert t_s == 1 and cfg.depth == 1 and w_in.shape[0] == 1
    past = page_table.shape[1] * cfg.page_size
    t_p = n_p * seq
    t_all = -(-(t_p + n_s) // cfg.row_align) * cfg.row_align
    pad = t_all - t_p - n_s
    x_all = jnp.concatenate([x_prompt.reshape(t_p, dm), x_sample.reshape(n_s, dm),
                             jnp.zeros((pad, dm), x_prompt.dtype)], axis=0)
    pos_all = jnp.concatenate([jnp.tile(jnp.arange(seq), n_p), jnp.full((n_s,), past), jnp.zeros((pad,), I32)])
    l = 0
    y_all, o = _layer(cfg, l, x_all, n_p, seq, n_s, pos_all, cache_kv[l], cache_idx_k[l], page_table,
                      state_conv[l], state_h[l], w_in[l], idx_k_norm_g[l], idx_k_norm_b[l],
                      conv_w[l], conv_b[l], lru_w_a[l], lru_b_a[l], lru_w_x[l], lru_b_x[l], lru_lambda[l],
                      w_o_attn[l], w_o_lru[l], w_out[l], ln1_g[l], ln1_b[l], w_router[l], b_router[l],
                      w_gate[l], w_up[l], w_down[l], ws_gate[l], ws_up[l], ws_down[l], ln2_g[l], ln2_b[l])
    y_p = y_all[:t_p].reshape(n_p, seq, dm)
    y_s = y_all[t_p:t_p + n_s].reshape(n_s, 1, dm)
    lead = lambda a: a[None]
    return (y_p, y_s, lead(o["kv_p"]), lead(o["ik_p"]), lead(o["cv_p"]), lead(o["h_p"]),
            lead(o["kv_s"]), lead(o["ik_s"]), lead(o["cv_s"]), lead(o["h_s"]))


def kernel(x_prompt, x_sample, cache_kv, cache_idx_k, page_table, state_conv, state_h, w_in, idx_k_norm_g, idx_k_norm_b, conv_w, conv_b, lru_w_a, lru_b_a, lru_w_x, lru_b_x, lru_lambda, w_o_attn, w_o_lru, w_out, ln1_g, ln1_b, w_router, b_router, w_gate, w_up, w_down, ws_gate, ws_up, ws_down, ln2_g, ln2_b):
    return _forward(Cfg(), x_prompt, x_sample, cache_kv, cache_idx_k, page_table, state_conv, state_h,
                    w_in, idx_k_norm_g, idx_k_norm_b, conv_w, conv_b, lru_w_a, lru_b_a, lru_w_x, lru_b_x,
                    lru_lambda, w_o_attn, w_o_lru, w_out, ln1_g, ln1_b, w_router, b_router,
                    w_gate, w_up, w_down, ws_gate, ws_up, ws_down, ln2_g, ln2_b)
```

```python
import functools
from typing import NamedTuple

import jax
import jax.numpy as jnp
import numpy as np
from jax import lax
from jax.experimental import pallas as pl
from jax.experimental.pallas import tpu as pltpu

F32 = jnp.float32
BF16 = jnp.bfloat16
I32 = jnp.int32
U32 = jnp.uint32

LANE = 128
SUBLANE = 8
VMEM_LIMIT = 56 * 1024 * 1024
MASKED = -1e30
INT_MIN = -2 ** 31


class Cfg(NamedTuple):
    n_heads: int = 16
    n_kv_heads: int = 4
    head_dim: int = 128
    idx_heads: int = 32
    idx_dim: int = 128
    topk_max: int = 256
    q_block: int = 128
    rope_theta: float = 10000.0
    lru_blocks: int = 16
    conv_width: int = 4
    lru_c: float = 8.0
    n_experts: int = 64
    top_k: int = 8
    n_groups: int = 8
    topk_groups: int = 4
    route_scale: float = 2.5
    ln_eps: float = 1e-5
    page_size: int = 128
    depth: int = 1
    key_chunk: int = 512
    moe_tile: int = 512
    row_align: int = 256


def _pick(dim, prefs):
    for p in prefs:
        if p <= dim and dim % p == 0:
            return p
    return dim


def _params(*sem):
    return pltpu.CompilerParams(dimension_semantics=sem, vmem_limit_bytes=VMEM_LIMIT)


def _mm_body(x_ref, w_ref, o_ref):
    o_ref[...] = jnp.dot(x_ref[...], w_ref[...], preferred_element_type=F32).astype(o_ref.dtype)


def _matmul(x, w, col0=0, n=None, out_dtype=F32, name="matmul"):
    m, k = x.shape
    n = w.shape[1] - col0 if n is None else n
    tm = _pick(m, (1408, 1024, 768, 640, 512, 256))
    tn = _pick(n, (512, 384, 256, 128))
    assert col0 % tn == 0
    j0 = col0 // tn
    return pl.pallas_call(
        _mm_body,
        grid=(m // tm, n // tn),
        in_specs=[pl.BlockSpec((tm, k), lambda i, j: (i, 0)),
                  pl.BlockSpec((k, tn), lambda i, j: (0, j0 + j))],
        out_specs=pl.BlockSpec((tm, tn), lambda i, j: (i, j)),
        out_shape=jax.ShapeDtypeStruct((m, n), out_dtype),
        compiler_params=_params("arbitrary", "arbitrary"),
        name=name,
    )(x, w)


def _rope(x, cosf, sinf):
    return x * cosf + pltpu.roll(x, x.shape[-1] // 2, axis=x.ndim - 1) * sinf


def _rope_tables(pos, dim, theta):
    half = dim // 2
    inv = theta ** (-jnp.arange(half, dtype=F32) / half)
    ang = pos.astype(F32)[:, None] * inv[None, :]
    cos, sin = jnp.cos(ang), jnp.sin(ang)
    return jnp.concatenate([cos, cos], -1), jnp.concatenate([-sin, sin], -1)


def _pack_halves(y):
    w = y.shape[1] // 2
    lo = pltpu.bitcast(y[:, :w].astype(BF16).astype(F32), U32)
    hi = pltpu.bitcast(y[:, w:].astype(BF16).astype(F32), U32)
    return (hi & jnp.uint32(0xFFFF0000)) | (lo >> 16)


def _unpack_halves(p):
    lo = pltpu.bitcast(p << 16, F32)
    hi = pltpu.bitcast(p & jnp.uint32(0xFFFF0000), F32)
    return lo, hi


def _layer_norm(y, g, b, eps):
    mu = jnp.mean(y, axis=-1, keepdims=True)
    yc = y - mu
    var = jnp.mean(yc * yc, axis=-1, keepdims=True)
    return yc * lax.rsqrt(var + eps) * g + b


def _kv_body(kv_ref, ki_ref, cos_ref, sin_ref, g_ref, b_ref,
             kvo_ref, kio_ref, kbf_ref, kibf_ref, *, cfg):
    c_heads, d = cfg.n_kv_heads, cfg.head_dim
    cosf, sinf = cos_ref[...], sin_ref[...]
    kv = kv_ref[...]
    for c in range(c_heads):
        kc = _rope(kv[:, c * d:(c + 1) * d], cosf, sinf)
        kvo_ref[:, c * d:(c + 1) * d] = kc
        kbf_ref[:, c * d:(c + 1) * d] = kc.astype(BF16)
    kvo_ref[:, c_heads * d:] = kv[:, c_heads * d:]
    ki = _rope(_layer_norm(ki_ref[...], g_ref[...], b_ref[...], cfg.ln_eps), cosf, sinf)
    kio_ref[...] = ki
    kibf_ref[...] = ki.astype(BF16)


def _vt_body(v_ref, vt_ref, *, cfg):
    d = cfg.head_dim
    for c in range(cfg.n_kv_heads):
        vt_ref[0, c * d:(c + 1) * d, :] = v_ref[:, c * d:(c + 1) * d].T.astype(BF16)


def _transpose_values(z_qkv, t_p, cfg):
    kc = cfg.key_chunk
    kvw = cfg.n_kv_heads * cfg.head_dim
    v_blk = (cfg.n_heads * cfg.head_dim + kvw) // kvw
    assert t_p % kc == 0
    return pl.pallas_call(
        functools.partial(_vt_body, cfg=cfg),
        grid=(t_p // kc,),
        in_specs=[pl.BlockSpec((kc, kvw), lambda i: (i, v_blk))],
        out_specs=pl.BlockSpec((1, kvw, kc), lambda i: (i, 0, 0)),
        out_shape=jax.ShapeDtypeStruct((t_p // kc, kvw, kc), BF16),
        compiler_params=_params("arbitrary"),
        name="transpose_values",
    )(z_qkv)


def _finalize_keys(z_qkv, z_i, cosf, sinf, ik_g, ik_b, cfg):
    t_all = z_qkv.shape[0]
    c_heads, d = cfg.n_kv_heads, cfg.head_dim
    kvw = c_heads * d
    qw = cfg.n_heads * d
    tr = cfg.row_align
    assert t_all % tr == 0 and qw % (2 * kvw) == 0
    return pl.pallas_call(
        functools.partial(_kv_body, cfg=cfg),
        grid=(t_all // tr,),
        in_specs=[pl.BlockSpec((tr, 2 * kvw), lambda i: (i, qw // (2 * kvw))),
                  pl.BlockSpec((tr, cfg.idx_dim), lambda i: (i, 0)),
                  pl.BlockSpec((tr, d), lambda i: (i, 0)),
                  pl.BlockSpec((tr, d), lambda i: (i, 0)),
                  pl.BlockSpec((1, cfg.idx_dim), lambda i: (0, 0)),
                  pl.BlockSpec((1, cfg.idx_dim), lambda i: (0, 0))],
        out_specs=[pl.BlockSpec((tr, 2 * kvw), lambda i: (i, 0)),
                   pl.BlockSpec((tr, cfg.idx_dim), lambda i: (i, 0)),
                   pl.BlockSpec((tr, kvw), lambda i: (i, 0)),
                   pl.BlockSpec((tr, cfg.idx_dim), lambda i: (i, 0))],
        out_shape=[jax.ShapeDtypeStruct((t_all, 2 * kvw), F32),
                   jax.ShapeDtypeStruct((t_all, cfg.idx_dim), F32),
                   jax.ShapeDtypeStruct((t_all, kvw), BF16),
                   jax.ShapeDtypeStruct((t_all, cfg.idx_dim), BF16)],
        compiler_params=_params("arbitrary"),
        name="finalize_keys",
    )(z_qkv, z_i, cosf, sinf, ik_g, ik_b)


def _sort_key(s):
    b = pltpu.bitcast(s, I32)
    return b ^ ((b >> 31) & jnp.int32(0x7FFFFFFF))


def _kth_largest_key(count_ge, k, width):
    def bit_step(i, ans):
        cand = ans | lax.shift_left(jnp.int32(1), jnp.int32(31) - i)
        cnt = count_ge(cand ^ jnp.int32(INT_MIN))
        return jnp.where(cnt >= k, cand, ans)

    ans = lax.fori_loop(0, 32, bit_step, jnp.zeros((1, width), I32))
    return ans ^ jnp.int32(INT_MIN)


def _attn_body(q_ref, qi_ref, wi_ref, cos_ref, sin_ref, kbf_ref, kibf_ref, vt_ref,
               o_ref, qs_ref, qis_ref, key_ref, m_ref, l_ref, acc_ref, *, cfg, topk):
    qb_idx = pl.program_id(1)
    qb, kc = cfg.q_block, cfg.key_chunk
    d, di = cfg.head_dim, cfg.idx_dim
    c_heads = cfg.n_kv_heads
    g_heads = cfg.n_heads // c_heads
    cosf, sinf = cos_ref[...], sin_ref[...]

    for h in range(cfg.n_heads):
        qs_ref[h * qb:(h + 1) * qb, :] = _rope(q_ref[:, h * d:(h + 1) * d], cosf, sinf).astype(BF16)
    for h in range(cfg.idx_heads):
        qis_ref[h * qb:(h + 1) * qb, :] = _rope(qi_ref[:, h * di:(h + 1) * di], cosf, sinf).astype(BF16)
    w_t = wi_ref[...].T * F32((di * cfg.idx_heads) ** -0.5)

    q0 = qb_idx * qb
    n_chunks = (q0 + qb + kc - 1) // kc
    tpos = q0 + lax.broadcasted_iota(I32, (kc, qb), 1)
    krow = lax.broadcasted_iota(I32, (kc, qb), 0)
    nt = (((1,), (1,)), ((), ()))

    def score_chunk(ci, carry):
        k0 = pl.multiple_of(ci * kc, kc)
        ki_c = kibf_ref[pl.ds(k0, kc), :]
        acc = jnp.zeros((kc, qb), F32)
        for hp in range(cfg.idx_heads // 2):
            z = lax.dot_general(ki_c, qis_ref[hp * 2 * qb:(hp + 1) * 2 * qb, :], nt,
                                preferred_element_type=F32)
            acc = acc + jnp.maximum(z[:, :qb], 0.0) * w_t[2 * hp:2 * hp + 1, :]
            acc = acc + jnp.maximum(z[:, qb:], 0.0) * w_t[2 * hp + 1:2 * hp + 2, :]
        causal = (k0 + krow) <= tpos
        key_ref[pl.ds(k0, kc), :] = _sort_key(jnp.where(causal, acc, -jnp.inf))
        return carry

    lax.fori_loop(0, n_chunks, score_chunk, 0)

    def count_ge(cand):
        def cnt_chunk(ci, acc8):
            k0 = pl.multiple_of(ci * kc, kc)
            hit = jnp.where(key_ref[pl.ds(k0, kc), :] >= cand, 1, 0).astype(I32)
            return acc8 + jnp.sum(hit.reshape(kc // SUBLANE, SUBLANE, qb), axis=0)
        acc8 = lax.fori_loop(0, n_chunks, cnt_chunk, jnp.zeros((SUBLANE, qb), I32))
        return jnp.sum(acc8, axis=0, keepdims=True)

    thr = _kth_largest_key(count_ge, topk, qb)

    surplus = jnp.where(thr > _sort_key(jnp.full((1, qb), -jnp.inf, F32)), count_ge(thr) - topk, 0)

    @pl.when(jnp.max(surplus) > 0)
    def _():
        wanted = (topk - count_ge(thr + 1)).astype(F32)
        earlier = jnp.where(lax.broadcasted_iota(I32, (kc, kc), 1) < lax.broadcasted_iota(I32, (kc, kc), 0),
                            1.0, 0.0).astype(BF16)

        def strike_chunk(ci, seen):
            k0 = pl.multiple_of(ci * kc, kc)
            blk = key_ref[pl.ds(k0, kc), :]
            tied = jnp.where(blk == thr, 1.0, 0.0)
            rank = seen + jnp.dot(earlier, tied.astype(BF16), preferred_element_type=F32)
            key_ref[pl.ds(k0, kc), :] = jnp.where((blk == thr) & (rank >= wanted), jnp.int32(INT_MIN), blk)
            return seen + jnp.sum(tied, axis=0, keepdims=True)

        lax.fori_loop(0, n_chunks, strike_chunk, jnp.zeros((1, qb), F32))

    m_ref[...] = jnp.full(m_ref.shape, MASKED, F32)
    l_ref[...] = jnp.zeros(l_ref.shape, F32)
    acc_ref[...] = jnp.zeros(acc_ref.shape, F32)
    scale = F32(d ** -0.5 * np.log2(np.e))

    def attend_chunk(ci, carry):
        k0 = pl.multiple_of(ci * kc, kc)
        keep = (key_ref[pl.ds(k0, kc), :] >= thr) & ((k0 + krow) <= tpos)
        bias = jnp.where(keep, 0.0, MASKED).astype(F32)
        bias = jnp.concatenate([bias] * g_heads, axis=1)
        for c in range(c_heads):
            k_c = kbf_ref[pl.ds(k0, kc), c * d:(c + 1) * d]
            s = lax.dot_general(k_c, qs_ref[c * g_heads * qb:(c + 1) * g_heads * qb, :], nt,
                                preferred_element_type=F32) * scale + bias
            m_old = m_ref[c:c + 1, :]
            m_new = jnp.maximum(m_old, jnp.max(s, axis=0, keepdims=True))
            alpha = jnp.exp2(m_old - m_new)
            p = jnp.exp2(s - m_new)
            l_ref[c:c + 1, :] = alpha * l_ref[c:c + 1, :] + jnp.sum(p, axis=0, keepdims=True)
            pv = jnp.dot(vt_ref[ci, c * d:(c + 1) * d, :], p.astype(BF16), preferred_element_type=F32)
            acc_ref[c * d:(c + 1) * d, :] = alpha * acc_ref[c * d:(c + 1) * d, :] + pv
            m_ref[c:c + 1, :] = m_new
        return carry

    lax.fori_loop(0, n_chunks, attend_chunk, 0)

    for c in range(c_heads):
        o_t = acc_ref[c * d:(c + 1) * d, :] / l_ref[c:c + 1, :]
        for g in range(g_heads):
            h = c * g_heads + g
            o_ref[:, h * d:(h + 1) * d] = o_t[:, g * qb:(g + 1) * qb].T.astype(o_ref.dtype)


def _prompt_attention(z_qkv, z_i, z_w, cosf, sinf, k_bf, ki_bf, v_t, n_seq, seq, cfg):
    qb, kc = cfg.q_block, cfg.key_chunk
    d = cfg.head_dim
    qw = cfg.n_heads * d
    kvw = cfg.n_kv_heads * d
    iw = cfg.idx_heads * cfg.idx_dim
    assert seq % kc == 0 and seq % qb == 0 and kc % qb == 0
    assert cfg.head_dim == LANE and cfg.idx_dim == LANE
    nqb = seq // qb
    topk = min(cfg.topk_max, seq // 4)
    row = lambda n, j: (n * nqb + j, 0)
    return pl.pallas_call(
        functools.partial(_attn_body, cfg=cfg, topk=topk),
        grid=(n_seq, nqb),
        in_specs=[pl.BlockSpec((qb, qw), row),
                  pl.BlockSpec((qb, iw), row),
                  pl.BlockSpec((qb, LANE), lambda n, j: (n * nqb + j, 1)),
                  pl.BlockSpec((qb, d), row),
                  pl.BlockSpec((qb, d), row),
                  pl.BlockSpec((seq, kvw), lambda n, j: (n, 0)),
                  pl.BlockSpec((seq, cfg.idx_dim), lambda n, j: (n, 0)),
                  pl.BlockSpec((seq // kc, kvw, kc), lambda n, j: (n, 0, 0))],
        out_specs=pl.BlockSpec((qb, qw), row),
        out_shape=jax.ShapeDtypeStruct((n_seq * seq, qw), BF16),
        scratch_shapes=[pltpu.VMEM((cfg.n_heads * qb, d), BF16),
                        pltpu.VMEM((cfg.idx_heads * qb, cfg.idx_dim), BF16),
                        pltpu.VMEM((seq, qb), I32),
                        pltpu.VMEM((SUBLANE, (cfg.n_heads // cfg.n_kv_heads) * qb), F32),
                        pltpu.VMEM((SUBLANE, (cfg.n_heads // cfg.n_kv_heads) * qb), F32),
                        pltpu.VMEM((kvw, (cfg.n_heads // cfg.n_kv_heads) * qb), F32)],
        compiler_params=_params("arbitrary", "arbitrary"),
        name="prompt_attention",
    )(z_qkv, z_i, z_w, cosf, sinf, k_bf, ki_bf, v_t)


def _softplus(x):
    return jnp.maximum(x, 0.0) + jnp.log1p(jnp.exp(-jnp.abs(x)))


def _lru_gates(xc, wa_ref, wx_ref, ba, bx, lam, cfg):
    w = xc.shape[1]
    bd = w // cfg.lru_blocks
    xb = xc.astype(BF16)
    r_parts, i_parts = [], []
    for k in range(cfg.lru_blocks):
        xk = xb[:, k * bd:(k + 1) * bd]
        r_parts.append(jnp.dot(xk, wa_ref[k], preferred_element_type=F32))
        i_parts.append(jnp.dot(xk, wx_ref[k], preferred_element_type=F32))
    r = jax.nn.sigmoid(jnp.concatenate(r_parts, axis=1) + ba)
    gate_i = jax.nn.sigmoid(jnp.concatenate(i_parts, axis=1) + bx)
    log_a = (-cfg.lru_c * r) * _softplus(-lam)
    a = jnp.exp(log_a)
    u = jnp.sqrt(-jnp.tanh(log_a) * (a * a + 1.0)) * (gate_i * xc)
    return a, u


def _lru_prompt_body(xl_ref, cw_ref, cb_ref, wa_ref, wx_ref, ba_ref, bx_ref, lam_ref,
                     h_ref, hlast_ref, prev_ref, carry_ref, *, cfg):
    j = pl.program_id(1)
    tb = xl_ref.shape[0]
    cw = cfg.conv_width

    @pl.when(j == 0)
    def _():
        prev_ref[...] = jnp.zeros(prev_ref.shape, F32)
        carry_ref[...] = jnp.zeros(carry_ref.shape, F32)

    xl = xl_ref[...]
    ext = jnp.concatenate([prev_ref[...], xl], axis=0)
    off = SUBLANE - (cw - 1)
    xc = cb_ref[...] + ext[off:off + tb] * cw_ref[0:1, :]
    for t in range(1, cw):
        xc = xc + ext[off + t:off + t + tb] * cw_ref[t:t + 1, :]
    prev_ref[...] = xl[tb - SUBLANE:, :]

    a, u = _lru_gates(xc, wa_ref, wx_ref, ba_ref[...], bx_ref[...], lam_ref[...], cfg)

    row = lax.broadcasted_iota(I32, a.shape, 0) & (SUBLANE - 1)
    s = 1
    while s < SUBLANE:
        ok = row >= s
        a_sh = jnp.where(ok, pltpu.roll(a, s, axis=0), 1.0)
        u_sh = jnp.where(ok, pltpu.roll(u, s, axis=0), 0.0)
        u = u + a * u_sh
        a = a * a_sh
        s *= 2
    h_prev = carry_ref[...]
    for gi in range(tb // SUBLANE):
        sl = slice(gi * SUBLANE, (gi + 1) * SUBLANE)
        h_rows = u[sl] + a[sl] * h_prev
        h_ref[sl, :] = h_rows.astype(h_ref.dtype)
        h_prev = h_rows[SUBLANE - 1:SUBLANE, :]
    carry_ref[...] = h_prev
    hlast_ref[0] = h_prev


def _lru_prompt(z_b, conv_w, conv_b, wa, wx, ba, bx, lam, n_seq, seq, cfg):
    w = conv_w.shape[1]
    tb = _pick(seq, (256, 128))
    nb = seq // tb
    bd = w // cfg.lru_blocks
    vec = pl.BlockSpec((1, w), lambda n, j: (0, 0))
    return pl.pallas_call(
        functools.partial(_lru_prompt_body, cfg=cfg),
        grid=(n_seq, nb),
        in_specs=[pl.BlockSpec((tb, w), lambda n, j: (n * nb + j, 0)),
                  pl.BlockSpec((cfg.conv_width, w), lambda n, j: (0, 0)),
                  vec,
                  pl.BlockSpec((cfg.lru_blocks, bd, bd), lambda n, j: (0, 0, 0)),
                  pl.BlockSpec((cfg.lru_blocks, bd, bd), lambda n, j: (0, 0, 0)),
                  vec, vec, vec],
        out_specs=[pl.BlockSpec((tb, w), lambda n, j: (n * nb + j, 0)),
                   pl.BlockSpec((1, 1, w), lambda n, j: (n, 0, 0))],
        out_shape=[jax.ShapeDtypeStruct((n_seq * seq, w), BF16),
                   jax.ShapeDtypeStruct((n_seq, 1, w), F32)],
        scratch_shapes=[pltpu.VMEM((SUBLANE, w), F32), pltpu.VMEM((1, w), F32)],
        compiler_params=_params("arbitrary", "arbitrary"),
        name="lru_prompt",
    )(z_b, conv_w, conv_b, wa, wx, ba, bx, lam)


def _lru_sample_body(xl_ref, hist_ref, h0_ref, cw_ref, cb_ref, wa_ref, wx_ref, ba_ref, bx_ref, lam_ref,
                     h_ref, hbf_ref, *, cfg):
    cw = cfg.conv_width
    xc = cb_ref[...] + hist_ref[0] * cw_ref[0:1, :]
    for t in range(1, cw - 1):
        xc = xc + hist_ref[t] * cw_ref[t:t + 1, :]
    xc = xc + xl_ref[...] * cw_ref[cw - 1:cw, :]
    a, u = _lru_gates(xc, wa_ref, wx_ref, ba_ref[...], bx_ref[...], lam_ref[...], cfg)
    h = a * h0_ref[...] + u
    h_ref[...] = h
    hbf_ref[...] = h.astype(BF16)


def _lru_sample(xl, hist, h0, conv_w, conv_b, wa, wx, ba, bx, lam, cfg):
    n, w = xl.shape
    return pl.pallas_call(
        functools.partial(_lru_sample_body, cfg=cfg),
        out_shape=[jax.ShapeDtypeStruct((n, w), F32), jax.ShapeDtypeStruct((n, w), BF16)],
        compiler_params=pltpu.CompilerParams(vmem_limit_bytes=VMEM_LIMIT),
        name="lru_sample",
    )(xl, hist, h0, conv_w, conv_b, wa, wx, ba, bx, lam)


def _sample_scores_body(pt_ref, qi_ref, w_ref, kin_ref, cos_ref, sin_ref, *rest, cfg, pages_per_step):
    page_refs, o_ref = rest[:pages_per_step], rest[pages_per_step]
    j = pl.program_id(1)
    nj = pl.num_programs(1)
    cosf, sinf = cos_ref[0], sin_ref[0]
    qi = _rope(qi_ref[0], cosf, sinf)
    qi_b = qi.astype(BF16)
    w = w_ref[0] * F32((cfg.idx_dim * cfg.idx_heads) ** -0.5)
    nt = (((1,), (1,)), ((), ()))
    rows = []
    for p in range(pages_per_step):
        z = lax.dot_general(qi_b, page_refs[p][0].astype(BF16), nt, preferred_element_type=F32)
        rows.append(jnp.sum(jnp.maximum(z, 0.0) * w, axis=0, keepdims=True))
    r0 = pl.multiple_of(j * pages_per_step, pages_per_step)
    o_ref[0, pl.ds(r0, pages_per_step), :] = jnp.concatenate(rows, axis=0)

    @pl.when(j == nj - 1)
    def _():
        z_new = jnp.sum(qi * kin_ref[0], axis=1, keepdims=True)
        s_new = jnp.sum(jnp.maximum(z_new, 0.0) * w, axis=0, keepdims=True)
        lane = lax.broadcasted_iota(I32, (SUBLANE, LANE), 1)
        sub = lax.broadcasted_iota(I32, (SUBLANE, LANE), 0)
        tail = jnp.where((lane == 0) & (sub == 0), jnp.broadcast_to(s_new, (SUBLANE, LANE)), -jnp.inf)
        o_ref[0, pl.ds(nj * pages_per_step, SUBLANE), :] = tail


def _sample_scores(page_table, qi, w_rep, ki_new, cos_s, sin_s, cache_idx_k, cfg):
    n, n_pages = page_table.shape
    pps = _pick(n_pages, (SUBLANE,))
    assert cfg.page_size == LANE
    hi, di = cfg.idx_heads, cfg.idx_dim
    per_seq = lambda b, j, pt: (b, 0, 0)
    page_specs = [pl.BlockSpec((1, cfg.page_size, di),
                               functools.partial(lambda b, j, pt, p: (pt[b, j * pps + p], 0, 0), p=p))
                  for p in range(pps)]
    grid_spec = pltpu.PrefetchScalarGridSpec(
        num_scalar_prefetch=1,
        grid=(n, n_pages // pps),
        in_specs=[pl.BlockSpec((1, hi, di), per_seq),
                  pl.BlockSpec((1, hi, LANE), per_seq),
                  pl.BlockSpec((1, 1, di), per_seq),
                  pl.BlockSpec((1, 1, di), per_seq),
                  pl.BlockSpec((1, 1, di), per_seq)] + page_specs,
        out_specs=pl.BlockSpec((1, n_pages + SUBLANE, LANE), per_seq),
    )
    return pl.pallas_call(
        functools.partial(_sample_scores_body, cfg=cfg, pages_per_step=pps),
        grid_spec=grid_spec,
        out_shape=jax.ShapeDtypeStruct((n, n_pages + SUBLANE, LANE), F32),
        compiler_params=_params("arbitrary", "arbitrary"),
        name="sample_scores",
    )(page_table, qi, w_rep, ki_new, cos_s, sin_s, *([cache_idx_k] * pps))


def _row_major_rank(flag, upper, lower):
    fb = flag.astype(BF16)
    in_row = jnp.dot(fb, upper, preferred_element_type=F32)
    row_tot = jnp.sum(flag, axis=1, keepdims=True)
    before = jnp.dot(lower, jnp.broadcast_to(row_tot, flag.shape).astype(BF16), preferred_element_type=F32)
    return before + in_row


def _sample_attend_body(pt_ref, s_ref, q_ref, kvn_ref, cos_ref, sin_ref, cache_hbm, o_ref,
                        lst_v, lst_s, buf_ref, lst_sem, sem, *, cfg, topk):
    b = pl.program_id(0)
    nh, d = cfg.n_heads, cfg.head_dim
    c_heads = cfg.n_kv_heads
    g_heads = nh // c_heads
    per = 2 * c_heads
    ps = cfg.page_size
    n_rows = s_ref.shape[1]
    n_pages = n_rows - SUBLANE
    scale = F32(d ** -0.5)
    head_c = lax.broadcasted_iota(I32, (nh, 1), 0) // g_heads

    score = s_ref[0]
    keys = _sort_key(score)

    def count_ge(cand):
        hit = jnp.where(keys >= cand, 1, 0).astype(I32)
        return jnp.sum(jnp.sum(hit, axis=0, keepdims=True), axis=1, keepdims=True)

    thr = _kth_largest_key(count_ge, topk, 1)
    live = score > -jnp.inf
    above = jnp.where((keys > thr) & live, 1.0, 0.0)
    tied = jnp.where((keys == thr) & live, 1.0, 0.0)
    upper = jnp.where(lax.broadcasted_iota(I32, (LANE, LANE), 0) < lax.broadcasted_iota(I32, (LANE, LANE), 1),
                      1.0, 0.0).astype(BF16)
    lower = jnp.where(lax.broadcasted_iota(I32, (n_rows, n_rows), 1) < lax.broadcasted_iota(I32, (n_rows, n_rows), 0),
                      1.0, 0.0).astype(BF16)
    n_above = jnp.sum(jnp.sum(above, axis=0, keepdims=True), axis=1, keepdims=True)
    keep = above + tied * jnp.where(_row_major_rank(tied, upper, lower) < F32(topk) - n_above, 1.0, 0.0)
    row = lax.broadcasted_iota(I32, (n_rows, LANE), 0)
    keep_new = jnp.sum(jnp.sum(jnp.where(row == n_pages, keep, 0.0), axis=0, keepdims=True),
                       axis=1, keepdims=True)
    keep = jnp.where(row < n_pages, keep, 0.0)

    rank = _row_major_rank(keep, upper, lower)
    n_sel = jnp.sum(jnp.sum(keep, axis=0, keepdims=True), axis=1, keepdims=True)
    pad_rows = jnp.zeros((LANE - n_rows, LANE), F32)
    rank_t = jnp.concatenate([rank, pad_rows], axis=0).T
    keep_t = jnp.concatenate([keep, pad_rows], axis=0).T
    list_pos = lax.broadcasted_iota(I32, (ps, topk), 1).astype(F32)
    slot_and_one = jnp.where(lax.broadcasted_iota(I32, (SUBLANE, ps), 0) == 0,
                             lax.broadcasted_iota(I32, (SUBLANE, ps), 1).astype(F32), 1.0).astype(BF16)
    pages = jnp.zeros((1, topk), F32)
    slots = jnp.zeros((1, topk), F32)
    for p in range(n_pages):
        here = jnp.where((rank_t[:, p:p + 1] == list_pos) & (keep_t[:, p:p + 1] > 0.5), 1.0, 0.0).astype(BF16)
        hit = jnp.dot(slot_and_one, here, preferred_element_type=F32)
        slots = slots + hit[0:1, :]
        pages = pages + F32(p) * hit[1:2, :]
    lst_v[...] = jnp.concatenate([pages, slots, jnp.zeros((SUBLANE - 2, topk), F32)], axis=0).astype(I32)
    cp = pltpu.make_async_copy(lst_v, lst_s, lst_sem)
    cp.start()
    cp.wait()

    def fetch(g, carry):
        for u in range(DMA_THREADS):
            r = g * DMA_THREADS + u
            src = pl.multiple_of((pt_ref[b, lst_s[0, r]] * ps + lst_s[1, r]) * per, per)
            pltpu.make_async_copy(cache_hbm.at[pl.ds(src, per)],
                                  buf_ref.at[pl.ds(pl.multiple_of(r * per, per), per)], sem).start(priority=u)
        return carry
    lax.fori_loop(0, topk // DMA_THREADS, fetch, 0)

    q_f = _rope(q_ref[0], cos_ref[0], sin_ref[0])
    kvn = kvn_ref[0]
    k_new = jnp.zeros((nh, d), F32)
    v_new = jnp.zeros((nh, d), F32)
    for c in range(c_heads):
        k_new = jnp.where(head_c == c, kvn[c:c + 1, :], k_new)
        v_new = jnp.where(head_c == c, kvn[c_heads + c:c_heads + c + 1, :], v_new)
    s_new = jnp.where(keep_new > 0.5, jnp.sum(q_f * k_new, axis=1, keepdims=True) * scale, MASKED)

    pltpu.make_async_copy(buf_ref, buf_ref, sem).wait()

    width = topk * per
    rows = buf_ref[...].astype(BF16)
    col = lax.broadcasted_iota(I32, (nh, width), 1)
    mine = ((col % per) == head_c) & ((col // per).astype(F32) < n_sel)
    nt = (((1,), (1,)), ((), ()))
    s = jnp.where(mine, lax.dot_general(q_f.astype(BF16), rows, nt, preferred_element_type=F32) * scale, MASKED)
    m = jnp.maximum(jnp.max(s, axis=1, keepdims=True), s_new)
    p_un = jnp.exp(s - m)
    p_new = jnp.exp(s_new - m)
    den = jnp.sum(p_un, axis=1, keepdims=True) + p_new
    pv = jnp.dot(pltpu.roll(p_un, c_heads, axis=1).astype(BF16), rows, preferred_element_type=F32)
    o_ref[0] = ((pv + p_new * v_new) / den).astype(o_ref.dtype)


def _sample_attend(page_table, scores, q, kv_new, cos_s, sin_s, cache_kv, cfg):
    n, n_pages = page_table.shape
    nh, d = cfg.n_heads, cfg.head_dim
    per = 2 * cfg.n_kv_heads
    topk = min(cfg.topk_max, (n_pages * cfg.page_size + 1) // 4)
    n_rows = scores.shape[1]
    assert n_rows <= LANE and cfg.page_size == LANE and per == SUBLANE and topk % DMA_THREADS == 0
    per_seq = lambda b, pt: (b, 0, 0)
    grid_spec = pltpu.PrefetchScalarGridSpec(
        num_scalar_prefetch=1,
        grid=(n,),
        in_specs=[pl.BlockSpec((1, n_rows, LANE), per_seq),
                  pl.BlockSpec((1, nh, d), per_seq),
                  pl.BlockSpec((1, per, d), per_seq),
                  pl.BlockSpec((1, 1, d), per_seq),
                  pl.BlockSpec((1, 1, d), per_seq),
                  pl.BlockSpec(memory_space=pl.ANY)],
        out_specs=pl.BlockSpec((1, nh, d), per_seq),
        scratch_shapes=[pltpu.VMEM((SUBLANE, topk), I32),
                        pltpu.SMEM((SUBLANE, topk), I32),
                        pltpu.VMEM((topk * per, d), F32),
                        pltpu.SemaphoreType.DMA(()), pltpu.SemaphoreType.DMA(())],
    )
    return pl.pallas_call(
        functools.partial(_sample_attend_body, cfg=cfg, topk=topk),
        grid_spec=grid_spec,
        out_shape=jax.ShapeDtypeStruct((n, nh, d), BF16),
        compiler_params=_params("arbitrary"),
        name="sample_attend",
    )(page_table, scores, q, kv_new, cos_s, sin_s, cache_kv)


def _merge_body(oa_ref, hl_ref, woa_ref, wol_ref, ga_ref, gb_ref, o_ref):
    ya = jnp.dot(oa_ref[...], woa_ref[...], preferred_element_type=F32)
    yl = jnp.dot(hl_ref[...], wol_ref[...], preferred_element_type=F32)
    o_ref[...] = (jax.nn.sigmoid(ga_ref[...]) * ya + jax.nn.sigmoid(gb_ref[...]) * yl).astype(o_ref.dtype)


def _merge(o_attn, h_lru, w_oa, w_ol, z_b, lru_w, cfg):
    m, aw = o_attn.shape
    dm = w_oa.shape[1]
    tm = _pick(m, (1408, 1024, 768, 640, 512, 256))
    tn = _pick(dm, (512, 256, 128))
    assert lru_w % tn == 0
    ga_off = lru_w // tn
    gb_off = (lru_w + dm) // tn
    return pl.pallas_call(
        _merge_body,
        grid=(m // tm, dm // tn),
        in_specs=[pl.BlockSpec((tm, aw), lambda i, j: (i, 0)),
                  pl.BlockSpec((tm, lru_w), lambda i, j: (i, 0)),
                  pl.BlockSpec((aw, tn), lambda i, j: (0, j)),
                  pl.BlockSpec((lru_w, tn), lambda i, j: (0, j)),
                  pl.BlockSpec((tm, tn), lambda i, j: (i, ga_off + j)),
                  pl.BlockSpec((tm, tn), lambda i, j: (i, gb_off + j))],
        out_specs=pl.BlockSpec((tm, tn), lambda i, j: (i, j)),
        out_shape=jax.ShapeDtypeStruct((m, dm), BF16),
        compiler_params=_params("arbitrary", "arbitrary"),
        name="merge_mixers",
    )(o_attn, h_lru, w_oa, w_ol, z_b, z_b)


def _proj_ln_body(a_ref, w_ref, x_ref, g_ref, b_ref, o_ref, obf_ref, opk_ref, *, alpha, eps, tn):
    j = pl.program_id(1)
    c0 = pl.multiple_of(j * tn, tn)
    y = jnp.dot(a_ref[...], w_ref[...], preferred_element_type=F32)
    o_ref[:, pl.ds(c0, tn)] = F32(alpha) * x_ref[...] + y

    @pl.when(j == pl.num_programs(1) - 1)
    def _():
        out = _layer_norm(o_ref[...], g_ref[...], b_ref[...], eps)
        o_ref[...] = out
        obf_ref[...] = out.astype(BF16)
        opk_ref[...] = _pack_halves(out)


def _proj_residual_ln(a, w, x, g, b, alpha, cfg):
    m, k = a.shape
    dm = w.shape[1]
    tm = _pick(m, (384, 256))
    tn = _pick(dm, (512, 256, 128))
    return pl.pallas_call(
        functools.partial(_proj_ln_body, alpha=alpha, eps=cfg.ln_eps, tn=tn),
        grid=(m // tm, dm // tn),
        in_specs=[pl.BlockSpec((tm, k), lambda i, j: (i, 0)),
                  pl.BlockSpec((k, tn), lambda i, j: (0, j)),
                  pl.BlockSpec((tm, tn), lambda i, j: (i, j)),
                  pl.BlockSpec((1, dm), lambda i, j: (0, 0)),
                  pl.BlockSpec((1, dm), lambda i, j: (0, 0))],
        out_specs=[pl.BlockSpec((tm, dm), lambda i, j: (i, 0)),
                   pl.BlockSpec((tm, dm), lambda i, j: (i, 0)),
                   pl.BlockSpec((tm, dm // 2), lambda i, j: (i, 0))],
        out_shape=[jax.ShapeDtypeStruct((m, dm), F32), jax.ShapeDtypeStruct((m, dm), BF16),
                   jax.ShapeDtypeStruct((m, dm // 2), U32)],
        compiler_params=_params("arbitrary", "arbitrary"),
        name="proj_residual_ln",
    )(a, w, x, g, b)


def _first_index_of_max(v, idx, big):
    m = jnp.max(v, axis=0, keepdims=True)
    first = jnp.min(jnp.where(v == m, idx, big), axis=0, keepdims=True)
    return m, first


def _router_body(x_ref, wr_ref, br_ref, ids_ref, gate_ref, rank_ref, cnt_ref, *, cfg, n_tok):
    i = pl.program_id(0)
    e, ng = cfg.n_experts, cfg.n_groups
    per = e // ng
    tm = x_ref.shape[0]
    nt = (((1,), (1,)), ((), ()))
    logits = lax.dot_general(wr_ref[...], x_ref[...], nt, preferred_element_type=F32)
    s = jax.nn.sigmoid(logits)
    choice = s + br_ref[...]
    eidx = lax.broadcasted_iota(I32, (e, tm), 0)

    grp_rows = []
    jidx = lax.broadcasted_iota(I32, (per, tm), 0)
    for g in range(ng):
        cg = choice[g * per:(g + 1) * per, :]
        m1, j1 = _first_index_of_max(cg, jidx, per)
        m2 = jnp.max(jnp.where(jidx == j1, -jnp.inf, cg), axis=0, keepdims=True)
        grp_rows.append(m1 + m2)
    grp = jnp.concatenate(grp_rows, axis=0)

    gidx = lax.broadcasted_iota(I32, (ng, tm), 0)
    grp_keep = jnp.zeros((ng, tm), F32)
    work = grp
    for _ in range(cfg.topk_groups):
        _, gsel = _first_index_of_max(work, gidx, ng)
        hit = gidx == gsel
        grp_keep = jnp.where(hit, 1.0, grp_keep)
        work = jnp.where(hit, -jnp.inf, work)

    keep_rows = [jnp.broadcast_to(grp_keep[g:g + 1, :], (per, tm)) for g in range(ng)]
    masked = jnp.where(jnp.concatenate(keep_rows, axis=0) > 0.5, choice, -jnp.inf)

    ids, wts, hits = [], [], []
    for _ in range(cfg.top_k):
        _, esel = _first_index_of_max(masked, eidx, e)
        hit = eidx == esel
        ids.append(esel)
        hits.append(hit)
        wts.append(jnp.sum(jnp.where(hit, s, 0.0), axis=0, keepdims=True))
        masked = jnp.where(hit, -jnp.inf, masked)
    wk = jnp.concatenate(wts, axis=0)
    ids_ref[...] = jnp.concatenate(ids, axis=0)
    gate_ref[...] = F32(cfg.route_scale) * wk / jnp.sum(wk, axis=0, keepdims=True)

    @pl.when(i == 0)
    def _():
        cnt_ref[...] = jnp.zeros(cnt_ref.shape, F32)

    tok = i * tm + lax.broadcasted_iota(I32, (e, tm), 1)
    picked = jnp.zeros((e, tm), F32)
    for h in hits:
        picked = jnp.where(h, 1.0, picked)
    picked = jnp.where(tok < n_tok, picked, 0.0)
    before = (lax.broadcasted_iota(I32, (tm, tm), 0) < lax.broadcasted_iota(I32, (tm, tm), 1))
    earlier = jnp.dot(picked.astype(BF16), jnp.where(before, 1.0, 0.0).astype(BF16),
                      preferred_element_type=F32)
    rank_all = cnt_ref[:, 0:1] + earlier
    rank_ref[...] = jnp.concatenate(
        [jnp.sum(jnp.where(h, rank_all, 0.0), axis=0, keepdims=True) for h in hits], axis=0).astype(I32)
    cnt_ref[...] = cnt_ref[...] + jnp.sum(picked, axis=1, keepdims=True)


def _router(x_bf, w_r_t, b_r, n_tok, cfg):
    m, dm = x_bf.shape
    tm = _pick(m, (256, 128))
    e = cfg.n_experts
    pick = pl.BlockSpec((cfg.top_k, tm), lambda i: (0, i))
    return pl.pallas_call(
        functools.partial(_router_body, cfg=cfg, n_tok=n_tok),
        grid=(m // tm,),
        in_specs=[pl.BlockSpec((tm, dm), lambda i: (i, 0)),
                  pl.BlockSpec((e, dm), lambda i: (0, 0)),
                  pl.BlockSpec((e, 1), lambda i: (0, 0))],
        out_specs=[pick, pick, pick, pl.BlockSpec((e, LANE), lambda i: (0, 0))],
        out_shape=[jax.ShapeDtypeStruct((cfg.top_k, m), I32),
                   jax.ShapeDtypeStruct((cfg.top_k, m), F32),
                   jax.ShapeDtypeStruct((cfg.top_k, m), I32),
                   jax.ShapeDtypeStruct((e, LANE), F32)],
        compiler_params=_params("arbitrary"),
        name="router",
    )(x_bf, w_r_t, b_r)


DMA_THREADS = 2


def _row_pitch(sub):
    return -(-sub // SUBLANE) * SUBLANE + SUBLANE


def _start_row_gather(idx_row, idx_smem, idx_sem, src_hbm, buf, sem, n, sub):
    cp = pltpu.make_async_copy(idx_row, idx_smem, idx_sem)
    cp.start()
    cp.wait()

    def start(g, carry):
        for u in range(DMA_THREADS):
            r = g * DMA_THREADS + u
            r0 = pl.multiple_of(r * _row_pitch(sub), SUBLANE)
            pltpu.make_async_copy(src_hbm.at[idx_smem[0, r]], buf.at[pl.ds(r0, sub)], sem).start(priority=u)
        return carry
    lax.fori_loop(0, n // DMA_THREADS, start, 0)


def _wait_row_gather(buf, sem, n, sub):
    pltpu.make_async_copy(buf.at[pl.ds(0, n * sub)], buf.at[pl.ds(0, n * sub)], sem).wait()


def _gathered_slab(buf, first, count, j, sub):
    return buf[pl.ds(first * _row_pitch(sub) + j, count, stride=_row_pitch(sub)), :]


def _dispatch_body(nrows_ref, src_ref, x_hbm, o_ref, idx_smem, buf_ref, idx_sem, sems, *, tile):
    i = pl.program_id(0)
    sub = x_hbm.shape[1]
    n_valid = (nrows_ref[0] + tile - 1) // tile

    def start(t):
        slot = t % 2
        _start_row_gather(src_ref.at[t], idx_smem.at[pl.ds(slot, 1)], idx_sem, x_hbm,
                          buf_ref.at[slot], sems.at[slot], tile, sub)

    @pl.when(i == 0)
    def _():
        start(0)

    @pl.when(i + 1 < n_valid)
    def _():
        start(i + 1)

    @pl.when(i < n_valid)
    def _():
        slot = i % 2
        _wait_row_gather(buf_ref.at[slot], sems.at[slot], tile, sub)
        half = sub * LANE
        for j in range(sub):
            lo, hi = _unpack_halves(_gathered_slab(buf_ref.at[slot], 0, tile, j, sub))
            o_ref[:, j * LANE:(j + 1) * LANE] = lo.astype(o_ref.dtype)
            o_ref[:, half + j * LANE:half + (j + 1) * LANE] = hi.astype(o_ref.dtype)


def _dispatch(x3, src_rows, n_rows, tile):
    r_max = src_rows.shape[0]
    sub = x3.shape[1]
    n_tiles = r_max // tile
    last = lambda i, nr: jnp.minimum(i, (nr[0] - 1) // tile)
    grid_spec = pltpu.PrefetchScalarGridSpec(
        num_scalar_prefetch=1,
        grid=(n_tiles,),
        in_specs=[pl.BlockSpec((n_tiles, 1, tile), lambda i, nr: (0, 0, 0)),
                  pl.BlockSpec(memory_space=pl.ANY)],
        out_specs=pl.BlockSpec((tile, 2 * sub * LANE), lambda i, nr: (last(i, nr), 0)),
        scratch_shapes=[pltpu.SMEM((2, tile), I32), pltpu.VMEM((2, tile * _row_pitch(sub), LANE), x3.dtype),
                        pltpu.SemaphoreType.DMA(()), pltpu.SemaphoreType.DMA((2,))],
    )
    return pl.pallas_call(
        functools.partial(_dispatch_body, tile=tile),
        grid_spec=grid_spec,
        out_shape=jax.ShapeDtypeStruct((r_max, 2 * sub * LANE), BF16),
        compiler_params=_params("arbitrary"),
        name="moe_dispatch",
    )(n_rows, src_rows.reshape(n_tiles, 1, tile), x3)


def _stream_expert_weights(te_ref, tf_ref, nx_ref, nt_ref, w_hbms, w_casts, wbuf, sems, cnt):
    p, m = pl.program_id(0), pl.program_id(1)
    width = w_casts[0].shape[1]

    def copies(expert, blk, slot):
        col = pl.multiple_of(blk * width, width)
        return [pltpu.make_async_copy(w.at[expert, :, pl.ds(col, width)], wbuf.at[slot, i], sems.at[slot])
                for i, w in enumerate(w_hbms)]

    @pl.when((p == 0) & (m == 0))
    def _():
        cnt[0] = 0
        for c in copies(te_ref[0], 0, 0):
            c.start()

    @pl.when((tf_ref[m] == 1) & (m < nt_ref[0]))
    def _():
        slot = cnt[0] % 2
        for c in copies(te_ref[m], p, slot):
            c.wait()
        nxt = nx_ref[m]

        @pl.when(nxt >= 0)
        def _():
            for c in copies(nxt, p, 1 - slot):
                c.start()

        @pl.when((nxt < 0) & (p + 1 < pl.num_programs(0)))
        def _():
            for c in copies(te_ref[0], p + 1, 1 - slot):
                c.start()

        for i, dst in enumerate(w_casts):
            dst[...] = wbuf[slot, i].astype(BF16)
        cnt[0] = cnt[0] + 1


def _expert_up_body(te_ref, tf_ref, nx_ref, nt_ref, x_ref, wg_hbm, wu_hbm, o_ref, wg_s, wu_s, wbuf, sems, cnt):
    _stream_expert_weights(te_ref, tf_ref, nx_ref, nt_ref, (wg_hbm, wu_hbm), (wg_s, wu_s), wbuf, sems, cnt)

    @pl.when(pl.program_id(1) < nt_ref[0])
    def _():
        x = x_ref[...]
        g = jnp.dot(x, wg_s[...], preferred_element_type=F32)
        u = jnp.dot(x, wu_s[...], preferred_element_type=F32)
        o_ref[...] = (jax.nn.silu(g) * u).astype(o_ref.dtype)


def _expert_up(xs, w_gate, w_up, tile_expert, tile_first, tile_next, n_tiles, tile):
    r_max, dm = xs.shape
    e, _, f = w_gate.shape
    tf = _pick(f, (512, 256, 128))
    n_mt = r_max // tile
    clamp = lambda m, nt: jnp.minimum(m, nt[0] - 1)
    grid_spec = pltpu.PrefetchScalarGridSpec(
        num_scalar_prefetch=4,
        grid=(f // tf, n_mt),
        in_specs=[pl.BlockSpec((tile, dm), lambda fi, m, te, tfst, nx, nt: (clamp(m, nt), 0)),
                  pl.BlockSpec(memory_space=pl.ANY),
                  pl.BlockSpec(memory_space=pl.ANY)],
        out_specs=pl.BlockSpec((tile, tf), lambda fi, m, te, tfst, nx, nt: (clamp(m, nt), fi)),
        scratch_shapes=[pltpu.VMEM((dm, tf), BF16), pltpu.VMEM((dm, tf), BF16),
                        pltpu.VMEM((2, 2, dm, tf), F32), pltpu.SemaphoreType.DMA((2,)),
                        pltpu.SMEM((1,), I32)],
    )
    return pl.pallas_call(
        _expert_up_body,
        grid_spec=grid_spec,
        out_shape=jax.ShapeDtypeStruct((r_max, f), BF16),
        compiler_params=_params("arbitrary", "arbitrary"),
        name="expert_up",
    )(tile_expert, tile_first, tile_next, n_tiles, xs, w_gate, w_up)


def _expert_down_body(te_ref, tf_ref, nx_ref, nt_ref, h_ref, wd_hbm, o_ref, wd_s, wbuf, sems, cnt):
    _stream_expert_weights(te_ref, tf_ref, nx_ref, nt_ref, (wd_hbm,), (wd_s,), wbuf, sems, cnt)

    @pl.when(pl.program_id(1) < nt_ref[0])
    def _():
        o_ref[...] = jnp.dot(h_ref[...], wd_s[...], preferred_element_type=F32)


def _expert_down(hid, w_down, tile_expert, tile_first, tile_next, n_tiles, tile):
    r_max, f = hid.shape
    dm = w_down.shape[2]
    tn = _pick(dm, (2048, 1024, 512, 256, 128))
    n_mt = r_max // tile
    clamp = lambda m, nt: jnp.minimum(m, nt[0] - 1)
    grid_spec = pltpu.PrefetchScalarGridSpec(
        num_scalar_prefetch=4,
        grid=(dm // tn, n_mt),
        in_specs=[pl.BlockSpec((tile, f), lambda ni, m, te, tfst, nx, nt: (clamp(m, nt), 0)),
                  pl.BlockSpec(memory_space=pl.ANY)],
        out_specs=pl.BlockSpec((tile, tn), lambda ni, m, te, tfst, nx, nt: (clamp(m, nt), ni)),
        scratch_shapes=[pltpu.VMEM((f, tn), BF16), pltpu.VMEM((2, 1, f, tn), F32),
                        pltpu.SemaphoreType.DMA((2,)), pltpu.SMEM((1,), I32)],
    )
    return pl.pallas_call(
        _expert_down_body,
        grid_spec=grid_spec,
        out_shape=jax.ShapeDtypeStruct((r_max, dm), F32),
        compiler_params=_params("arbitrary", "arbitrary"),
        name="expert_down",
    )(tile_expert, tile_first, tile_next, n_tiles, hid, w_down)


def _shared_up_body(x_ref, wg_ref, wu_ref, o_ref):
    x = x_ref[...]
    g = jnp.dot(x, wg_ref[...], preferred_element_type=F32)
    u = jnp.dot(x, wu_ref[...], preferred_element_type=F32)
    o_ref[...] = (jax.nn.silu(g) * u).astype(o_ref.dtype)


def _shared_up(x_bf, wg, wu):
    m, dm = x_bf.shape
    f = wg.shape[1]
    tm = _pick(m, (1408, 1024, 768, 640, 512, 256))
    tf = _pick(f, (256, 128))
    return pl.pallas_call(
        _shared_up_body,
        grid=(m // tm, f // tf),
        in_specs=[pl.BlockSpec((tm, dm), lambda i, j: (i, 0)),
                  pl.BlockSpec((dm, tf), lambda i, j: (0, j)),
                  pl.BlockSpec((dm, tf), lambda i, j: (0, j))],
        out_specs=pl.BlockSpec((tm, tf), lambda i, j: (i, j)),
        out_shape=jax.ShapeDtypeStruct((m, f), BF16),
        compiler_params=_params("arbitrary", "arbitrary"),
        name="shared_up",
    )(x_bf, wg, wu)


def _combine_body(pos_ref, gate_ref, ys_hbm, x_ref, sh_ref, wsd_ref, g_ref, b_ref, o_ref,
                  idx_smem, buf_ref, idx_sem, sems, *, tb, top_k, alpha, eps):
    i = pl.program_id(0)
    n = tb * top_k
    dm = x_ref.shape[1]

    def start(t):
        slot = t % 2
        cp = pltpu.make_async_copy(pos_ref.at[t], idx_smem.at[pl.ds(slot, 1)], idx_sem)
        cp.start()
        cp.wait()

        def issue(g, carry):
            for u in range(DMA_THREADS):
                r = g * DMA_THREADS + u
                pltpu.make_async_copy(ys_hbm.at[pl.ds(idx_smem[slot, r], 1), :],
                                      buf_ref.at[slot, pl.ds(r, 1), :], sems.at[slot]).start(priority=u)
            return carry
        lax.fori_loop(0, n // DMA_THREADS, issue, 0)

    @pl.when(i == 0)
    def _():
        start(0)

    @pl.when(i + 1 < pl.num_programs(0))
    def _():
        start(i + 1)

    shared = jnp.dot(sh_ref[...], wsd_ref[...], preferred_element_type=F32)
    slot = i % 2
    pltpu.make_async_copy(buf_ref.at[slot], buf_ref.at[slot], sems.at[slot]).wait()
    routed = shared
    for k in range(top_k):
        gate = jnp.concatenate([gate_ref[k * tb:(k + 1) * tb, :]] * (dm // LANE), axis=1)
        routed = routed + gate * buf_ref[slot, k * tb:(k + 1) * tb, :]
    o_ref[...] = _layer_norm(F32(alpha) * x_ref[...] + routed, g_ref[...], b_ref[...], eps)


def _combine(pos, gate_rows, ys, x, sh_hid, ws_down, g, b, alpha, tb, cfg):
    m, dm = x.shape
    f = sh_hid.shape[1]
    n = tb * cfg.top_k
    nblk = m // tb
    return pl.pallas_call(
        functools.partial(_combine_body, tb=tb, top_k=cfg.top_k, alpha=alpha, eps=cfg.ln_eps),
        grid=(nblk,),
        in_specs=[pl.BlockSpec((nblk, 1, n), lambda i: (0, 0, 0)),
                  pl.BlockSpec((n, LANE), lambda i: (i, 0)),
                  pl.BlockSpec(memory_space=pl.ANY),
                  pl.BlockSpec((tb, dm), lambda i: (i, 0)),
                  pl.BlockSpec((tb, f), lambda i: (i, 0)),
                  pl.BlockSpec((f, dm), lambda i: (0, 0)),
                  pl.BlockSpec((1, dm), lambda i: (0, 0)),
                  pl.BlockSpec((1, dm), lambda i: (0, 0))],
        out_specs=pl.BlockSpec((tb, dm), lambda i: (i, 0)),
        out_shape=jax.ShapeDtypeStruct((m, dm), F32),
        scratch_shapes=[pltpu.SMEM((2, n), I32), pltpu.VMEM((2, n, dm), F32),
                        pltpu.SemaphoreType.DMA(()), pltpu.SemaphoreType.DMA((2,))],
        compiler_params=_params("arbitrary"),
        name="moe_combine",
    )(pos, gate_rows, ys, x, sh_hid, ws_down, g, b)


def _routing_tables(ids, gates, ranks, counts, n_tok, tile, tb, cfg):
    e, k = cfg.n_experts, cfg.top_k
    t_all = ids.shape[1]
    r_max = ((n_tok * k + e * (tile - 1)) // tile + 1) * tile
    counts = counts[:, 0].astype(I32)
    padded = ((counts + tile - 1) // tile) * tile
    ends = jnp.cumsum(padded)
    start_pad = ends - padded
    eye = ids[:, :, None] == jnp.arange(e, dtype=I32)[None, None, :]
    dest = jnp.sum(jnp.where(eye, start_pad[None, None, :], 0), axis=2) + ranks
    tok = jnp.broadcast_to(jnp.arange(t_all, dtype=I32)[None, :], (k, t_all))
    real = tok < n_tok
    src_rows = jnp.zeros((r_max,), I32).at[jnp.where(real, dest, r_max).reshape(-1)].set(
        tok.reshape(-1), mode="drop", unique_indices=True)
    blocks = lambda a: a.reshape(k, t_all // tb, tb).transpose(1, 0, 2).reshape(t_all // tb, 1, k * tb)
    pos = blocks(jnp.where(real, dest, 0))
    gate_rows = jnp.broadcast_to(blocks(jnp.where(real, gates, 0.0)).reshape(-1, 1), (t_all * k, LANE))
    n_rows = ends[-1:]
    tile_start = jnp.arange(r_max // tile, dtype=I32) * tile
    tile_expert = jnp.minimum(jnp.sum((ends[None, :] <= tile_start[:, None]).astype(I32), axis=1), e - 1)
    prev = jnp.concatenate([jnp.full((1,), -1, I32), tile_expert[:-1]])
    tile_first = (tile_expert != prev).astype(I32)
    ar = jnp.arange(e, dtype=I32)
    later = jnp.where((padded > 0)[None, :] & (ar[None, :] > ar[:, None]), ar[None, :], e)
    next_expert = jnp.min(later, axis=1)
    next_expert = jnp.where(next_expert == e, -1, next_expert)
    tile_next = jnp.sum(jnp.where(tile_expert[:, None] == ar[None, :], next_expert[None, :], 0), axis=1)
    return src_rows, pos, gate_rows, n_rows, n_rows // tile, tile_expert, tile_first, tile_next


def _layer(cfg, l, x_all, n_p, seq, n_s, pos_all, cache_kv, cache_idx_k, page_table, state_conv, state_h,
           w_in, ik_g, ik_b, conv_w, conv_b, lru_w_a, lru_b_a, lru_w_x, lru_b_x, lru_lambda,
           w_o_attn, w_o_lru, w_out, ln1_g, ln1_b, w_router, b_router,
           w_gate, w_up, w_down, ws_gate, ws_up, ws_down, ln2_g, ln2_b):
    t_all, dm = x_all.shape
    t_p = n_p * seq
    d, di = cfg.head_dim, cfg.idx_dim
    qw, kvw, iw = cfg.n_heads * d, cfg.n_kv_heads * d, cfg.idx_heads * di
    lw = conv_w.shape[1]
    alpha = (2.0 * cfg.depth) ** 0.25
    c_qkv = qw + 2 * kvw
    c_i = c_qkv + iw + di
    c_w = c_i + cfg.idx_heads

    x_bf = x_all.astype(BF16)
    w_in_bf = w_in.astype(BF16)
    z_qkv = _matmul(x_bf, w_in_bf, 0, c_qkv, name="in_proj_qkv")
    z_i = _matmul(x_bf, w_in_bf, c_qkv, iw, name="in_proj_idx_q")
    z_kw = _matmul(x_bf, w_in_bf, c_qkv + iw, 2 * LANE, name="in_proj_idx_kw")
    z_b = _matmul(x_bf, w_in_bf[:, c_w:], name="in_proj_lru_gates")

    cosf, sinf = _rope_tables(pos_all, d, cfg.rope_theta)
    kv_all, ki_all, k_bf, ki_bf = _finalize_keys(
        z_qkv, z_kw, cosf, sinf, ik_g.reshape(1, di), ik_b.reshape(1, di), cfg)
    v_t = _transpose_values(z_qkv, t_p, cfg)

    o_attn_p = _prompt_attention(z_qkv, z_i, z_kw, cosf, sinf, k_bf, ki_bf, v_t, n_p, seq, cfg)
    wa_bf, wx_bf = lru_w_a.astype(BF16), lru_w_x.astype(BF16)
    row = lambda v: v.reshape(1, -1)
    h_lru_p, h_last_p = _lru_prompt(z_b, conv_w, row(conv_b), wa_bf, wx_bf, row(lru_b_a), row(lru_b_x),
                                    row(lru_lambda), n_p, seq, cfg)

    sl = slice(t_p, t_p + n_s)
    cos_s, sin_s = cosf[sl][:, None, :], sinf[sl][:, None, :]
    qi_s = z_i[sl].reshape(n_s, cfg.idx_heads, di)
    w_rep = jnp.broadcast_to(z_kw[sl, di:di + cfg.idx_heads][:, :, None], (n_s, cfg.idx_heads, LANE))
    n_phys = cache_idx_k.shape[0]
    scores = _sample_scores(page_table, qi_s, w_rep, ki_all[sl][:, None, :], cos_s, sin_s, cache_idx_k, cfg)
    q_s = z_qkv[sl, :qw].reshape(n_s, cfg.n_heads, d)
    kv_new = kv_all[sl].reshape(n_s, 2 * cfg.n_kv_heads, d)
    o_attn_s = _sample_attend(page_table, scores, q_s, kv_new, cos_s, sin_s,
                              cache_kv.reshape(n_phys * cfg.page_size * 2 * cfg.n_kv_heads, d), cfg)
    xl_s = z_b[sl, :lw]
    hist_s = jnp.moveaxis(state_conv, 1, 0)
    h_s, h_s_bf = _lru_sample(xl_s, hist_s, state_h, conv_w, row(conv_b), wa_bf, wx_bf,
                              row(lru_b_a), row(lru_b_x), row(lru_lambda), cfg)

    pad = t_all - t_p - n_s
    o_attn = jnp.concatenate([o_attn_p, o_attn_s.reshape(n_s, qw), jnp.zeros((pad, qw), BF16)], axis=0)
    h_lru = jnp.concatenate([h_lru_p, h_s_bf, jnp.zeros((pad, lw), BF16)], axis=0)

    merged = _merge(o_attn, h_lru, w_o_attn.astype(BF16), w_o_lru.astype(BF16), z_b, lw, cfg)
    x1, x1_bf, x1_pk = _proj_residual_ln(merged, w_out.astype(BF16), x_all, row(ln1_g), row(ln1_b), alpha, cfg)

    n_tok = t_p + n_s
    ids, gates, ranks, counts = _router(x1_bf, w_router.T.astype(BF16), b_router.reshape(-1, 1), n_tok, cfg)
    tile = cfg.moe_tile
    tb = _pick(t_all, (64, 32, 16, 8))
    src_rows, pos, gate_rows, n_rows, n_tiles, tile_expert, tile_first, tile_next = _routing_tables(
        ids, gates, ranks, counts, n_tok, tile, tb, cfg)
    xs = _dispatch(x1_pk.reshape(t_all, dm // (2 * LANE), LANE), src_rows, n_rows, tile)
    hid = _expert_up(xs, w_gate, w_up, tile_expert, tile_first, tile_next, n_tiles, tile)
    ys = _expert_down(hid, w_down, tile_expert, tile_first, tile_next, n_tiles, tile)
    sh_hid = _shared_up(x1_bf, ws_gate.astype(BF16), ws_up.astype(BF16))
    y_all = _combine(pos, gate_rows, ys, x1, sh_hid, ws_down.astype(BF16), row(ln2_g), row(ln2_b),
                     alpha, tb, cfg)

    conv_p = z_b[:t_p, :lw].reshape(n_p, seq, lw)[:, seq - (cfg.conv_width - 1):, :]
    conv_s = jnp.concatenate([state_conv[:, 1:, :], xl_s[:, None, :]], axis=1)
    outs = dict(
        kv_p=kv_all[:t_p].reshape(n_p, seq, 2, cfg.n_kv_heads, d),
        ik_p=ki_all[:t_p].reshape(n_p, seq, di),
        cv_p=conv_p, h_p=h_last_p.reshape(n_p, lw),
        kv_s=kv_all[sl].reshape(n_s, 1, 2, cfg.n_kv_heads, d),
        ik_s=ki_all[sl].reshape(n_s, 1, di),
        cv_s=conv_s, h_s=h_s)
    return y_all, outs


def _forward(cfg, x_prompt, x_sample, cache_kv, cache_idx_k, page_table, state_conv, state_h,
             w_in, idx_k_norm_g, idx_k_norm_b, conv_w, conv_b, lru_w_a, lru_b_a, lru_w_x, lru_b_x,
             lru_lambda, w_o_attn, w_o_lru, w_out, ln1_g, ln1_b, w_router, b_router,
             w_gate, w_up, w_down, ws_gate, ws_up, ws_down, ln2_g, ln2_b):
    n_p, seq, dm = x_prompt.shape
    n_s, t_s, _ = x_sample.shape
    assert t_s == 1 and cfg.depth == 1 and w_in.shape[0] == 1
    past = page_table.shape[1] * cfg.page_size
    t_p = n_p * seq
    t_all = -(-(t_p + n_s) // cfg.row_align) * cfg.row_align
    pad = t_all - t_p - n_s
    x_all = jnp.concatenate([x_prompt.reshape(t_p, dm), x_sample.reshape(n_s, dm),
                             jnp.zeros((pad, dm), x_prompt.dtype)], axis=0)
    pos_all = jnp.concatenate([jnp.tile(jnp.arange(seq), n_p), jnp.full((n_s,), past), jnp.zeros((pad,), I32)])
    l = 0
    y_all, o = _layer(cfg, l, x_all, n_p, seq, n_s, pos_all, cache_kv[l], cache_idx_k[l], page_table,
                      state_conv[l], state_h[l], w_in[l], idx_k_norm_g[l], idx_k_norm_b[l],
                      conv_w[l], conv_b[l], lru_w_a[l], lru_b_a[l], lru_w_x[l], lru_b_x[l], lru_lambda[l],
                      w_o_attn[l], w_o_lru[l], w_out[l], ln1_g[l], ln1_b[l], w_router[l], b_router[l],
                      w_gate[l], w_up[l], w_down[l], ws_gate[l], ws_up[l], ws_down[l], ln2_g[l], ln2_b[l])
    y_p = y_all[:t_p].reshape(n_p, seq, dm)
    y_s = y_all[t_p:t_p + n_s].reshape(n_s, 1, dm)
    lead = lambda a: a[None]
    return (y_p, y_s, lead(o["kv_p"]), lead(o["ik_p"]), lead(o["cv_p"]), lead(o["h_p"]),
            lead(o["kv_s"]), lead(o["ik_s"]), lead(o["cv_s"]), lead(o["h_s"]))


def kernel(x_prompt, x_sample, cache_kv, cache_idx_k, page_table, state_conv, state_h, w_in, idx_k_norm_g, idx_k_norm_b, conv_w, conv_b, lru_w_a, lru_b_a, lru_w_x, lru_b_x, lru_lambda, w_o_attn, w_o_lru, w_out, ln1_g, ln1_b, w_router, b_router, w_gate, w_up, w_down, ws_gate, ws_up, ws_down, ln2_g, ln2_b):
    return _forward(Cfg(), x_prompt, x_sample, cache_kv, cache_idx_k, page_table, state_conv, state_h,
                    w_in, idx_k_norm_g, idx_k_norm_b, conv_w, conv_b, lru_w_a, lru_b_a, lru_w_x, lru_b_x,
                    lru_lambda, w_o_attn, w_o_lru, w_out, ln1_g, ln1_b, w_router, b_router,
                    w_gate, w_up, w_down, ws_gate, ws_up, ws_down, ln2_g, ln2_b)
```

```python
import functools
from typing import NamedTuple

import jax
import jax.numpy as jnp
import numpy as np
from jax import lax
from jax.experimental import pallas as pl
from jax.experimental.pallas import tpu as pltpu

F32 = jnp.float32
BF16 = jnp.bfloat16
I32 = jnp.int32
U32 = jnp.uint32

LANE = 128
SUBLANE = 8
VMEM_LIMIT = 56 * 1024 * 1024
MASKED = -1e30
INT_MIN = -2 ** 31


class Cfg(NamedTuple):
    n_heads: int = 16
    n_kv_heads: int = 4
    head_dim: int = 128
    idx_heads: int = 32
    idx_dim: int = 128
    topk_max: int = 256
    q_block: int = 128
    rope_theta: float = 10000.0
    lru_blocks: int = 16
    conv_width: int = 4
    lru_c: float = 8.0
    n_experts: int = 64
    top_k: int = 8
    n_groups: int = 8
    topk_groups: int = 4
    route_scale: float = 2.5
    ln_eps: float = 1e-5
    page_size: int = 128
    depth: int = 1
    key_chunk: int = 512
    moe_tile: int = 512
    row_align: int = 256


def _pick(dim, prefs):
    for p in prefs:
        if p <= dim and dim % p == 0:
            return p
    return dim


def _params(*sem):
    return pltpu.CompilerParams(dimension_semantics=sem, vmem_limit_bytes=VMEM_LIMIT)


def _mm_body(x_ref, w_ref, o_ref):
    o_ref[...] = jnp.dot(x_ref[...], w_ref[...], preferred_element_type=F32).astype(o_ref.dtype)


def _matmul(x, w, col0=0, n=None, out_dtype=F32, name="matmul"):
    m, k = x.shape
    n = w.shape[1] - col0 if n is None else n
    tm = _pick(m, (1408, 1024, 768, 640, 512, 256))
    tn = _pick(n, (512, 384, 256, 128))
    assert col0 % tn == 0
    j0 = col0 // tn
    return pl.pallas_call(
        _mm_body,
        grid=(m // tm, n // tn),
        in_specs=[pl.BlockSpec((tm, k), lambda i, j: (i, 0)),
                  pl.BlockSpec((k, tn), lambda i, j: (0, j0 + j))],
        out_specs=pl.BlockSpec((tm, tn), lambda i, j: (i, j)),
        out_shape=jax.ShapeDtypeStruct((m, n), out_dtype),
        compiler_params=_params("arbitrary", "arbitrary"),
        name=name,
    )(x, w)


def _rope(x, cosf, sinf):
    return x * cosf + pltpu.roll(x, x.shape[-1] // 2, axis=x.ndim - 1) * sinf


def _rope_tables(pos, dim, theta):
    half = dim // 2
    inv = theta ** (-jnp.arange(half, dtype=F32) / half)
    ang = pos.astype(F32)[:, None] * inv[None, :]
    cos, sin = jnp.cos(ang), jnp.sin(ang)
    return jnp.concatenate([cos, cos], -1), jnp.concatenate([-sin, sin], -1)


def _pack_halves(y):
    w = y.shape[1] // 2
    lo = pltpu.bitcast(y[:, :w].astype(BF16).astype(F32), U32)
    hi = pltpu.bitcast(y[:, w:].astype(BF16).astype(F32), U32)
    return (hi & jnp.uint32(0xFFFF0000)) | (lo >> 16)


def _unpack_halves(p):
    lo = pltpu.bitcast(p << 16, F32)
    hi = pltpu.bitcast(p & jnp.uint32(0xFFFF0000), F32)
    return lo, hi


def _layer_norm(y, g, b, eps):
    mu = jnp.mean(y, axis=-1, keepdims=True)
    yc = y - mu
    var = jnp.mean(yc * yc, axis=-1, keepdims=True)
    return yc * lax.rsqrt(var + eps) * g + b


def _kv_body(kv_ref, ki_ref, cos_ref, sin_ref, g_ref, b_ref,
             kvo_ref, kio_ref, kbf_ref, kibf_ref, *, cfg):
    c_heads, d = cfg.n_kv_heads, cfg.head_dim
    cosf, sinf = cos_ref[...], sin_ref[...]
    kv = kv_ref[...]
    for c in range(c_heads):
        kc = _rope(kv[:, c * d:(c + 1) * d], cosf, sinf)
        kvo_ref[:, c * d:(c + 1) * d] = kc
        kbf_ref[:, c * d:(c + 1) * d] = kc.astype(BF16)
    kvo_ref[:, c_heads * d:] = kv[:, c_heads * d:]
    ki = _rope(_layer_norm(ki_ref[...], g_ref[...], b_ref[...], cfg.ln_eps), cosf, sinf)
    kio_ref[...] = ki
    kibf_ref[...] = ki.astype(BF16)


def _vt_body(v_ref, vt_ref, *, cfg):
    d = cfg.head_dim
    for c in range(cfg.n_kv_heads):
        vt_ref[0, c * d:(c + 1) * d, :] = v_ref[:, c * d:(c + 1) * d].T.astype(BF16)


def _transpose_values(z_qkv, t_p, cfg):
    kc = cfg.key_chunk
    kvw = cfg.n_kv_heads * cfg.head_dim
    v_blk = (cfg.n_heads * cfg.head_dim + kvw) // kvw
    assert t_p % kc == 0
    return pl.pallas_call(
        functools.partial(_vt_body, cfg=cfg),
        grid=(t_p // kc,),
        in_specs=[pl.BlockSpec((kc, kvw), lambda i: (i, v_blk))],
        out_specs=pl.BlockSpec((1, kvw, kc), lambda i: (i, 0, 0)),
        out_shape=jax.ShapeDtypeStruct((t_p // kc, kvw, kc), BF16),
        compiler_params=_params("arbitrary"),
        name="transpose_values",
    )(z_qkv)


def _finalize_keys(z_qkv, z_i, cosf, sinf, ik_g, ik_b, n_seq, seq, cfg):
    t_all = z_qkv.shape[0]
    c_heads, d = cfg.n_kv_heads, cfg.head_dim
    kvw = c_heads * d
    qw = cfg.n_heads * d
    tr = cfg.row_align
    assert t_all % tr == 0 and seq % tr == 0 and qw % (2 * kvw) == 0
    per_seq, n_prompt = seq // tr, n_seq * seq // tr
    table = lambda i: (jnp.where(i < n_prompt, i % per_seq, per_seq + i - n_prompt), 0)
    return pl.pallas_call(
        functools.partial(_kv_body, cfg=cfg),
        grid=(t_all // tr,),
        in_specs=[pl.BlockSpec((tr, 2 * kvw), lambda i: (i, qw // (2 * kvw))),
                  pl.BlockSpec((tr, cfg.idx_dim), lambda i: (i, 0)),
                  pl.BlockSpec((tr, d), table),
                  pl.BlockSpec((tr, d), table),
                  pl.BlockSpec((1, cfg.idx_dim), lambda i: (0, 0)),
                  pl.BlockSpec((1, cfg.idx_dim), lambda i: (0, 0))],
        out_specs=[pl.BlockSpec((tr, 2 * kvw), lambda i: (i, 0)),
                   pl.BlockSpec((tr, cfg.idx_dim), lambda i: (i, 0)),
                   pl.BlockSpec((tr, kvw), lambda i: (i, 0)),
                   pl.BlockSpec((tr, cfg.idx_dim), lambda i: (i, 0))],
        out_shape=[jax.ShapeDtypeStruct((t_all, 2 * kvw), F32),
                   jax.ShapeDtypeStruct((t_all, cfg.idx_dim), F32),
                   jax.ShapeDtypeStruct((t_all, kvw), BF16),
                   jax.ShapeDtypeStruct((t_all, cfg.idx_dim), BF16)],
        compiler_params=_params("arbitrary"),
        name="finalize_keys",
    )(z_qkv, z_i, cosf, sinf, ik_g, ik_b)


def _sort_key(s):
    b = pltpu.bitcast(s, I32)
    return b ^ ((b >> 31) & jnp.int32(0x7FFFFFFF))


def _kth_largest_key(count_ge, k, width):
    def bit_step(i, ans):
        cand = ans | lax.shift_left(jnp.int32(1), jnp.int32(31) - i)
        cnt = count_ge(cand ^ jnp.int32(INT_MIN))
        return jnp.where(cnt >= k, cand, ans)

    ans = lax.fori_loop(0, 32, bit_step, jnp.zeros((1, width), I32))
    return ans ^ jnp.int32(INT_MIN)


def _attn_body(q_ref, qi_ref, wi_ref, cos_ref, sin_ref, kbf_ref, kibf_ref, vt_ref,
               o_ref, qs_ref, qis_ref, key_ref, m_ref, l_ref, acc_ref, *, cfg, topk):
    qb_idx = pl.program_id(1)
    qb, kc = cfg.q_block, cfg.key_chunk
    d, di = cfg.head_dim, cfg.idx_dim
    c_heads = cfg.n_kv_heads
    g_heads = cfg.n_heads // c_heads
    cosf, sinf = cos_ref[...], sin_ref[...]

    for h in range(cfg.n_heads):
        qs_ref[h * qb:(h + 1) * qb, :] = _rope(q_ref[:, h * d:(h + 1) * d], cosf, sinf).astype(BF16)
    for h in range(cfg.idx_heads):
        qis_ref[h * qb:(h + 1) * qb, :] = _rope(qi_ref[:, h * di:(h + 1) * di], cosf, sinf).astype(BF16)
    w_t = wi_ref[...].T * F32((di * cfg.idx_heads) ** -0.5)

    q0 = qb_idx * qb
    n_chunks = (q0 + qb + kc - 1) // kc
    tpos = q0 + lax.broadcasted_iota(I32, (kc, qb), 1)
    krow = lax.broadcasted_iota(I32, (kc, qb), 0)
    nt = (((1,), (1,)), ((), ()))

    def score_chunk(ci, carry):
        k0 = pl.multiple_of(ci * kc, kc)
        ki_c = kibf_ref[pl.ds(k0, kc), :]
        acc = jnp.zeros((kc, qb), F32)
        for hp in range(cfg.idx_heads // 2):
            z = lax.dot_general(ki_c, qis_ref[hp * 2 * qb:(hp + 1) * 2 * qb, :], nt,
                                preferred_element_type=F32)
            acc = acc + jnp.maximum(z[:, :qb], 0.0) * w_t[2 * hp:2 * hp + 1, :]
            acc = acc + jnp.maximum(z[:, qb:], 0.0) * w_t[2 * hp + 1:2 * hp + 2, :]
        causal = (k0 + krow) <= tpos
        key_ref[pl.ds(k0, kc), :] = _sort_key(jnp.where(causal, acc, -jnp.inf))
        return carry

    lax.fori_loop(0, n_chunks, score_chunk, 0)

    def count_ge(cand):
        def cnt_chunk(ci, acc8):
            k0 = pl.multiple_of(ci * kc, kc)
            hit = jnp.where(key_ref[pl.ds(k0, kc), :] >= cand, 1, 0).astype(I32)
            return acc8 + jnp.sum(hit.reshape(kc // SUBLANE, SUBLANE, qb), axis=0)
        acc8 = lax.fori_loop(0, n_chunks, cnt_chunk, jnp.zeros((SUBLANE, qb), I32))
        return jnp.sum(acc8, axis=0, keepdims=True)

    thr = _kth_largest_key(count_ge, topk, qb)

    surplus = jnp.where(thr > _sort_key(jnp.full((1, qb), -jnp.inf, F32)), count_ge(thr) - topk, 0)

    @pl.when(jnp.max(surplus) > 0)
    def _():
        wanted = (topk - count_ge(thr + 1)).astype(F32)
        earlier = jnp.where(lax.broadcasted_iota(I32, (kc, kc), 1) < lax.broadcasted_iota(I32, (kc, kc), 0),
                            1.0, 0.0).astype(BF16)

        def strike_chunk(ci, seen):
            k0 = pl.multiple_of(ci * kc, kc)
            blk = key_ref[pl.ds(k0, kc), :]
            tied = jnp.where(blk == thr, 1.0, 0.0)
            rank = seen + jnp.dot(earlier, tied.astype(BF16), preferred_element_type=F32)
            key_ref[pl.ds(k0, kc), :] = jnp.where((blk == thr) & (rank >= wanted), jnp.int32(INT_MIN), blk)
            return seen + jnp.sum(tied, axis=0, keepdims=True)

        lax.fori_loop(0, n_chunks, strike_chunk, jnp.zeros((1, qb), F32))

    m_ref[...] = jnp.full(m_ref.shape, MASKED, F32)
    l_ref[...] = jnp.zeros(l_ref.shape, F32)
    acc_ref[...] = jnp.zeros(acc_ref.shape, F32)
    scale = F32(d ** -0.5 * np.log2(np.e))

    def attend_chunk(ci, carry):
        k0 = pl.multiple_of(ci * kc, kc)
        keep = (key_ref[pl.ds(k0, kc), :] >= thr) & ((k0 + krow) <= tpos)
        bias = jnp.where(keep, 0.0, MASKED).astype(F32)
        bias = jnp.concatenate([bias] * g_heads, axis=1)
        for c in range(c_heads):
            k_c = kbf_ref[pl.ds(k0, kc), c * d:(c + 1) * d]
            s = lax.dot_general(k_c, qs_ref[c * g_heads * qb:(c + 1) * g_heads * qb, :], nt,
                                preferred_element_type=F32) * scale + bias
            m_old = m_ref[c:c + 1, :]
            m_new = jnp.maximum(m_old, jnp.max(s, axis=0, keepdims=True))
            alpha = jnp.exp2(m_old - m_new)
            p = jnp.exp2(s - m_new)
            l_ref[c:c + 1, :] = alpha * l_ref[c:c + 1, :] + jnp.sum(p, axis=0, keepdims=True)
            pv = jnp.dot(vt_ref[ci, c * d:(c + 1) * d, :], p.astype(BF16), preferred_element_type=F32)
            acc_ref[c * d:(c + 1) * d, :] = alpha * acc_ref[c * d:(c + 1) * d, :] + pv
            m_ref[c:c + 1, :] = m_new
        return carry

    lax.fori_loop(0, n_chunks, attend_chunk, 0)

    for c in range(c_heads):
        o_t = acc_ref[c * d:(c + 1) * d, :] / l_ref[c:c + 1, :]
        for g in range(g_heads):
            h = c * g_heads + g
            o_ref[:, h * d:(h + 1) * d] = o_t[:, g * qb:(g + 1) * qb].T.astype(o_ref.dtype)


def _prompt_attention(z_qkv, z_i, z_w, cosf, sinf, k_bf, ki_bf, v_t, n_seq, seq, cfg):
    qb, kc = cfg.q_block, cfg.key_chunk
    d = cfg.head_dim
    qw = cfg.n_heads * d
    kvw = cfg.n_kv_heads * d
    iw = cfg.idx_heads * cfg.idx_dim
    assert seq % kc == 0 and seq % qb == 0 and kc % qb == 0
    assert cfg.head_dim == LANE and cfg.idx_dim == LANE
    nqb = seq // qb
    topk = min(cfg.topk_max, seq // 4)
    row = lambda n, j: (n * nqb + j, 0)
    return pl.pallas_call(
        functools.partial(_attn_body, cfg=cfg, topk=topk),
        grid=(n_seq, nqb),
        in_specs=[pl.BlockSpec((qb, qw), row),
                  pl.BlockSpec((qb, iw), row),
                  pl.BlockSpec((qb, LANE), lambda n, j: (n * nqb + j, 1)),
                  pl.BlockSpec((qb, d), lambda n, j: (j, 0)),
                  pl.BlockSpec((qb, d), lambda n, j: (j, 0)),
                  pl.BlockSpec((seq, kvw), lambda n, j: (n, 0)),
                  pl.BlockSpec((seq, cfg.idx_dim), lambda n, j: (n, 0)),
                  pl.BlockSpec((seq // kc, kvw, kc), lambda n, j: (n, 0, 0))],
        out_specs=pl.BlockSpec((qb, qw), row),
        out_shape=jax.ShapeDtypeStruct((n_seq * seq, qw), BF16),
        scratch_shapes=[pltpu.VMEM((cfg.n_heads * qb, d), BF16),
                        pltpu.VMEM((cfg.idx_heads * qb, cfg.idx_dim), BF16),
                        pltpu.VMEM((seq, qb), I32),
                        pltpu.VMEM((SUBLANE, (cfg.n_heads // cfg.n_kv_heads) * qb), F32),
                        pltpu.VMEM((SUBLANE, (cfg.n_heads // cfg.n_kv_heads) * qb), F32),
                        pltpu.VMEM((kvw, (cfg.n_heads // cfg.n_kv_heads) * qb), F32)],
        compiler_params=_params("arbitrary", "arbitrary"),
        name="prompt_attention",
    )(z_qkv, z_i, z_w, cosf, sinf, k_bf, ki_bf, v_t)


def _softplus(x):
    return jnp.maximum(x, 0.0) + jnp.log1p(jnp.exp(-jnp.abs(x)))


def _lru_gates(xc, wa_ref, wx_ref, ba, bx, lam, cfg):
    w = xc.shape[1]
    bd = w // cfg.lru_blocks
    xb = xc.astype(BF16)
    r_parts, i_parts = [], []
    for k in range(cfg.lru_blocks):
        xk = xb[:, k * bd:(k + 1) * bd]
        r_parts.append(jnp.dot(xk, wa_ref[k], preferred_element_type=F32))
        i_parts.append(jnp.dot(xk, wx_ref[k], preferred_element_type=F32))
    r = jax.nn.sigmoid(jnp.concatenate(r_parts, axis=1) + ba)
    gate_i = jax.nn.sigmoid(jnp.concatenate(i_parts, axis=1) + bx)
    log_a = (-cfg.lru_c * r) * _softplus(-lam)
    a = jnp.exp(log_a)
    u = jnp.sqrt(-jnp.tanh(log_a) * (a * a + 1.0)) * (gate_i * xc)
    return a, u


def _lru_prompt_body(xl_ref, cw_ref, cb_ref, wa_ref, wx_ref, ba_ref, bx_ref, lam_ref,
                     h_ref, hlast_ref, prev_ref, carry_ref, *, cfg):
    j = pl.program_id(1)
    tb = xl_ref.shape[0]
    cw = cfg.conv_width

    @pl.when(j == 0)
    def _():
        prev_ref[...] = jnp.zeros(prev_ref.shape, F32)
        carry_ref[...] = jnp.zeros(carry_ref.shape, F32)

    xl = xl_ref[...]
    ext = jnp.concatenate([prev_ref[...], xl], axis=0)
    off = SUBLANE - (cw - 1)
    xc = cb_ref[...] + ext[off:off + tb] * cw_ref[0:1, :]
    for t in range(1, cw):
        xc = xc + ext[off + t:off + t + tb] * cw_ref[t:t + 1, :]
    prev_ref[...] = xl[tb - SUBLANE:, :]

    a, u = _lru_gates(xc, wa_ref, wx_ref, ba_ref[...], bx_ref[...], lam_ref[...], cfg)

    row = lax.broadcasted_iota(I32, a.shape, 0) & (SUBLANE - 1)
    s = 1
    while s < SUBLANE:
        ok = row >= s
        a_sh = jnp.where(ok, pltpu.roll(a, s, axis=0), 1.0)
        u_sh = jnp.where(ok, pltpu.roll(u, s, axis=0), 0.0)
        u = u + a * u_sh
        a = a * a_sh
        s *= 2
    h_prev = carry_ref[...]
    for gi in range(tb // SUBLANE):
        sl = slice(gi * SUBLANE, (gi + 1) * SUBLANE)
        h_rows = u[sl] + a[sl] * h_prev
        h_ref[sl, :] = h_rows.astype(h_ref.dtype)
        h_prev = h_rows[SUBLANE - 1:SUBLANE, :]
    carry_ref[...] = h_prev
    hlast_ref[0] = h_prev


def _lru_prompt(z_b, conv_w, conv_b, wa, wx, ba, bx, lam, n_seq, seq, cfg):
    w = conv_w.shape[1]
    tb = _pick(seq, (256, 128))
    nb = seq // tb
    bd = w // cfg.lru_blocks
    vec = pl.BlockSpec((1, w), lambda n, j: (0, 0))
    return pl.pallas_call(
        functools.partial(_lru_prompt_body, cfg=cfg),
        grid=(n_seq, nb),
        in_specs=[pl.BlockSpec((tb, w), lambda n, j: (n * nb + j, 0)),
                  pl.BlockSpec((cfg.conv_width, w), lambda n, j: (0, 0)),
                  vec,
                  pl.BlockSpec((cfg.lru_blocks, bd, bd), lambda n, j: (0, 0, 0)),
                  pl.BlockSpec((cfg.lru_blocks, bd, bd), lambda n, j: (0, 0, 0)),
                  vec, vec, vec],
        out_specs=[pl.BlockSpec((tb, w), lambda n, j: (n * nb + j, 0)),
                   pl.BlockSpec((1, 1, w), lambda n, j: (n, 0, 0))],
        out_shape=[jax.ShapeDtypeStruct((n_seq * seq, w), BF16),
                   jax.ShapeDtypeStruct((n_seq, 1, w), F32)],
        scratch_shapes=[pltpu.VMEM((SUBLANE, w), F32), pltpu.VMEM((1, w), F32)],
        compiler_params=_params("arbitrary", "arbitrary"),
        name="lru_prompt",
    )(z_b, conv_w, conv_b, wa, wx, ba, bx, lam)


def _lru_sample_body(xl_ref, hist_ref, h0_ref, cw_ref, cb_ref, wa_ref, wx_ref, ba_ref, bx_ref, lam_ref,
                     h_ref, hbf_ref, *, cfg):
    cw = cfg.conv_width
    xc = cb_ref[...] + hist_ref[0] * cw_ref[0:1, :]
    for t in range(1, cw - 1):
        xc = xc + hist_ref[t] * cw_ref[t:t + 1, :]
    xc = xc + xl_ref[...] * cw_ref[cw - 1:cw, :]
    a, u = _lru_gates(xc, wa_ref, wx_ref, ba_ref[...], bx_ref[...], lam_ref[...], cfg)
    h = a * h0_ref[...] + u
    h_ref[...] = h
    hbf_ref[...] = h.astype(BF16)


def _lru_sample(xl, hist, h0, conv_w, conv_b, wa, wx, ba, bx, lam, cfg):
    n, w = xl.shape
    return pl.pallas_call(
        functools.partial(_lru_sample_body, cfg=cfg),
        out_shape=[jax.ShapeDtypeStruct((n, w), F32), jax.ShapeDtypeStruct((n, w), BF16)],
        compiler_params=pltpu.CompilerParams(vmem_limit_bytes=VMEM_LIMIT),
        name="lru_sample",
    )(xl, hist, h0, conv_w, conv_b, wa, wx, ba, bx, lam)


def _sample_scores_body(pt_ref, qi_ref, w_ref, kin_ref, cos_ref, sin_ref, *rest, cfg, pages_per_step):
    page_refs, o_ref = rest[:pages_per_step], rest[pages_per_step]
    j = pl.program_id(1)
    nj = pl.num_programs(1)
    cosf, sinf = cos_ref[0], sin_ref[0]
    qi = _rope(qi_ref[0], cosf, sinf)
    qi_b = qi.astype(BF16)
    w = w_ref[0] * F32((cfg.idx_dim * cfg.idx_heads) ** -0.5)
    nt = (((1,), (1,)), ((), ()))
    rows = []
    for p in range(pages_per_step):
        z = lax.dot_general(qi_b, page_refs[p][0].astype(BF16), nt, preferred_element_type=F32)
        rows.append(jnp.sum(jnp.maximum(z, 0.0) * w, axis=0, keepdims=True))
    r0 = pl.multiple_of(j * pages_per_step, pages_per_step)
    o_ref[0, pl.ds(r0, pages_per_step), :] = jnp.concatenate(rows, axis=0)

    @pl.when(j == nj - 1)
    def _():
        z_new = jnp.sum(qi * kin_ref[0], axis=1, keepdims=True)
        s_new = jnp.sum(jnp.maximum(z_new, 0.0) * w, axis=0, keepdims=True)
        lane = lax.broadcasted_iota(I32, (SUBLANE, LANE), 1)
        sub = lax.broadcasted_iota(I32, (SUBLANE, LANE), 0)
        tail = jnp.where((lane == 0) & (sub == 0), jnp.broadcast_to(s_new, (SUBLANE, LANE)), -jnp.inf)
        o_ref[0, pl.ds(nj * pages_per_step, SUBLANE), :] = tail


def _sample_scores(page_table, qi, w_rep, ki_new, cos_s, sin_s, cache_idx_k, cfg):
    n, n_pages = page_table.shape
    pps = _pick(n_pages, (2 * SUBLANE, SUBLANE))
    assert cfg.page_size == LANE
    hi, di = cfg.idx_heads, cfg.idx_dim
    per_seq = lambda b, j, pt: (b, 0, 0)
    page_specs = [pl.BlockSpec((1, cfg.page_size, di),
                               functools.partial(lambda b, j, pt, p: (pt[b, j * pps + p], 0, 0), p=p))
                  for p in range(pps)]
    grid_spec = pltpu.PrefetchScalarGridSpec(
        num_scalar_prefetch=1,
        grid=(n, n_pages // pps),
        in_specs=[pl.BlockSpec((1, hi, di), per_seq),
                  pl.BlockSpec((1, hi, LANE), per_seq),
                  pl.BlockSpec((1, 1, di), per_seq),
                  pl.BlockSpec((1, 1, di), per_seq),
                  pl.BlockSpec((1, 1, di), per_seq)] + page_specs,
        out_specs=pl.BlockSpec((1, n_pages + SUBLANE, LANE), per_seq),
    )
    return pl.pallas_call(
        functools.partial(_sample_scores_body, cfg=cfg, pages_per_step=pps),
        grid_spec=grid_spec,
        out_shape=jax.ShapeDtypeStruct((n, n_pages + SUBLANE, LANE), F32),
        compiler_params=_params("arbitrary", "arbitrary"),
        name="sample_scores",
    )(page_table, qi, w_rep, ki_new, cos_s, sin_s, *([cache_idx_k] * pps))


def _row_major_rank(flag, upper, lower):
    fb = flag.astype(BF16)
    in_row = jnp.dot(fb, upper, preferred_element_type=F32)
    row_tot = jnp.sum(flag, axis=1, keepdims=True)
    before = jnp.dot(lower, jnp.broadcast_to(row_tot, flag.shape).astype(BF16), preferred_element_type=F32)
    return before + in_row


def _sample_attend_body(pt_ref, s_ref, q_ref, kvn_ref, cos_ref, sin_ref, cache_hbm, o_ref,
                        lst_v, lst_s, buf_ref, lst_sem, sem, *, cfg, topk):
    b = pl.program_id(0)
    nh, d = cfg.n_heads, cfg.head_dim
    c_heads = cfg.n_kv_heads
    g_heads = nh // c_heads
    per = 2 * c_heads
    ps = cfg.page_size
    n_rows = s_ref.shape[1]
    n_pages = n_rows - SUBLANE
    scale = F32(d ** -0.5)
    head_c = lax.broadcasted_iota(I32, (nh, 1), 0) // g_heads

    score = s_ref[0]
    keys = _sort_key(score)

    def count_ge(cand):
        hit = jnp.where(keys >= cand, 1, 0).astype(I32)
        return jnp.sum(jnp.sum(hit, axis=0, keepdims=True), axis=1, keepdims=True)

    thr = _kth_largest_key(count_ge, topk, 1)
    live = score > -jnp.inf
    above = jnp.where((keys > thr) & live, 1.0, 0.0)
    tied = jnp.where((keys == thr) & live, 1.0, 0.0)
    upper = jnp.where(lax.broadcasted_iota(I32, (LANE, LANE), 0) < lax.broadcasted_iota(I32, (LANE, LANE), 1),
                      1.0, 0.0).astype(BF16)
    lower = jnp.where(lax.broadcasted_iota(I32, (n_rows, n_rows), 1) < lax.broadcasted_iota(I32, (n_rows, n_rows), 0),
                      1.0, 0.0).astype(BF16)
    n_above = jnp.sum(jnp.sum(above, axis=0, keepdims=True), axis=1, keepdims=True)
    keep = above + tied * jnp.where(_row_major_rank(tied, upper, lower) < F32(topk) - n_above, 1.0, 0.0)
    row = lax.broadcasted_iota(I32, (n_rows, LANE), 0)
    keep_new = jnp.sum(jnp.sum(jnp.where(row == n_pages, keep, 0.0), axis=0, keepdims=True),
                       axis=1, keepdims=True)
    keep = jnp.where(row < n_pages, keep, 0.0)

    rank = _row_major_rank(keep, upper, lower)
    n_sel = jnp.sum(jnp.sum(keep, axis=0, keepdims=True), axis=1, keepdims=True)
    pad_rows = jnp.zeros((LANE - n_rows, LANE), F32)
    rank_t = jnp.concatenate([rank, pad_rows], axis=0).T
    keep_t = jnp.concatenate([keep, pad_rows], axis=0).T
    list_pos = lax.broadcasted_iota(I32, (ps, topk), 1).astype(F32)
    slot_and_one = jnp.where(lax.broadcasted_iota(I32, (SUBLANE, ps), 0) == 0,
                             lax.broadcasted_iota(I32, (SUBLANE, ps), 1).astype(F32), 1.0).astype(BF16)
    pages = jnp.zeros((1, topk), F32)
    slots = jnp.zeros((1, topk), F32)
    for p in range(n_pages):
        here = jnp.where((rank_t[:, p:p + 1] == list_pos) & (keep_t[:, p:p + 1] > 0.5), 1.0, 0.0).astype(BF16)
        hit = jnp.dot(slot_and_one, here, preferred_element_type=F32)
        slots = slots + hit[0:1, :]
        pages = pages + F32(p) * hit[1:2, :]
    lst_v[...] = jnp.concatenate([pages, slots, jnp.zeros((SUBLANE - 2, topk), F32)], axis=0).astype(I32)
    cp = pltpu.make_async_copy(lst_v, lst_s, lst_sem)
    cp.start()
    cp.wait()

    def fetch(g, carry):
        for u in range(DMA_THREADS):
            r = g * DMA_THREADS + u
            src = pl.multiple_of((pt_ref[b, lst_s[0, r]] * ps + lst_s[1, r]) * per, per)
            pltpu.make_async_copy(cache_hbm.at[pl.ds(src, per)],
                                  buf_ref.at[pl.ds(pl.multiple_of(r * per, per), per)], sem).start(priority=u)
        return carry
    lax.fori_loop(0, topk // DMA_THREADS, fetch, 0)

    q_f = _rope(q_ref[0], cos_ref[0], sin_ref[0])
    kvn = kvn_ref[0]
    k_new = jnp.zeros((nh, d), F32)
    v_new = jnp.zeros((nh, d), F32)
    for c in range(c_heads):
        k_new = jnp.where(head_c == c, kvn[c:c + 1, :], k_new)
        v_new = jnp.where(head_c == c, kvn[c_heads + c:c_heads + c + 1, :], v_new)
    s_new = jnp.where(keep_new > 0.5, jnp.sum(q_f * k_new, axis=1, keepdims=True) * scale, MASKED)

    pltpu.make_async_copy(buf_ref, buf_ref, sem).wait()

    width = topk * per
    rows = buf_ref[...].astype(BF16)
    col = lax.broadcasted_iota(I32, (nh, width), 1)
    mine = ((col % per) == head_c) & ((col // per).astype(F32) < n_sel)
    nt = (((1,), (1,)), ((), ()))
    s = jnp.where(mine, lax.dot_general(q_f.astype(BF16), rows, nt, preferred_element_type=F32) * scale, MASKED)
    m = jnp.maximum(jnp.max(s, axis=1, keepdims=True), s_new)
    p_un = jnp.exp(s - m)
    p_new = jnp.exp(s_new - m)
    den = jnp.sum(p_un, axis=1, keepdims=True) + p_new
    pv = jnp.dot(pltpu.roll(p_un, c_heads, axis=1).astype(BF16), rows, preferred_element_type=F32)
    o_ref[0] = ((pv + p_new * v_new) / den).astype(o_ref.dtype)


def _sample_attend(page_table, scores, q, kv_new, cos_s, sin_s, cache_kv, cfg):
    n, n_pages = page_table.shape
    nh, d = cfg.n_heads, cfg.head_dim
    per = 2 * cfg.n_kv_heads
    topk = min(cfg.topk_max, (n_pages * cfg.page_size + 1) // 4)
    n_rows = scores.shape[1]
    assert n_rows <= LANE and cfg.page_size == LANE and per == SUBLANE and topk % DMA_THREADS == 0
    per_seq = lambda b, pt: (b, 0, 0)
    grid_spec = pltpu.PrefetchScalarGridSpec(
        num_scalar_prefetch=1,
        grid=(n,),
        in_specs=[pl.BlockSpec((1, n_rows, LANE), per_seq),
                  pl.BlockSpec((1, nh, d), per_seq),
                  pl.BlockSpec((1, per, d), per_seq),
                  pl.BlockSpec((1, 1, d), per_seq),
                  pl.BlockSpec((1, 1, d), per_seq),
                  pl.BlockSpec(memory_space=pl.ANY)],
        out_specs=pl.BlockSpec((1, nh, d), per_seq),
        scratch_shapes=[pltpu.VMEM((SUBLANE, topk), I32),
                        pltpu.SMEM((SUBLANE, topk), I32),
                        pltpu.VMEM((topk * per, d), F32),
                        pltpu.SemaphoreType.DMA(()), pltpu.SemaphoreType.DMA(())],
    )
    return pl.pallas_call(
        functools.partial(_sample_attend_body, cfg=cfg, topk=topk),
        grid_spec=grid_spec,
        out_shape=jax.ShapeDtypeStruct((n, nh, d), BF16),
        compiler_params=_params("arbitrary"),
        name="sample_attend",
    )(page_table, scores, q, kv_new, cos_s, sin_s, cache_kv)


def _merge_body(oa_ref, hl_ref, woa_ref, wol_ref, ga_ref, gb_ref, o_ref):
    ya = jnp.dot(oa_ref[...], woa_ref[...], preferred_element_type=F32)
    yl = jnp.dot(hl_ref[...], wol_ref[...], preferred_element_type=F32)
    o_ref[...] = (jax.nn.sigmoid(ga_ref[...]) * ya + jax.nn.sigmoid(gb_ref[...]) * yl).astype(o_ref.dtype)


def _merge(o_attn, h_lru, w_oa, w_ol, z_b, lru_w, cfg):
    m, aw = o_attn.shape
    dm = w_oa.shape[1]
    tm = _pick(m, (1408, 1024, 768, 640, 512, 256))
    tn = _pick(dm, (512, 256, 128))
    assert lru_w % tn == 0
    ga_off = lru_w // tn
    gb_off = (lru_w + dm) // tn
    return pl.pallas_call(
        _merge_body,
        grid=(m // tm, dm // tn),
        in_specs=[pl.BlockSpec((tm, aw), lambda i, j: (i, 0)),
                  pl.BlockSpec((tm, lru_w), lambda i, j: (i, 0)),
                  pl.BlockSpec((aw, tn), lambda i, j: (0, j)),
                  pl.BlockSpec((lru_w, tn), lambda i, j: (0, j)),
                  pl.BlockSpec((tm, tn), lambda i, j: (i, ga_off + j)),
                  pl.BlockSpec((tm, tn), lambda i, j: (i, gb_off + j))],
        out_specs=pl.BlockSpec((tm, tn), lambda i, j: (i, j)),
        out_shape=jax.ShapeDtypeStruct((m, dm), BF16),
        compiler_params=_params("arbitrary", "arbitrary"),
        name="merge_mixers",
    )(o_attn, h_lru, w_oa, w_ol, z_b, z_b)


def _proj_ln_body(a_ref, w_ref, x_ref, g_ref, b_ref, o_ref, obf_ref, opk_ref, *, alpha, eps, tn):
    j = pl.program_id(1)
    c0 = pl.multiple_of(j * tn, tn)
    y = jnp.dot(a_ref[...], w_ref[...], preferred_element_type=F32)
    o_ref[:, pl.ds(c0, tn)] = F32(alpha) * x_ref[...] + y

    @pl.when(j == pl.num_programs(1) - 1)
    def _():
        out = _layer_norm(o_ref[...], g_ref[...], b_ref[...], eps)
        o_ref[...] = out
        obf_ref[...] = out.astype(BF16)
        opk_ref[...] = _pack_halves(out)


def _proj_residual_ln(a, w, x, g, b, alpha, cfg):
    m, k = a.shape
    dm = w.shape[1]
    tm = _pick(m, (384, 256))
    tn = _pick(dm, (512, 256, 128))
    return pl.pallas_call(
        functools.partial(_proj_ln_body, alpha=alpha, eps=cfg.ln_eps, tn=tn),
        grid=(m // tm, dm // tn),
        in_specs=[pl.BlockSpec((tm, k), lambda i, j: (i, 0)),
                  pl.BlockSpec((k, tn), lambda i, j: (0, j)),
                  pl.BlockSpec((tm, tn), lambda i, j: (i, j)),
                  pl.BlockSpec((1, dm), lambda i, j: (0, 0)),
                  pl.BlockSpec((1, dm), lambda i, j: (0, 0))],
        out_specs=[pl.BlockSpec((tm, dm), lambda i, j: (i, 0)),
                   pl.BlockSpec((tm, dm), lambda i, j: (i, 0)),
                   pl.BlockSpec((tm, dm // 2), lambda i, j: (i, 0))],
        out_shape=[jax.ShapeDtypeStruct((m, dm), F32), jax.ShapeDtypeStruct((m, dm), BF16),
                   jax.ShapeDtypeStruct((m, dm // 2), U32)],
        compiler_params=_params("arbitrary", "arbitrary"),
        name="proj_residual_ln",
    )(a, w, x, g, b)


def _first_index_of_max(v, idx, big):
    m = jnp.max(v, axis=0, keepdims=True)
    first = jnp.min(jnp.where(v == m, idx, big), axis=0, keepdims=True)
    return m, first


def _router_body(x_ref, wr_ref, br_ref, ids_ref, gate_ref, rank_ref, cnt_ref, *, cfg, n_tok):
    i = pl.program_id(0)
    e, ng = cfg.n_experts, cfg.n_groups
    per = e // ng
    tm = x_ref.shape[0]
    nt = (((1,), (1,)), ((), ()))
    logits = lax.dot_general(wr_ref[...], x_ref[...], nt, preferred_element_type=F32)
    s = jax.nn.sigmoid(logits)
    choice = s + br_ref[...]
    eidx = lax.broadcasted_iota(I32, (e, tm), 0)

    grp_rows = []
    jidx = lax.broadcasted_iota(I32, (per, tm), 0)
    for g in range(ng):
        cg = choice[g * per:(g + 1) * per, :]
        m1, j1 = _first_index_of_max(cg, jidx, per)
        m2 = jnp.max(jnp.where(jidx == j1, -jnp.inf, cg), axis=0, keepdims=True)
        grp_rows.append(m1 + m2)
    grp = jnp.concatenate(grp_rows, axis=0)

    gidx = lax.broadcasted_iota(I32, (ng, tm), 0)
    grp_keep = jnp.zeros((ng, tm), F32)
    work = grp
    for _ in range(cfg.topk_groups):
        _, gsel = _first_index_of_max(work, gidx, ng)
        hit = gidx == gsel
        grp_keep = jnp.where(hit, 1.0, grp_keep)
        work = jnp.where(hit, -jnp.inf, work)

    keep_rows = [jnp.broadcast_to(grp_keep[g:g + 1, :], (per, tm)) for g in range(ng)]
    masked = jnp.where(jnp.concatenate(keep_rows, axis=0) > 0.5, choice, -jnp.inf)

    ids, wts, hits = [], [], []
    for _ in range(cfg.top_k):
        _, esel = _first_index_of_max(masked, eidx, e)
        hit = eidx == esel
        ids.append(esel)
        hits.append(hit)
        wts.append(jnp.sum(jnp.where(hit, s, 0.0), axis=0, keepdims=True))
        masked = jnp.where(hit, -jnp.inf, masked)
    wk = jnp.concatenate(wts, axis=0)
    ids_ref[...] = jnp.concatenate(ids, axis=0)
    gate_ref[...] = F32(cfg.route_scale) * wk / jnp.sum(wk, axis=0, keepdims=True)

    @pl.when(i == 0)
    def _():
        cnt_ref[...] = jnp.zeros(cnt_ref.shape, F32)

    tok = i * tm + lax.broadcasted_iota(I32, (e, tm), 1)
    picked = jnp.zeros((e, tm), F32)
    for h in hits:
        picked = jnp.where(h, 1.0, picked)
    picked = jnp.where(tok < n_tok, picked, 0.0)
    before = (lax.broadcasted_iota(I32, (tm, tm), 0) < lax.broadcasted_iota(I32, (tm, tm), 1))
    earlier = jnp.dot(picked.astype(BF16), jnp.where(before, 1.0, 0.0).astype(BF16),
                      preferred_element_type=F32)
    rank_all = cnt_ref[:, 0:1] + earlier
    rank_ref[...] = jnp.concatenate(
        [jnp.sum(jnp.where(h, rank_all, 0.0), axis=0, keepdims=True) for h in hits], axis=0).astype(I32)
    cnt_ref[...] = cnt_ref[...] + jnp.sum(picked, axis=1, keepdims=True)


def _router(x_bf, w_r_t, b_r, n_tok, cfg):
    m, dm = x_bf.shape
    tm = _pick(m, (256, 128))
    e = cfg.n_experts
    pick = pl.BlockSpec((cfg.top_k, tm), lambda i: (0, i))
    return pl.pallas_call(
        functools.partial(_router_body, cfg=cfg, n_tok=n_tok),
        grid=(m // tm,),
        in_specs=[pl.BlockSpec((tm, dm), lambda i: (i, 0)),
                  pl.BlockSpec((e, dm), lambda i: (0, 0)),
                  pl.BlockSpec((e, 1), lambda i: (0, 0))],
        out_specs=[pick, pick, pick, pl.BlockSpec((e, LANE), lambda i: (0, 0))],
        out_shape=[jax.ShapeDtypeStruct((cfg.top_k, m), I32),
                   jax.ShapeDtypeStruct((cfg.top_k, m), F32),
                   jax.ShapeDtypeStruct((cfg.top_k, m), I32),
                   jax.ShapeDtypeStruct((e, LANE), F32)],
        compiler_params=_params("arbitrary"),
        name="router",
    )(x_bf, w_r_t, b_r)


DMA_THREADS = 2


def _row_pitch(sub):
    return -(-sub // SUBLANE) * SUBLANE + SUBLANE


def _start_row_gather(idx_row, idx_smem, idx_sem, src_hbm, buf, sem, n, sub):
    cp = pltpu.make_async_copy(idx_row, idx_smem, idx_sem)
    cp.start()
    cp.wait()

    def start(g, carry):
        for u in range(DMA_THREADS):
            r = g * DMA_THREADS + u
            r0 = pl.multiple_of(r * _row_pitch(sub), SUBLANE)
            pltpu.make_async_copy(src_hbm.at[idx_smem[0, r]], buf.at[pl.ds(r0, sub)], sem).start(priority=u)
        return carry
    lax.fori_loop(0, n // DMA_THREADS, start, 0)


def _wait_row_gather(buf, sem, n, sub):
    pltpu.make_async_copy(buf.at[pl.ds(0, n * sub)], buf.at[pl.ds(0, n * sub)], sem).wait()


def _gathered_slab(buf, first, count, j, sub):
    return buf[pl.ds(first * _row_pitch(sub) + j, count, stride=_row_pitch(sub)), :]


def _dispatch_body(nrows_ref, src_ref, x_hbm, o_ref, idx_smem, buf_ref, idx_sem, sems, *, tile):
    i = pl.program_id(0)
    sub = x_hbm.shape[1]
    n_valid = (nrows_ref[0] + tile - 1) // tile

    def start(t):
        slot = t % 2
        _start_row_gather(src_ref.at[t], idx_smem.at[pl.ds(slot, 1)], idx_sem, x_hbm,
                          buf_ref.at[slot], sems.at[slot], tile, sub)

    @pl.when(i == 0)
    def _():
        start(0)

    @pl.when(i + 1 < n_valid)
    def _():
        start(i + 1)

    @pl.when(i < n_valid)
    def _():
        slot = i % 2
        _wait_row_gather(buf_ref.at[slot], sems.at[slot], tile, sub)
        half = sub * LANE
        for j in range(sub):
            lo, hi = _unpack_halves(_gathered_slab(buf_ref.at[slot], 0, tile, j, sub))
            o_ref[:, j * LANE:(j + 1) * LANE] = lo.astype(o_ref.dtype)
            o_ref[:, half + j * LANE:half + (j + 1) * LANE] = hi.astype(o_ref.dtype)


def _dispatch(x3, src_rows, n_rows, tile):
    r_max = src_rows.shape[0]
    sub = x3.shape[1]
    n_tiles = r_max // tile
    last = lambda i, nr: jnp.minimum(i, (nr[0] - 1) // tile)
    grid_spec = pltpu.PrefetchScalarGridSpec(
        num_scalar_prefetch=1,
        grid=(n_tiles,),
        in_specs=[pl.BlockSpec((n_tiles, 1, tile), lambda i, nr: (0, 0, 0)),
                  pl.BlockSpec(memory_space=pl.ANY)],
        out_specs=pl.BlockSpec((tile, 2 * sub * LANE), lambda i, nr: (last(i, nr), 0)),
        scratch_shapes=[pltpu.SMEM((2, tile), I32), pltpu.VMEM((2, tile * _row_pitch(sub), LANE), x3.dtype),
                        pltpu.SemaphoreType.DMA(()), pltpu.SemaphoreType.DMA((2,))],
    )
    return pl.pallas_call(
        functools.partial(_dispatch_body, tile=tile),
        grid_spec=grid_spec,
        out_shape=jax.ShapeDtypeStruct((r_max, 2 * sub * LANE), BF16),
        compiler_params=_params("arbitrary"),
        name="moe_dispatch",
    )(n_rows, src_rows.reshape(n_tiles, 1, tile), x3)


def _stream_expert_weights(te_ref, tf_ref, nx_ref, nt_ref, w_hbms, w_casts, wbuf, sems, cnt):
    p, m = pl.program_id(0), pl.program_id(1)
    width = w_casts[0].shape[1]

    def copies(expert, blk, slot):
        col = pl.multiple_of(blk * width, width)
        return [pltpu.make_async_copy(w.at[expert, :, pl.ds(col, width)], wbuf.at[slot, i], sems.at[slot])
                for i, w in enumerate(w_hbms)]

    @pl.when((p == 0) & (m == 0))
    def _():
        cnt[0] = 0
        for c in copies(te_ref[0], 0, 0):
            c.start()

    @pl.when((tf_ref[m] == 1) & (m < nt_ref[0]))
    def _():
        slot = cnt[0] % 2
        for c in copies(te_ref[m], p, slot):
            c.wait()
        nxt = nx_ref[m]

        @pl.when(nxt >= 0)
        def _():
            for c in copies(nxt, p, 1 - slot):
                c.start()

        @pl.when((nxt < 0) & (p + 1 < pl.num_programs(0)))
        def _():
            for c in copies(te_ref[0], p + 1, 1 - slot):
                c.start()

        for i, dst in enumerate(w_casts):
            dst[...] = wbuf[slot, i].astype(BF16)
        cnt[0] = cnt[0] + 1


def _expert_up_body(te_ref, tf_ref, nx_ref, nt_ref, x_ref, wg_hbm, wu_hbm, o_ref, wg_s, wu_s, wbuf, sems, cnt):
    _stream_expert_weights(te_ref, tf_ref, nx_ref, nt_ref, (wg_hbm, wu_hbm), (wg_s, wu_s), wbuf, sems, cnt)

    @pl.when(pl.program_id(1) < nt_ref[0])
    def _():
        x = x_ref[...]
        g = jnp.dot(x, wg_s[...], preferred_element_type=F32)
        u = jnp.dot(x, wu_s[...], preferred_element_type=F32)
        o_ref[...] = (jax.nn.silu(g) * u).astype(o_ref.dtype)


def _expert_up(xs, w_gate, w_up, tile_expert, tile_first, tile_next, n_tiles, tile):
    r_max, dm = xs.shape
    e, _, f = w_gate.shape
    tf = _pick(f, (512, 256, 128))
    n_mt = r_max // tile
    clamp = lambda m, nt: jnp.minimum(m, nt[0] - 1)
    grid_spec = pltpu.PrefetchScalarGridSpec(
        num_scalar_prefetch=4,
        grid=(f // tf, n_mt),
        in_specs=[pl.BlockSpec((tile, dm), lambda fi, m, te, tfst, nx, nt: (clamp(m, nt), 0)),
                  pl.BlockSpec(memory_space=pl.ANY),
                  pl.BlockSpec(memory_space=pl.ANY)],
        out_specs=pl.BlockSpec((tile, tf), lambda fi, m, te, tfst, nx, nt: (clamp(m, nt), fi)),
        scratch_shapes=[pltpu.VMEM((dm, tf), BF16), pltpu.VMEM((dm, tf), BF16),
                        pltpu.VMEM((2, 2, dm, tf), F32), pltpu.SemaphoreType.DMA((2,)),
                        pltpu.SMEM((1,), I32)],
    )
    return pl.pallas_call(
        _expert_up_body,
        grid_spec=grid_spec,
        out_shape=jax.ShapeDtypeStruct((r_max, f), BF16),
        compiler_params=_params("arbitrary", "arbitrary"),
        name="expert_up",
    )(tile_expert, tile_first, tile_next, n_tiles, xs, w_gate, w_up)


def _expert_down_body(te_ref, tf_ref, nx_ref, nt_ref, h_ref, wd_hbm, o_ref, wd_s, wbuf, sems, cnt):
    _stream_expert_weights(te_ref, tf_ref, nx_ref, nt_ref, (wd_hbm,), (wd_s,), wbuf, sems, cnt)

    @pl.when(pl.program_id(1) < nt_ref[0])
    def _():
        o_ref[...] = jnp.dot(h_ref[...], wd_s[...], preferred_element_type=F32)


def _expert_down(hid, w_down, tile_expert, tile_first, tile_next, n_tiles, tile):
    r_max, f = hid.shape
    dm = w_down.shape[2]
    tn = _pick(dm, (2048, 1024, 512, 256, 128))
    n_mt = r_max // tile
    clamp = lambda m, nt: jnp.minimum(m, nt[0] - 1)
    grid_spec = pltpu.PrefetchScalarGridSpec(
        num_scalar_prefetch=4,
        grid=(dm // tn, n_mt),
        in_specs=[pl.BlockSpec((tile, f), lambda ni, m, te, tfst, nx, nt: (clamp(m, nt), 0)),
                  pl.BlockSpec(memory_space=pl.ANY)],
        out_specs=pl.BlockSpec((tile, tn), lambda ni, m, te, tfst, nx, nt: (clamp(m, nt), ni)),
        scratch_shapes=[pltpu.VMEM((f, tn), BF16), pltpu.VMEM((2, 1, f, tn), F32),
                        pltpu.SemaphoreType.DMA((2,)), pltpu.SMEM((1,), I32)],
    )
    return pl.pallas_call(
        _expert_down_body,
        grid_spec=grid_spec,
        out_shape=jax.ShapeDtypeStruct((r_max, dm), F32),
        compiler_params=_params("arbitrary", "arbitrary"),
        name="expert_down",
    )(tile_expert, tile_first, tile_next, n_tiles, hid, w_down)


def _shared_up_body(x_ref, wg_ref, wu_ref, o_ref):
    x = x_ref[...]
    g = jnp.dot(x, wg_ref[...], preferred_element_type=F32)
    u = jnp.dot(x, wu_ref[...], preferred_element_type=F32)
    o_ref[...] = (jax.nn.silu(g) * u).astype(o_ref.dtype)


def _shared_up(x_bf, wg, wu):
    m, dm = x_bf.shape
    f = wg.shape[1]
    tm = _pick(m, (1408, 1024, 768, 640, 512, 256))
    tf = _pick(f, (256, 128))
    return pl.pallas_call(
        _shared_up_body,
        grid=(m // tm, f // tf),
        in_specs=[pl.BlockSpec((tm, dm), lambda i, j: (i, 0)),
                  pl.BlockSpec((dm, tf), lambda i, j: (0, j)),
                  pl.BlockSpec((dm, tf), lambda i, j: (0, j))],
        out_specs=pl.BlockSpec((tm, tf), lambda i, j: (i, j)),
        out_shape=jax.ShapeDtypeStruct((m, f), BF16),
        compiler_params=_params("arbitrary", "arbitrary"),
        name="shared_up",
    )(x_bf, wg, wu)


def _combine_body(pos_ref, gate_ref, ys_hbm, x_ref, sh_ref, wsd_ref, g_ref, b_ref, o_ref,
                  idx_smem, buf_ref, idx_sem, sems, *, tb, top_k, alpha, eps):
    i = pl.program_id(0)
    n = tb * top_k
    dm = x_ref.shape[1]

    def start(t):
        slot = t % 2
        cp = pltpu.make_async_copy(pos_ref.at[t], idx_smem.at[pl.ds(slot, 1)], idx_sem)
        cp.start()
        cp.wait()

        def issue(g, carry):
            for u in range(DMA_THREADS):
                r = g * DMA_THREADS + u
                pltpu.make_async_copy(ys_hbm.at[pl.ds(idx_smem[slot, r], 1), :],
                                      buf_ref.at[slot, pl.ds(r, 1), :], sems.at[slot]).start(priority=u)
            return carry
        lax.fori_loop(0, n // DMA_THREADS, issue, 0)

    @pl.when(i == 0)
    def _():
        start(0)

    @pl.when(i + 1 < pl.num_programs(0))
    def _():
        start(i + 1)

    shared = jnp.dot(sh_ref[...], wsd_ref[...], preferred_element_type=F32)
    slot = i % 2
    pltpu.make_async_copy(buf_ref.at[slot], buf_ref.at[slot], sems.at[slot]).wait()
    routed = shared
    for k in range(top_k):
        gate = jnp.concatenate([gate_ref[k * tb:(k + 1) * tb, :]] * (dm // LANE), axis=1)
        routed = routed + gate * buf_ref[slot, k * tb:(k + 1) * tb, :]
    o_ref[...] = _layer_norm(F32(alpha) * x_ref[...] + routed, g_ref[...], b_ref[...], eps)


def _combine(pos, gate_rows, ys, x, sh_hid, ws_down, g, b, alpha, tb, cfg):
    m, dm = x.shape
    f = sh_hid.shape[1]
    n = tb * cfg.top_k
    nblk = m // tb
    return pl.pallas_call(
        functools.partial(_combine_body, tb=tb, top_k=cfg.top_k, alpha=alpha, eps=cfg.ln_eps),
        grid=(nblk,),
        in_specs=[pl.BlockSpec((nblk, 1, n), lambda i: (0, 0, 0)),
                  pl.BlockSpec((n, LANE), lambda i: (i, 0)),
                  pl.BlockSpec(memory_space=pl.ANY),
                  pl.BlockSpec((tb, dm), lambda i: (i, 0)),
                  pl.BlockSpec((tb, f), lambda i: (i, 0)),
                  pl.BlockSpec((f, dm), lambda i: (0, 0)),
                  pl.BlockSpec((1, dm), lambda i: (0, 0)),
                  pl.BlockSpec((1, dm), lambda i: (0, 0))],
        out_specs=pl.BlockSpec((tb, dm), lambda i: (i, 0)),
        out_shape=jax.ShapeDtypeStruct((m, dm), F32),
        scratch_shapes=[pltpu.SMEM((2, n), I32), pltpu.VMEM((2, n, dm), F32),
                        pltpu.SemaphoreType.DMA(()), pltpu.SemaphoreType.DMA((2,))],
        compiler_params=_params("arbitrary"),
        name="moe_combine",
    )(pos, gate_rows, ys, x, sh_hid, ws_down, g, b)


def _routing_tables(ids, gates, ranks, counts, n_tok, tile, tb, cfg):
    e, k = cfg.n_experts, cfg.top_k
    t_all = ids.shape[1]
    r_max = ((n_tok * k + e * (tile - 1)) // tile + 1) * tile
    counts = counts[:, 0].astype(I32)
    padded = ((counts + tile - 1) // tile) * tile
    ends = jnp.cumsum(padded)
    start_pad = ends - padded
    eye = ids[:, :, None] == jnp.arange(e, dtype=I32)[None, None, :]
    dest = jnp.sum(jnp.where(eye, start_pad[None, None, :], 0), axis=2) + ranks
    tok = jnp.broadcast_to(jnp.arange(t_all, dtype=I32)[None, :], (k, t_all))
    real = tok < n_tok
    src_rows = jnp.zeros((r_max,), I32).at[jnp.where(real, dest, r_max).reshape(-1)].set(
        tok.reshape(-1), mode="drop", unique_indices=True)
    blocks = lambda a: a.reshape(k, t_all // tb, tb).transpose(1, 0, 2).reshape(t_all // tb, 1, k * tb)
    pos = blocks(jnp.where(real, dest, 0))
    gate_rows = jnp.broadcast_to(blocks(jnp.where(real, gates, 0.0)).reshape(-1, 1), (t_all * k, LANE))
    n_rows = ends[-1:]
    tile_start = jnp.arange(r_max // tile, dtype=I32) * tile
    tile_expert = jnp.minimum(jnp.sum((ends[None, :] <= tile_start[:, None]).astype(I32), axis=1), e - 1)
    prev = jnp.concatenate([jnp.full((1,), -1, I32), tile_expert[:-1]])
    tile_first = (tile_expert != prev).astype(I32)
    ar = jnp.arange(e, dtype=I32)
    later = jnp.where((padded > 0)[None, :] & (ar[None, :] > ar[:, None]), ar[None, :], e)
    next_expert = jnp.min(later, axis=1)
    next_expert = jnp.where(next_expert == e, -1, next_expert)
    tile_next = jnp.sum(jnp.where(tile_expert[:, None] == ar[None, :], next_expert[None, :], 0), axis=1)
    return src_rows, pos, gate_rows, n_rows, n_rows // tile, tile_expert, tile_first, tile_next


def _layer(cfg, l, x_all, n_p, seq, n_s, past, cache_kv, cache_idx_k, page_table, state_conv, state_h,
           w_in, ik_g, ik_b, conv_w, conv_b, lru_w_a, lru_b_a, lru_w_x, lru_b_x, lru_lambda,
           w_o_attn, w_o_lru, w_out, ln1_g, ln1_b, w_router, b_router,
           w_gate, w_up, w_down, ws_gate, ws_up, ws_down, ln2_g, ln2_b):
    t_all, dm = x_all.shape
    t_p = n_p * seq
    d, di = cfg.head_dim, cfg.idx_dim
    qw, kvw, iw = cfg.n_heads * d, cfg.n_kv_heads * d, cfg.idx_heads * di
    lw = conv_w.shape[1]
    alpha = (2.0 * cfg.depth) ** 0.25
    c_qkv = qw + 2 * kvw
    c_i = c_qkv + iw + di
    c_w = c_i + cfg.idx_heads

    x_bf = x_all.astype(BF16)
    w_in_bf = w_in.astype(BF16)
    z_qkv = _matmul(x_bf, w_in_bf, 0, c_qkv, name="in_proj_qkv")
    z_i = _matmul(x_bf, w_in_bf, c_qkv, iw, name="in_proj_idx_q")
    z_kw = _matmul(x_bf, w_in_bf, c_qkv + iw, 2 * LANE, name="in_proj_idx_kw")
    z_b = _matmul(x_bf, w_in_bf[:, c_w:], name="in_proj_lru_gates")

    cos_p, sin_p = _rope_tables(jnp.arange(seq), d, cfg.rope_theta)
    cos_1, sin_1 = _rope_tables(jnp.full((1,), past), d, cfg.rope_theta)
    cosf = jnp.concatenate([cos_p, jnp.broadcast_to(cos_1, (t_all - t_p, d))], axis=0)
    sinf = jnp.concatenate([sin_p, jnp.broadcast_to(sin_1, (t_all - t_p, d))], axis=0)
    kv_all, ki_all, k_bf, ki_bf = _finalize_keys(
        z_qkv, z_kw, cosf, sinf, ik_g.reshape(1, di), ik_b.reshape(1, di), n_p, seq, cfg)
    v_t = _transpose_values(z_qkv, t_p, cfg)

    o_attn_p = _prompt_attention(z_qkv, z_i, z_kw, cosf, sinf, k_bf, ki_bf, v_t, n_p, seq, cfg)
    wa_bf, wx_bf = lru_w_a.astype(BF16), lru_w_x.astype(BF16)
    row = lambda v: v.reshape(1, -1)
    h_lru_p, h_last_p = _lru_prompt(z_b, conv_w, row(conv_b), wa_bf, wx_bf, row(lru_b_a), row(lru_b_x),
                                    row(lru_lambda), n_p, seq, cfg)

    sl = slice(t_p, t_p + n_s)
    cos_s, sin_s = jnp.broadcast_to(cos_1[None], (n_s, 1, d)), jnp.broadcast_to(sin_1[None], (n_s, 1, d))
    qi_s = z_i[sl].reshape(n_s, cfg.idx_heads, di)
    w_rep = jnp.broadcast_to(z_kw[sl, di:di + cfg.idx_heads][:, :, None], (n_s, cfg.idx_heads, LANE))
    n_phys = cache_idx_k.shape[0]
    scores = _sample_scores(page_table, qi_s, w_rep, ki_all[sl][:, None, :], cos_s, sin_s, cache_idx_k, cfg)
    q_s = z_qkv[sl, :qw].reshape(n_s, cfg.n_heads, d)
    kv_new = kv_all[sl].reshape(n_s, 2 * cfg.n_kv_heads, d)
    o_attn_s = _sample_attend(page_table, scores, q_s, kv_new, cos_s, sin_s,
                              cache_kv.reshape(n_phys * cfg.page_size * 2 * cfg.n_kv_heads, d), cfg)
    xl_s = z_b[sl, :lw]
    hist_s = jnp.moveaxis(state_conv, 1, 0)
    h_s, h_s_bf = _lru_sample(xl_s, hist_s, state_h, conv_w, row(conv_b), wa_bf, wx_bf,
                              row(lru_b_a), row(lru_b_x), row(lru_lambda), cfg)

    pad = t_all - t_p - n_s
    o_attn = jnp.concatenate([o_attn_p, o_attn_s.reshape(n_s, qw), jnp.zeros((pad, qw), BF16)], axis=0)
    h_lru = jnp.concatenate([h_lru_p, h_s_bf, jnp.zeros((pad, lw), BF16)], axis=0)

    merged = _merge(o_attn, h_lru, w_o_attn.astype(BF16), w_o_lru.astype(BF16), z_b, lw, cfg)
    x1, x1_bf, x1_pk = _proj_residual_ln(merged, w_out.astype(BF16), x_all, row(ln1_g), row(ln1_b), alpha, cfg)

    n_tok = t_p + n_s
    ids, gates, ranks, counts = _router(x1_bf, w_router.T.astype(BF16), b_router.reshape(-1, 1), n_tok, cfg)
    tile = cfg.moe_tile
    tb = _pick(t_all, (64, 32, 16, 8))
    src_rows, pos, gate_rows, n_rows, n_tiles, tile_expert, tile_first, tile_next = _routing_tables(
        ids, gates, ranks, counts, n_tok, tile, tb, cfg)
    xs = _dispatch(x1_pk.reshape(t_all, dm // (2 * LANE), LANE), src_rows, n_rows, tile)
    hid = _expert_up(xs, w_gate, w_up, tile_expert, tile_first, tile_next, n_tiles, tile)
    ys = _expert_down(hid, w_down, tile_expert, tile_first, tile_next, n_tiles, tile)
    sh_hid = _shared_up(x1_bf, ws_gate.astype(BF16), ws_up.astype(BF16))
    y_all = _combine(pos, gate_rows, ys, x1, sh_hid, ws_down.astype(BF16), row(ln2_g), row(ln2_b),
                     alpha, tb, cfg)

    conv_p = z_b[:t_p, :lw].reshape(n_p, seq, lw)[:, seq - (cfg.conv_width - 1):, :]
    conv_s = jnp.concatenate([state_conv[:, 1:, :], xl_s[:, None, :]], axis=1)
    outs = dict(
        kv_p=kv_all[:t_p].reshape(n_p, seq, 2, cfg.n_kv_heads, d),
        ik_p=ki_all[:t_p].reshape(n_p, seq, di),
        cv_p=conv_p, h_p=h_last_p.reshape(n_p, lw),
        kv_s=kv_all[sl].reshape(n_s, 1, 2, cfg.n_kv_heads, d),
        ik_s=ki_all[sl].reshape(n_s, 1, di),
        cv_s=conv_s, h_s=h_s)
    return y_all, outs


def _forward(cfg, x_prompt, x_sample, cache_kv, cache_idx_k, page_table, state_conv, state_h,
             w_in, idx_k_norm_g, idx_k_norm_b, conv_w, conv_b, lru_w_a, lru_b_a, lru_w_x, lru_b_x,
             lru_lambda, w_o_attn, w_o_lru, w_out, ln1_g, ln1_b, w_router, b_router,
             w_gate, w_up, w_down, ws_gate, ws_up, ws_down, ln2_g, ln2_b):
    n_p, seq, dm = x_prompt.shape
    n_s, t_s, _ = x_sample.shape
    assert t_s == 1 and cfg.depth == 1 and w_in.shape[0] == 1
    past = page_table.shape[1] * cfg.page_size
    t_p = n_p * seq
    t_all = -(-(t_p + n_s) // cfg.row_align) * cfg.row_align
    pad = t_all - t_p - n_s
    x_all = jnp.concatenate([x_prompt.reshape(t_p, dm), x_sample.reshape(n_s, dm),
                             jnp.zeros((pad, dm), x_prompt.dtype)], axis=0)
    l = 0
    y_all, o = _layer(cfg, l, x_all, n_p, seq, n_s, past, cache_kv[l], cache_idx_k[l], page_table,
                      state_conv[l], state_h[l], w_in[l], idx_k_norm_g[l], idx_k_norm_b[l],
                      conv_w[l], conv_b[l], lru_w_a[l], lru_b_a[l], lru_w_x[l], lru_b_x[l], lru_lambda[l],
                      w_o_attn[l], w_o_lru[l], w_out[l], ln1_g[l], ln1_b[l], w_router[l], b_router[l],
                      w_gate[l], w_up[l], w_down[l], ws_gate[l], ws_up[l], ws_down[l], ln2_g[l], ln2_b[l])
    y_p = y_all[:t_p].reshape(n_p, seq, dm)
    y_s = y_all[t_p:t_p + n_s].reshape(n_s, 1, dm)
    lead = lambda a: a[None]
    return (y_p, y_s, lead(o["kv_p"]), lead(o["ik_p"]), lead(o["cv_p"]), lead(o["h_p"]),
            lead(o["kv_s"]), lead(o["ik_s"]), lead(o["cv_s"]), lead(o["h_s"]))


def kernel(x_prompt, x_sample, cache_kv, cache_idx_k, page_table, state_conv, state_h, w_in, idx_k_norm_g, idx_k_norm_b, conv_w, conv_b, lru_w_a, lru_b_a, lru_w_x, lru_b_x, lru_lambda, w_o_attn, w_o_lru, w_out, ln1_g, ln1_b, w_router, b_router, w_gate, w_up, w_down, ws_gate, ws_up, ws_down, ln2_g, ln2_b):
    return _forward(Cfg(), x_prompt, x_sample, cache_kv, cache_idx_k, page_table, state_conv, state_h,
                    w_in, idx_k_norm_g, idx_k_norm_b, conv_w, conv_b, lru_w_a, lru_b_a, lru_w_x, lru_b_x,
                    lru_lambda, w_o_attn, w_o_lru, w_out, ln1_g, ln1_b, w_router, b_router,
                    w_gate, w_up, w_down, ws_gate, ws_up, ws_down, ln2_g, ln2_b)
```

```python
import functools
from typing import NamedTuple

import jax
import jax.numpy as jnp
import numpy as np
from jax import lax
from jax.experimental import pallas as pl
from jax.experimental.pallas import tpu as pltpu

F32 = jnp.float32
BF16 = jnp.bfloat16
I32 = jnp.int32
U32 = jnp.uint32

LANE = 128
SUBLANE = 8
VMEM_LIMIT = 56 * 1024 * 1024
MASKED = -1e30
INT_MIN = -2 ** 31


class Cfg(NamedTuple):
    n_heads: int = 16
    n_kv_heads: int = 4
    head_dim: int = 128
    idx_heads: int = 32
    idx_dim: int = 128
    topk_max: int = 256
    q_block: int = 128
    rope_theta: float = 10000.0
    lru_blocks: int = 16
    conv_width: int = 4
    lru_c: float = 8.0
    n_experts: int = 64
    top_k: int = 8
    n_groups: int = 8
    topk_groups: int = 4
    route_scale: float = 2.5
    ln_eps: float = 1e-5
    page_size: int = 128
    depth: int = 1
    key_chunk: int = 512
    moe_tile: int = 512
    row_align: int = 256


def _pick(dim, prefs):
    for p in prefs:
        if p <= dim and dim % p == 0:
            return p
    return dim


def _params(*sem):
    return pltpu.CompilerParams(dimension_semantics=sem, vmem_limit_bytes=VMEM_LIMIT)


def _mm_body(x_ref, w_ref, o_ref):
    o_ref[...] = jnp.dot(x_ref[...], w_ref[...], preferred_element_type=F32).astype(o_ref.dtype)


def _matmul(x, w, col0=0, n=None, out_dtype=F32, name="matmul"):
    m, k = x.shape
    n = w.shape[1] - col0 if n is None else n
    tm = _pick(m, (1408, 1024, 768, 640, 512, 256))
    tn = _pick(n, (512, 384, 256, 128))
    assert col0 % tn == 0
    j0 = col0 // tn
    return pl.pallas_call(
        _mm_body,
        grid=(m // tm, n // tn),
        in_specs=[pl.BlockSpec((tm, k), lambda i, j: (i, 0)),
                  pl.BlockSpec((k, tn), lambda i, j: (0, j0 + j))],
        out_specs=pl.BlockSpec((tm, tn), lambda i, j: (i, j)),
        out_shape=jax.ShapeDtypeStruct((m, n), out_dtype),
        compiler_params=_params("arbitrary", "arbitrary"),
        name=name,
    )(x, w)


def _rope(x, cosf, sinf):
    return x * cosf + pltpu.roll(x, x.shape[-1] // 2, axis=x.ndim - 1) * sinf


def _rope_tables(pos, dim, theta):
    half = dim // 2
    inv = theta ** (-jnp.arange(half, dtype=F32) / half)
    ang = pos.astype(F32)[:, None] * inv[None, :]
    cos, sin = jnp.cos(ang), jnp.sin(ang)
    return jnp.concatenate([cos, cos], -1), jnp.concatenate([-sin, sin], -1)


def _pack_halves(y):
    w = y.shape[1] // 2
    lo = pltpu.bitcast(y[:, :w].astype(BF16).astype(F32), U32)
    hi = pltpu.bitcast(y[:, w:].astype(BF16).astype(F32), U32)
    return (hi & jnp.uint32(0xFFFF0000)) | (lo >> 16)


def _unpack_halves(p):
    lo = pltpu.bitcast(p << 16, F32)
    hi = pltpu.bitcast(p & jnp.uint32(0xFFFF0000), F32)
    return lo, hi


def _layer_norm(y, g, b, eps):
    mu = jnp.mean(y, axis=-1, keepdims=True)
    yc = y - mu
    var = jnp.mean(yc * yc, axis=-1, keepdims=True)
    return yc * lax.rsqrt(var + eps) * g + b


def _kv_body(kv_ref, ki_ref, cos_ref, sin_ref, g_ref, b_ref,
             kvh_ref, kvt_ref, kio_ref, kbf_ref, kibf_ref, *, cfg, head_blocks):
    i = pl.program_id(0)
    c_heads, d = cfg.n_kv_heads, cfg.head_dim
    cosf, sinf = cos_ref[...], sin_ref[...]
    kv = kv_ref[...]
    keys = jnp.concatenate([_rope(kv[:, c * d:(c + 1) * d], cosf, sinf) for c in range(c_heads)], axis=1)
    kbf_ref[...] = keys.astype(BF16)
    kv_out = jnp.concatenate([keys, kv[:, c_heads * d:]], axis=1)

    @pl.when(i < head_blocks)
    def _():
        kvh_ref[...] = kv_out

    @pl.when(i >= head_blocks)
    def _():
        kvt_ref[...] = kv_out

    ki = _rope(_layer_norm(ki_ref[...], g_ref[...], b_ref[...], cfg.ln_eps), cosf, sinf)
    kio_ref[...] = ki
    kibf_ref[...] = ki.astype(BF16)


def _vt_body(v_ref, vt_ref, *, cfg):
    d = cfg.head_dim
    for c in range(cfg.n_kv_heads):
        vt_ref[0, c * d:(c + 1) * d, :] = v_ref[:, c * d:(c + 1) * d].T.astype(BF16)


def _transpose_values(z_qkv, t_p, cfg):
    kc = cfg.key_chunk
    kvw = cfg.n_kv_heads * cfg.head_dim
    v_blk = (cfg.n_heads * cfg.head_dim + kvw) // kvw
    assert t_p % kc == 0
    return pl.pallas_call(
        functools.partial(_vt_body, cfg=cfg),
        grid=(t_p // kc,),
        in_specs=[pl.BlockSpec((kc, kvw), lambda i: (i, v_blk))],
        out_specs=pl.BlockSpec((1, kvw, kc), lambda i: (i, 0, 0)),
        out_shape=jax.ShapeDtypeStruct((t_p // kc, kvw, kc), BF16),
        compiler_params=_params("arbitrary"),
        name="transpose_values",
    )(z_qkv)


def _finalize_keys(z_qkv, z_i, cosf, sinf, ik_g, ik_b, n_seq, seq, cfg):
    t_all = z_qkv.shape[0]
    c_heads, d = cfg.n_kv_heads, cfg.head_dim
    kvw = c_heads * d
    qw = cfg.n_heads * d
    tr = cfg.row_align
    assert t_all % tr == 0 and seq % tr == 0 and qw % (2 * kvw) == 0
    per_seq, n_prompt = seq // tr, n_seq * seq // tr
    table = lambda i: (jnp.where(i < n_prompt, i % per_seq, per_seq + i - n_prompt), 0)
    return pl.pallas_call(
        functools.partial(_kv_body, cfg=cfg, head_blocks=n_prompt),
        grid=(t_all // tr,),
        in_specs=[pl.BlockSpec((tr, 2 * kvw), lambda i: (i, qw // (2 * kvw))),
                  pl.BlockSpec((tr, cfg.idx_dim), lambda i: (i, 0)),
                  pl.BlockSpec((tr, d), table),
                  pl.BlockSpec((tr, d), table),
                  pl.BlockSpec((1, cfg.idx_dim), lambda i: (0, 0)),
                  pl.BlockSpec((1, cfg.idx_dim), lambda i: (0, 0))],
        out_specs=[pl.BlockSpec((tr, 2 * kvw), lambda i: (jnp.minimum(i, n_prompt - 1), 0)),
                   pl.BlockSpec((tr, 2 * kvw), lambda i: (jnp.maximum(i - n_prompt, 0), 0)),
                   pl.BlockSpec((tr, cfg.idx_dim), lambda i: (i, 0)),
                   pl.BlockSpec((tr, kvw), lambda i: (i, 0)),
                   pl.BlockSpec((tr, cfg.idx_dim), lambda i: (i, 0))],
        out_shape=[jax.ShapeDtypeStruct((n_prompt * tr, 2 * kvw), F32),
                   jax.ShapeDtypeStruct((t_all - n_prompt * tr, 2 * kvw), F32),
                   jax.ShapeDtypeStruct((t_all, cfg.idx_dim), F32),
                   jax.ShapeDtypeStruct((t_all, kvw), BF16),
                   jax.ShapeDtypeStruct((t_all, cfg.idx_dim), BF16)],
        compiler_params=_params("arbitrary"),
        name="finalize_keys",
    )(z_qkv, z_i, cosf, sinf, ik_g, ik_b)


def _sort_key(s):
    b = pltpu.bitcast(s, I32)
    return b ^ ((b >> 31) & jnp.int32(0x7FFFFFFF))


def _kth_largest_key(count_ge, k, width, total):
    def bit_step(i, carry):
        ans, n_ge = carry
        cand = ans | lax.shift_left(jnp.int32(1), jnp.int32(31) - i)
        cnt = count_ge(cand ^ jnp.int32(INT_MIN))
        take = cnt >= k
        return jnp.where(take, cand, ans), jnp.where(take, cnt, n_ge)

    start = (jnp.zeros((1, width), I32), jnp.zeros((1, width), I32) + total)
    ans, n_ge = lax.fori_loop(0, 32, bit_step, start)
    return ans ^ jnp.int32(INT_MIN), n_ge


def _attn_body(q_ref, qi_ref, wi_ref, cos_ref, sin_ref, kbf_ref, kibf_ref, vt_ref,
               o_ref, qs_ref, qis_ref, key_ref, m_ref, l_ref, acc_ref, *, cfg, topk):
    qb_idx = pl.program_id(1)
    qb, kc = cfg.q_block, cfg.key_chunk
    d, di = cfg.head_dim, cfg.idx_dim
    c_heads = cfg.n_kv_heads
    g_heads = cfg.n_heads // c_heads
    cosf, sinf = cos_ref[...], sin_ref[...]

    for h in range(cfg.n_heads):
        qs_ref[h * qb:(h + 1) * qb, :] = _rope(q_ref[:, h * d:(h + 1) * d], cosf, sinf).astype(BF16)
    for h in range(cfg.idx_heads):
        qis_ref[h * qb:(h + 1) * qb, :] = _rope(qi_ref[:, h * di:(h + 1) * di], cosf, sinf).astype(BF16)
    w_t = wi_ref[...].T * F32((di * cfg.idx_heads) ** -0.5)

    q0 = qb_idx * qb
    n_chunks = (q0 + qb + kc - 1) // kc
    tpos = q0 + lax.broadcasted_iota(I32, (kc, qb), 1)
    krow = lax.broadcasted_iota(I32, (kc, qb), 0)
    nt = (((1,), (1,)), ((), ()))

    def score_chunk(ci, carry):
        k0 = pl.multiple_of(ci * kc, kc)
        ki_c = kibf_ref[pl.ds(k0, kc), :]
        acc = jnp.zeros((kc, qb), F32)
        for hp in range(cfg.idx_heads // 2):
            z = lax.dot_general(ki_c, qis_ref[hp * 2 * qb:(hp + 1) * 2 * qb, :], nt,
                                preferred_element_type=F32)
            acc = acc + jnp.maximum(z[:, :qb], 0.0) * w_t[2 * hp:2 * hp + 1, :]
            acc = acc + jnp.maximum(z[:, qb:], 0.0) * w_t[2 * hp + 1:2 * hp + 2, :]
        causal = (k0 + krow) <= tpos
        key_ref[pl.ds(k0, kc), :] = _sort_key(jnp.where(causal, acc, -jnp.inf))
        return carry

    lax.fori_loop(0, n_chunks, score_chunk, 0)

    def count_ge(cand):
        def cnt_chunk(ci, acc8):
            k0 = pl.multiple_of(ci * kc, kc)
            hit = jnp.where(key_ref[pl.ds(k0, kc), :] >= cand, 1, 0).astype(I32)
            return acc8 + jnp.sum(hit.reshape(kc // SUBLANE, SUBLANE, qb), axis=0)
        acc8 = lax.fori_loop(0, n_chunks, cnt_chunk, jnp.zeros((SUBLANE, qb), I32))
        return jnp.sum(acc8, axis=0, keepdims=True)

    thr, n_ge = _kth_largest_key(count_ge, topk, qb, n_chunks * kc)

    surplus = jnp.where(thr > _sort_key(jnp.full((1, qb), -jnp.inf, F32)), n_ge - topk, 0)

    @pl.when(jnp.max(surplus) > 0)
    def _():
        wanted = (topk - count_ge(thr + 1)).astype(F32)
        earlier = jnp.where(lax.broadcasted_iota(I32, (kc, kc), 1) < lax.broadcasted_iota(I32, (kc, kc), 0),
                            1.0, 0.0).astype(BF16)

        def strike_chunk(ci, seen):
            k0 = pl.multiple_of(ci * kc, kc)
            blk = key_ref[pl.ds(k0, kc), :]
            tied = jnp.where(blk == thr, 1.0, 0.0)
            rank = seen + jnp.dot(earlier, tied.astype(BF16), preferred_element_type=F32)
            key_ref[pl.ds(k0, kc), :] = jnp.where((blk == thr) & (rank >= wanted), jnp.int32(INT_MIN), blk)
            return seen + jnp.sum(tied, axis=0, keepdims=True)

        lax.fori_loop(0, n_chunks, strike_chunk, jnp.zeros((1, qb), F32))

    m_ref[...] = jnp.full(m_ref.shape, MASKED, F32)
    l_ref[...] = jnp.zeros(l_ref.shape, F32)
    acc_ref[...] = jnp.zeros(acc_ref.shape, F32)
    scale = F32(d ** -0.5 * np.log2(np.e))

    def attend_chunk(ci, carry):
        k0 = pl.multiple_of(ci * kc, kc)
        keep = (key_ref[pl.ds(k0, kc), :] >= thr) & ((k0 + krow) <= tpos)
        bias = jnp.where(keep, 0.0, MASKED).astype(F32)
        bias = jnp.concatenate([bias] * g_heads, axis=1)
        for c in range(c_heads):
            k_c = kbf_ref[pl.ds(k0, kc), c * d:(c + 1) * d]
            s = lax.dot_general(k_c, qs_ref[c * g_heads * qb:(c + 1) * g_heads * qb, :], nt,
                                preferred_element_type=F32) * scale + bias
            m_old = m_ref[c:c + 1, :]
            m_new = jnp.maximum(m_old, jnp.max(s, axis=0, keepdims=True))
            alpha = jnp.exp2(m_old - m_new)
            p = jnp.exp2(s - m_new)
            l_ref[c:c + 1, :] = alpha * l_ref[c:c + 1, :] + jnp.sum(p, axis=0, keepdims=True)
            pv = jnp.dot(vt_ref[ci, c * d:(c + 1) * d, :], p.astype(BF16), preferred_element_type=F32)
            acc_ref[c * d:(c + 1) * d, :] = alpha * acc_ref[c * d:(c + 1) * d, :] + pv
            m_ref[c:c + 1, :] = m_new
        return carry

    lax.fori_loop(0, n_chunks, attend_chunk, 0)

    for c in range(c_heads):
        o_t = acc_ref[c * d:(c + 1) * d, :] / l_ref[c:c + 1, :]
        for g in range(g_heads):
            h = c * g_heads + g
            o_ref[:, h * d:(h + 1) * d] = o_t[:, g * qb:(g + 1) * qb].T.astype(o_ref.dtype)


def _prompt_attention(z_qkv, z_i, z_w, cosf, sinf, k_bf, ki_bf, v_t, n_seq, seq, cfg):
    qb, kc = cfg.q_block, cfg.key_chunk
    d = cfg.head_dim
    qw = cfg.n_heads * d
    kvw = cfg.n_kv_heads * d
    iw = cfg.idx_heads * cfg.idx_dim
    assert seq % kc == 0 and seq % qb == 0 and kc % qb == 0
    assert cfg.head_dim == LANE and cfg.idx_dim == LANE
    nqb = seq // qb
    topk = min(cfg.topk_max, seq // 4)
    row = lambda n, j: (n * nqb + j, 0)
    return pl.pallas_call(
        functools.partial(_attn_body, cfg=cfg, topk=topk),
        grid=(n_seq, nqb),
        in_specs=[pl.BlockSpec((qb, qw), row),
                  pl.BlockSpec((qb, iw), row),
                  pl.BlockSpec((qb, LANE), lambda n, j: (n * nqb + j, 1)),
                  pl.BlockSpec((qb, d), lambda n, j: (j, 0)),
                  pl.BlockSpec((qb, d), lambda n, j: (j, 0)),
                  pl.BlockSpec((seq, kvw), lambda n, j: (n, 0)),
                  pl.BlockSpec((seq, cfg.idx_dim), lambda n, j: (n, 0)),
                  pl.BlockSpec((seq // kc, kvw, kc), lambda n, j: (n, 0, 0))],
        out_specs=pl.BlockSpec((qb, qw), row),
        out_shape=jax.ShapeDtypeStruct((n_seq * seq, qw), BF16),
        scratch_shapes=[pltpu.VMEM((cfg.n_heads * qb, d), BF16),
                        pltpu.VMEM((cfg.idx_heads * qb, cfg.idx_dim), BF16),
                        pltpu.VMEM((seq, qb), I32),
                        pltpu.VMEM((SUBLANE, (cfg.n_heads // cfg.n_kv_heads) * qb), F32),
                        pltpu.VMEM((SUBLANE, (cfg.n_heads // cfg.n_kv_heads) * qb), F32),
                        pltpu.VMEM((kvw, (cfg.n_heads // cfg.n_kv_heads) * qb), F32)],
        compiler_params=_params("arbitrary", "arbitrary"),
        name="prompt_attention",
    )(z_qkv, z_i, z_w, cosf, sinf, k_bf, ki_bf, v_t)


def _softplus(x):
    return jnp.maximum(x, 0.0) + jnp.log1p(jnp.exp(-jnp.abs(x)))


def _lru_gates(xc, wa_ref, wx_ref, ba, bx, lam, cfg):
    w = xc.shape[1]
    bd = w // cfg.lru_blocks
    xb = xc.astype(BF16)
    r_parts, i_parts = [], []
    for k in range(cfg.lru_blocks):
        xk = xb[:, k * bd:(k + 1) * bd]
        r_parts.append(jnp.dot(xk, wa_ref[k], preferred_element_type=F32))
        i_parts.append(jnp.dot(xk, wx_ref[k], preferred_element_type=F32))
    r = jax.nn.sigmoid(jnp.concatenate(r_parts, axis=1) + ba)
    gate_i = jax.nn.sigmoid(jnp.concatenate(i_parts, axis=1) + bx)
    log_a = (-cfg.lru_c * r) * _softplus(-lam)
    a = jnp.exp(log_a)
    u = jnp.sqrt(-jnp.tanh(log_a) * (a * a + 1.0)) * (gate_i * xc)
    return a, u


def _lru_prompt_body(xl_ref, cw_ref, cb_ref, wa_ref, wx_ref, ba_ref, bx_ref, lam_ref,
                     h_ref, hlast_ref, prev_ref, carry_ref, *, cfg):
    j = pl.program_id(1)
    tb = xl_ref.shape[0]
    cw = cfg.conv_width

    @pl.when(j == 0)
    def _():
        prev_ref[...] = jnp.zeros(prev_ref.shape, F32)
        carry_ref[...] = jnp.zeros(carry_ref.shape, F32)

    xl = xl_ref[...]
    ext = jnp.concatenate([prev_ref[...], xl], axis=0)
    off = SUBLANE - (cw - 1)
    xc = cb_ref[...] + ext[off:off + tb] * cw_ref[0:1, :]
    for t in range(1, cw):
        xc = xc + ext[off + t:off + t + tb] * cw_ref[t:t + 1, :]
    prev_ref[...] = xl[tb - SUBLANE:, :]

    a, u = _lru_gates(xc, wa_ref, wx_ref, ba_ref[...], bx_ref[...], lam_ref[...], cfg)

    row = lax.broadcasted_iota(I32, a.shape, 0) & (SUBLANE - 1)
    s = 1
    while s < SUBLANE:
        ok = row >= s
        a_sh = jnp.where(ok, pltpu.roll(a, s, axis=0), 1.0)
        u_sh = jnp.where(ok, pltpu.roll(u, s, axis=0), 0.0)
        u = u + a * u_sh
        a = a * a_sh
        s *= 2
    h_prev = carry_ref[...]
    for gi in range(tb // SUBLANE):
        sl = slice(gi * SUBLANE, (gi + 1) * SUBLANE)
        h_rows = u[sl] + a[sl] * h_prev
        h_ref[sl, :] = h_rows.astype(h_ref.dtype)
        h_prev = h_rows[SUBLANE - 1:SUBLANE, :]
    carry_ref[...] = h_prev
    hlast_ref[0] = h_prev


def _lru_prompt(z_b, conv_w, conv_b, wa, wx, ba, bx, lam, n_seq, seq, cfg):
    w = conv_w.shape[1]
    tb = _pick(seq, (256, 128))
    nb = seq // tb
    bd = w // cfg.lru_blocks
    vec = pl.BlockSpec((1, w), lambda n, j: (0, 0))
    return pl.pallas_call(
        functools.partial(_lru_prompt_body, cfg=cfg),
        grid=(n_seq, nb),
        in_specs=[pl.BlockSpec((tb, w), lambda n, j: (n * nb + j, 0)),
                  pl.BlockSpec((cfg.conv_width, w), lambda n, j: (0, 0)),
                  vec,
                  pl.BlockSpec((cfg.lru_blocks, bd, bd), lambda n, j: (0, 0, 0)),
                  pl.BlockSpec((cfg.lru_blocks, bd, bd), lambda n, j: (0, 0, 0)),
                  vec, vec, vec],
        out_specs=[pl.BlockSpec((tb, w), lambda n, j: (n * nb + j, 0)),
                   pl.BlockSpec((1, 1, w), lambda n, j: (n, 0, 0))],
        out_shape=[jax.ShapeDtypeStruct((n_seq * seq, w), BF16),
                   jax.ShapeDtypeStruct((n_seq, 1, w), F32)],
        scratch_shapes=[pltpu.VMEM((SUBLANE, w), F32), pltpu.VMEM((1, w), F32)],
        compiler_params=_params("arbitrary", "arbitrary"),
        name="lru_prompt",
    )(z_b, conv_w, conv_b, wa, wx, ba, bx, lam)


def _lru_sample_body(xl_ref, hist_ref, h0_ref, cw_ref, cb_ref, wa_ref, wx_ref, ba_ref, bx_ref, lam_ref,
                     h_ref, hbf_ref, *, cfg):
    cw = cfg.conv_width
    xc = cb_ref[...] + hist_ref[0] * cw_ref[0:1, :]
    for t in range(1, cw - 1):
        xc = xc + hist_ref[t] * cw_ref[t:t + 1, :]
    xc = xc + xl_ref[...] * cw_ref[cw - 1:cw, :]
    a, u = _lru_gates(xc, wa_ref, wx_ref, ba_ref[...], bx_ref[...], lam_ref[...], cfg)
    h = a * h0_ref[...] + u
    h_ref[...] = h
    hbf_ref[...] = h.astype(BF16)


def _lru_sample(xl, hist, h0, conv_w, conv_b, wa, wx, ba, bx, lam, cfg):
    n, w = xl.shape
    return pl.pallas_call(
        functools.partial(_lru_sample_body, cfg=cfg),
        out_shape=[jax.ShapeDtypeStruct((n, w), F32), jax.ShapeDtypeStruct((n, w), BF16)],
        compiler_params=pltpu.CompilerParams(vmem_limit_bytes=VMEM_LIMIT),
        name="lru_sample",
    )(xl, hist, h0, conv_w, conv_b, wa, wx, ba, bx, lam)


def _sample_scores_body(pt_ref, qi_ref, w_ref, kin_ref, cos_ref, sin_ref, *rest, cfg, pages_per_step):
    page_refs, o_ref = rest[:pages_per_step], rest[pages_per_step]
    j = pl.program_id(1)
    nj = pl.num_programs(1)
    cosf, sinf = cos_ref[0], sin_ref[0]
    qi = _rope(qi_ref[0], cosf, sinf)
    qi_b = qi.astype(BF16)
    w = w_ref[0] * F32((cfg.idx_dim * cfg.idx_heads) ** -0.5)
    nt = (((1,), (1,)), ((), ()))
    rows = []
    for p in range(pages_per_step):
        z = lax.dot_general(qi_b, page_refs[p][0].astype(BF16), nt, preferred_element_type=F32)
        rows.append(jnp.sum(jnp.maximum(z, 0.0) * w, axis=0, keepdims=True))
    r0 = pl.multiple_of(j * pages_per_step, pages_per_step)
    o_ref[0, pl.ds(r0, pages_per_step), :] = jnp.concatenate(rows, axis=0)

    @pl.when(j == nj - 1)
    def _():
        z_new = jnp.sum(qi * kin_ref[0], axis=1, keepdims=True)
        s_new = jnp.sum(jnp.maximum(z_new, 0.0) * w, axis=0, keepdims=True)
        lane = lax.broadcasted_iota(I32, (SUBLANE, LANE), 1)
        sub = lax.broadcasted_iota(I32, (SUBLANE, LANE), 0)
        tail = jnp.where((lane == 0) & (sub == 0), jnp.broadcast_to(s_new, (SUBLANE, LANE)), -jnp.inf)
        o_ref[0, pl.ds(nj * pages_per_step, SUBLANE), :] = tail


def _sample_scores(page_table, qi, w_rep, ki_new, cos_s, sin_s, cache_idx_k, cfg):
    n, n_pages = page_table.shape
    pps = _pick(n_pages, (2 * SUBLANE, SUBLANE))
    assert cfg.page_size == LANE
    hi, di = cfg.idx_heads, cfg.idx_dim
    per_seq = lambda b, j, pt: (b, 0, 0)
    page_specs = [pl.BlockSpec((1, cfg.page_size, di),
                               functools.partial(lambda b, j, pt, p: (pt[b, j * pps + p], 0, 0), p=p))
                  for p in range(pps)]
    grid_spec = pltpu.PrefetchScalarGridSpec(
        num_scalar_prefetch=1,
        grid=(n, n_pages // pps),
        in_specs=[pl.BlockSpec((1, hi, di), per_seq),
                  pl.BlockSpec((1, hi, LANE), per_seq),
                  pl.BlockSpec((1, 1, di), per_seq),
                  pl.BlockSpec((1, 1, di), per_seq),
                  pl.BlockSpec((1, 1, di), per_seq)] + page_specs,
        out_specs=pl.BlockSpec((1, n_pages + SUBLANE, LANE), per_seq),
    )
    return pl.pallas_call(
        functools.partial(_sample_scores_body, cfg=cfg, pages_per_step=pps),
        grid_spec=grid_spec,
        out_shape=jax.ShapeDtypeStruct((n, n_pages + SUBLANE, LANE), F32),
        compiler_params=_params("arbitrary", "arbitrary"),
        name="sample_scores",
    )(page_table, qi, w_rep, ki_new, cos_s, sin_s, *([cache_idx_k] * pps))


def _row_major_rank(flag, upper, lower):
    fb = flag.astype(BF16)
    in_row = jnp.dot(fb, upper, preferred_element_type=F32)
    row_tot = jnp.sum(flag, axis=1, keepdims=True)
    before = jnp.dot(lower, jnp.broadcast_to(row_tot, flag.shape).astype(BF16), preferred_element_type=F32)
    return before + in_row


def _sample_attend_body(pt_ref, s_ref, q_ref, kvn_ref, cos_ref, sin_ref, cache_hbm, o_ref,
                        lst_v, lst_s, buf_ref, lst_sem, sem, *, cfg, topk):
    b = pl.program_id(0)
    nh, d = cfg.n_heads, cfg.head_dim
    c_heads = cfg.n_kv_heads
    g_heads = nh // c_heads
    per = 2 * c_heads
    ps = cfg.page_size
    n_rows = s_ref.shape[1]
    n_pages = n_rows - SUBLANE
    scale = F32(d ** -0.5)
    head_c = lax.broadcasted_iota(I32, (nh, 1), 0) // g_heads

    score = s_ref[0]
    keys = _sort_key(score)

    def count_ge(cand):
        hit = jnp.where(keys >= cand, 1, 0).astype(I32)
        return jnp.sum(jnp.sum(hit, axis=0, keepdims=True), axis=1, keepdims=True)

    thr, _ = _kth_largest_key(count_ge, topk, 1, n_rows * LANE)
    live = score > -jnp.inf
    above = jnp.where((keys > thr) & live, 1.0, 0.0)
    tied = jnp.where((keys == thr) & live, 1.0, 0.0)
    upper = jnp.where(lax.broadcasted_iota(I32, (LANE, LANE), 0) < lax.broadcasted_iota(I32, (LANE, LANE), 1),
                      1.0, 0.0).astype(BF16)
    lower = jnp.where(lax.broadcasted_iota(I32, (n_rows, n_rows), 1) < lax.broadcasted_iota(I32, (n_rows, n_rows), 0),
                      1.0, 0.0).astype(BF16)
    n_above = jnp.sum(jnp.sum(above, axis=0, keepdims=True), axis=1, keepdims=True)
    keep = above + tied * jnp.where(_row_major_rank(tied, upper, lower) < F32(topk) - n_above, 1.0, 0.0)
    row = lax.broadcasted_iota(I32, (n_rows, LANE), 0)
    keep_new = jnp.sum(jnp.sum(jnp.where(row == n_pages, keep, 0.0), axis=0, keepdims=True),
                       axis=1, keepdims=True)
    keep = jnp.where(row < n_pages, keep, 0.0)

    rank = _row_major_rank(keep, upper, lower)
    n_sel = jnp.sum(jnp.sum(keep, axis=0, keepdims=True), axis=1, keepdims=True)
    pad_rows = jnp.zeros((LANE - n_rows, LANE), F32)
    rank_t = jnp.concatenate([rank, pad_rows], axis=0).T
    keep_t = jnp.concatenate([keep, pad_rows], axis=0).T
    list_pos = lax.broadcasted_iota(I32, (ps, topk), 1).astype(F32)
    slot_and_one = jnp.where(lax.broadcasted_iota(I32, (SUBLANE, ps), 0) == 0,
                             lax.broadcasted_iota(I32, (SUBLANE, ps), 1).astype(F32), 1.0).astype(BF16)
    pages = jnp.zeros((1, topk), F32)
    slots = jnp.zeros((1, topk), F32)
    for p in range(n_pages):
        here = jnp.where((rank_t[:, p:p + 1] == list_pos) & (keep_t[:, p:p + 1] > 0.5), 1.0, 0.0).astype(BF16)
        hit = jnp.dot(slot_and_one, here, preferred_element_type=F32)
        slots = slots + hit[0:1, :]
        pages = pages + F32(p) * hit[1:2, :]
    lst_v[...] = jnp.concatenate([pages, slots, jnp.zeros((SUBLANE - 2, topk), F32)], axis=0).astype(I32)
    cp = pltpu.make_async_copy(lst_v, lst_s, lst_sem)
    cp.start()
    cp.wait()

    def fetch(g, carry):
        for u in range(DMA_THREADS):
            r = g * DMA_THREADS + u
            src = pl.multiple_of((pt_ref[b, lst_s[0, r]] * ps + lst_s[1, r]) * per, per)
            pltpu.make_async_copy(cache_hbm.at[pl.ds(src, per)],
                                  buf_ref.at[pl.ds(pl.multiple_of(r * per, per), per)], sem).start(priority=u)
        return carry
    lax.fori_loop(0, topk // DMA_THREADS, fetch, 0)

    q_f = _rope(q_ref[0], cos_ref[0], sin_ref[0])
    kvn = kvn_ref[0]
    k_new = jnp.zeros((nh, d), F32)
    v_new = jnp.zeros((nh, d), F32)
    for c in range(c_heads):
        k_new = jnp.where(head_c == c, kvn[c:c + 1, :], k_new)
        v_new = jnp.where(head_c == c, kvn[c_heads + c:c_heads + c + 1, :], v_new)
    s_new = jnp.where(keep_new > 0.5, jnp.sum(q_f * k_new, axis=1, keepdims=True) * scale, MASKED)

    pltpu.make_async_copy(buf_ref, buf_ref, sem).wait()

    width = topk * per
    rows = buf_ref[...].astype(BF16)
    col = lax.broadcasted_iota(I32, (nh, width), 1)
    mine = ((col % per) == head_c) & ((col // per).astype(F32) < n_sel)
    nt = (((1,), (1,)), ((), ()))
    s = jnp.where(mine, lax.dot_general(q_f.astype(BF16), rows, nt, preferred_element_type=F32) * scale, MASKED)
    m = jnp.maximum(jnp.max(s, axis=1, keepdims=True), s_new)
    p_un = jnp.exp(s - m)
    p_new = jnp.exp(s_new - m)
    den = jnp.sum(p_un, axis=1, keepdims=True) + p_new
    pv = jnp.dot(pltpu.roll(p_un, c_heads, axis=1).astype(BF16), rows, preferred_element_type=F32)
    o_ref[0] = ((pv + p_new * v_new) / den).astype(o_ref.dtype)


def _sample_attend(page_table, scores, q, kv_new, cos_s, sin_s, cache_kv, cfg):
    n, n_pages = page_table.shape
    nh, d = cfg.n_heads, cfg.head_dim
    per = 2 * cfg.n_kv_heads
    topk = min(cfg.topk_max, (n_pages * cfg.page_size + 1) // 4)
    n_rows = scores.shape[1]
    assert n_rows <= LANE and cfg.page_size == LANE and per == SUBLANE and topk % DMA_THREADS == 0
    per_seq = lambda b, pt: (b, 0, 0)
    grid_spec = pltpu.PrefetchScalarGridSpec(
        num_scalar_prefetch=1,
        grid=(n,),
        in_specs=[pl.BlockSpec((1, n_rows, LANE), per_seq),
                  pl.BlockSpec((1, nh, d), per_seq),
                  pl.BlockSpec((1, per, d), per_seq),
                  pl.BlockSpec((1, 1, d), per_seq),
                  pl.BlockSpec((1, 1, d), per_seq),
                  pl.BlockSpec(memory_space=pl.ANY)],
        out_specs=pl.BlockSpec((1, nh, d), per_seq),
        scratch_shapes=[pltpu.VMEM((SUBLANE, topk), I32),
                        pltpu.SMEM((SUBLANE, topk), I32),
                        pltpu.VMEM((topk * per, d), F32),
                        pltpu.SemaphoreType.DMA(()), pltpu.SemaphoreType.DMA(())],
    )
    return pl.pallas_call(
        functools.partial(_sample_attend_body, cfg=cfg, topk=topk),
        grid_spec=grid_spec,
        out_shape=jax.ShapeDtypeStruct((n, nh, d), BF16),
        compiler_params=_params("arbitrary"),
        name="sample_attend",
    )(page_table, scores, q, kv_new, cos_s, sin_s, cache_kv)


def _merge_body(oa_ref, hl_ref, woa_ref, wol_ref, ga_ref, gb_ref, o_ref):
    ya = jnp.dot(oa_ref[...], woa_ref[...], preferred_element_type=F32)
    yl = jnp.dot(hl_ref[...], wol_ref[...], preferred_element_type=F32)
    o_ref[...] = (jax.nn.sigmoid(ga_ref[...]) * ya + jax.nn.sigmoid(gb_ref[...]) * yl).astype(o_ref.dtype)


def _merge(o_attn, h_lru, w_oa, w_ol, z_b, lru_w, cfg):
    m, aw = o_attn.shape
    dm = w_oa.shape[1]
    tm = _pick(m, (1408, 1024, 768, 640, 512, 256))
    tn = _pick(dm, (512, 256, 128))
    assert lru_w % tn == 0
    ga_off = lru_w // tn
    gb_off = (lru_w + dm) // tn
    return pl.pallas_call(
        _merge_body,
        grid=(m // tm, dm // tn),
        in_specs=[pl.BlockSpec((tm, aw), lambda i, j: (i, 0)),
                  pl.BlockSpec((tm, lru_w), lambda i, j: (i, 0)),
                  pl.BlockSpec((aw, tn), lambda i, j: (0, j)),
                  pl.BlockSpec((lru_w, tn), lambda i, j: (0, j)),
                  pl.BlockSpec((tm, tn), lambda i, j: (i, ga_off + j)),
                  pl.BlockSpec((tm, tn), lambda i, j: (i, gb_off + j))],
        out_specs=pl.BlockSpec((tm, tn), lambda i, j: (i, j)),
        out_shape=jax.ShapeDtypeStruct((m, dm), BF16),
        compiler_params=_params("arbitrary", "arbitrary"),
        name="merge_mixers",
    )(o_attn, h_lru, w_oa, w_ol, z_b, z_b)


def _proj_ln_body(a_ref, w_ref, x_ref, g_ref, b_ref, o_ref, obf_ref, opk_ref, *, alpha, eps, tn):
    j = pl.program_id(1)
    c0 = pl.multiple_of(j * tn, tn)
    y = jnp.dot(a_ref[...], w_ref[...], preferred_element_type=F32)
    o_ref[:, pl.ds(c0, tn)] = F32(alpha) * x_ref[...] + y

    @pl.when(j == pl.num_programs(1) - 1)
    def _():
        out = _layer_norm(o_ref[...], g_ref[...], b_ref[...], eps)
        o_ref[...] = out
        obf_ref[...] = out.astype(BF16)
        opk_ref[...] = _pack_halves(out)


def _proj_residual_ln(a, w, x, g, b, alpha, cfg):
    m, k = a.shape
    dm = w.shape[1]
    tm = _pick(m, (384, 256))
    tn = _pick(dm, (512, 256, 128))
    return pl.pallas_call(
        functools.partial(_proj_ln_body, alpha=alpha, eps=cfg.ln_eps, tn=tn),
        grid=(m // tm, dm // tn),
        in_specs=[pl.BlockSpec((tm, k), lambda i, j: (i, 0)),
                  pl.BlockSpec((k, tn), lambda i, j: (0, j)),
                  pl.BlockSpec((tm, tn), lambda i, j: (i, j)),
                  pl.BlockSpec((1, dm), lambda i, j: (0, 0)),
                  pl.BlockSpec((1, dm), lambda i, j: (0, 0))],
        out_specs=[pl.BlockSpec((tm, dm), lambda i, j: (i, 0)),
                   pl.BlockSpec((tm, dm), lambda i, j: (i, 0)),
                   pl.BlockSpec((tm, dm // 2), lambda i, j: (i, 0))],
        out_shape=[jax.ShapeDtypeStruct((m, dm), F32), jax.ShapeDtypeStruct((m, dm), BF16),
                   jax.ShapeDtypeStruct((m, dm // 2), U32)],
        compiler_params=_params("arbitrary", "arbitrary"),
        name="proj_residual_ln",
    )(a, w, x, g, b)


def _first_index_of_max(v, idx, big):
    m = jnp.max(v, axis=0, keepdims=True)
    first = jnp.min(jnp.where(v == m, idx, big), axis=0, keepdims=True)
    return m, first


def _router_body(x_ref, wr_ref, br_ref, ids_ref, gate_ref, rank_ref, cnt_ref, *, cfg, n_tok):
    i = pl.program_id(0)
    e, ng = cfg.n_experts, cfg.n_groups
    per = e // ng
    tm = x_ref.shape[0]
    nt = (((1,), (1,)), ((), ()))
    logits = lax.dot_general(wr_ref[...], x_ref[...], nt, preferred_element_type=F32)
    s = jax.nn.sigmoid(logits)
    choice = s + br_ref[...]
    eidx = lax.broadcasted_iota(I32, (e, tm), 0)

    grp_rows = []
    jidx = lax.broadcasted_iota(I32, (per, tm), 0)
    for g in range(ng):
        cg = choice[g * per:(g + 1) * per, :]
        m1, j1 = _first_index_of_max(cg, jidx, per)
        m2 = jnp.max(jnp.where(jidx == j1, -jnp.inf, cg), axis=0, keepdims=True)
        grp_rows.append(m1 + m2)
    grp = jnp.concatenate(grp_rows, axis=0)

    gidx = lax.broadcasted_iota(I32, (ng, tm), 0)
    grp_keep = jnp.zeros((ng, tm), F32)
    work = grp
    for _ in range(cfg.topk_groups):
        _, gsel = _first_index_of_max(work, gidx, ng)
        hit = gidx == gsel
        grp_keep = jnp.where(hit, 1.0, grp_keep)
        work = jnp.where(hit, -jnp.inf, work)

    keep_rows = [jnp.broadcast_to(grp_keep[g:g + 1, :], (per, tm)) for g in range(ng)]
    masked = jnp.where(jnp.concatenate(keep_rows, axis=0) > 0.5, choice, -jnp.inf)

    ids, wts, hits = [], [], []
    for _ in range(cfg.top_k):
        _, esel = _first_index_of_max(masked, eidx, e)
        hit = eidx == esel
        ids.append(esel)
        hits.append(hit)
        wts.append(jnp.sum(jnp.where(hit, s, 0.0), axis=0, keepdims=True))
        masked = jnp.where(hit, -jnp.inf, masked)
    wk = jnp.concatenate(wts, axis=0)
    ids_ref[...] = jnp.concatenate(ids, axis=0)
    gate_ref[...] = F32(cfg.route_scale) * wk / jnp.sum(wk, axis=0, keepdims=True)

    @pl.when(i == 0)
    def _():
        cnt_ref[...] = jnp.zeros(cnt_ref.shape, F32)

    tok = i * tm + lax.broadcasted_iota(I32, (e, tm), 1)
    picked = jnp.zeros((e, tm), F32)
    for h in hits:
        picked = jnp.where(h, 1.0, picked)
    picked = jnp.where(tok < n_tok, picked, 0.0)
    before = (lax.broadcasted_iota(I32, (tm, tm), 0) < lax.broadcasted_iota(I32, (tm, tm), 1))
    earlier = jnp.dot(picked.astype(BF16), jnp.where(before, 1.0, 0.0).astype(BF16),
                      preferred_element_type=F32)
    rank_all = cnt_ref[:, 0:1] + earlier
    rank_ref[...] = jnp.concatenate(
        [jnp.sum(jnp.where(h, rank_all, 0.0), axis=0, keepdims=True) for h in hits], axis=0).astype(I32)
    cnt_ref[...] = cnt_ref[...] + jnp.sum(picked, axis=1, keepdims=True)


def _router(x_bf, w_r_t, b_r, n_tok, cfg):
    m, dm = x_bf.shape
    tm = _pick(m, (256, 128))
    e = cfg.n_experts
    pick = pl.BlockSpec((cfg.top_k, tm), lambda i: (0, i))
    return pl.pallas_call(
        functools.partial(_router_body, cfg=cfg, n_tok=n_tok),
        grid=(m // tm,),
        in_specs=[pl.BlockSpec((tm, dm), lambda i: (i, 0)),
                  pl.BlockSpec((e, dm), lambda i: (0, 0)),
                  pl.BlockSpec((e, 1), lambda i: (0, 0))],
        out_specs=[pick, pick, pick, pl.BlockSpec((e, LANE), lambda i: (0, 0))],
        out_shape=[jax.ShapeDtypeStruct((cfg.top_k, m), I32),
                   jax.ShapeDtypeStruct((cfg.top_k, m), F32),
                   jax.ShapeDtypeStruct((cfg.top_k, m), I32),
                   jax.ShapeDtypeStruct((e, LANE), F32)],
        compiler_params=_params("arbitrary"),
        name="router",
    )(x_bf, w_r_t, b_r)


DMA_THREADS = 2


def _dispatch_body(nfill_ref, dst_ref, fill_ref, x_ref, xs_hbm, idx_smem, zero_ref, idx_sem, sem,
                   *, tb, top_k, token_blocks):
    i = pl.program_id(0)
    n = tb * top_k

    def scatter(table_row, source_row):
        cp = pltpu.make_async_copy(table_row, idx_smem, idx_sem)
        cp.start()
        cp.wait()

        def issue(g, carry):
            for k in range(top_k):
                for u in range(DMA_THREADS):
                    t = g * DMA_THREADS + u
                    pltpu.make_async_copy(source_row(t), xs_hbm.at[pl.ds(idx_smem[0, k * tb + t], 1), :],
                                          sem).start(priority=u)
            return carry
        lax.fori_loop(0, tb // DMA_THREADS, issue, 0)
        pltpu.make_async_copy(xs_hbm.at[pl.ds(0, n), :], xs_hbm.at[pl.ds(0, n), :], sem).wait()

    @pl.when(i < token_blocks)
    def _():
        scatter(dst_ref.at[jnp.minimum(i, token_blocks - 1)], lambda t: x_ref.at[pl.ds(t, 1), :])

    @pl.when(i == token_blocks)
    def _():
        zero_ref[...] = jnp.zeros(zero_ref.shape, zero_ref.dtype)

    @pl.when((i >= token_blocks) & ((i - token_blocks) * n < nfill_ref[0]))
    def _():
        scatter(fill_ref.at[jnp.maximum(i - token_blocks, 0)], lambda t: zero_ref)


def _dispatch(x_pk, dst_blocks, fill_blocks, n_fill, rows_total, tb, cfg):
    t_all, w = x_pk.shape
    nb, nfb = dst_blocks.shape[0], fill_blocks.shape[0]
    n = tb * cfg.top_k
    grid_spec = pltpu.PrefetchScalarGridSpec(
        num_scalar_prefetch=1,
        grid=(nb + nfb,),
        in_specs=[pl.BlockSpec((nb, 1, n), lambda i, nf: (0, 0, 0)),
                  pl.BlockSpec((nfb, 1, n), lambda i, nf: (0, 0, 0)),
                  pl.BlockSpec((tb, w), lambda i, nf: (jnp.minimum(i, nb - 1), 0))],
        out_specs=pl.BlockSpec(memory_space=pl.ANY),
        scratch_shapes=[pltpu.SMEM((1, n), I32), pltpu.VMEM((1, w), x_pk.dtype),
                        pltpu.SemaphoreType.DMA(()), pltpu.SemaphoreType.DMA(())],
    )
    return pl.pallas_call(
        functools.partial(_dispatch_body, tb=tb, top_k=cfg.top_k, token_blocks=nb),
        grid_spec=grid_spec,
        out_shape=jax.ShapeDtypeStruct((rows_total, w), x_pk.dtype),
        compiler_params=_params("arbitrary"),
        name="moe_dispatch",
    )(n_fill, dst_blocks, fill_blocks, x_pk)


def _stream_expert_weights(te_ref, tf_ref, nx_ref, nt_ref, w_hbms, w_casts, wbuf, sems, cnt):
    p, m = pl.program_id(0), pl.program_id(1)
    width = w_casts[0].shape[1]

    def copies(expert, blk, slot):
        col = pl.multiple_of(blk * width, width)
        return [pltpu.make_async_copy(w.at[expert, :, pl.ds(col, width)], wbuf.at[slot, i], sems.at[slot])
                for i, w in enumerate(w_hbms)]

    @pl.when((p == 0) & (m == 0))
    def _():
        cnt[0] = 0
        for c in copies(te_ref[0], 0, 0):
            c.start()

    @pl.when((tf_ref[m] == 1) & (m < nt_ref[0]))
    def _():
        slot = cnt[0] % 2
        for c in copies(te_ref[m], p, slot):
            c.wait()
        nxt = nx_ref[m]

        @pl.when(nxt >= 0)
        def _():
            for c in copies(nxt, p, 1 - slot):
                c.start()

        @pl.when((nxt < 0) & (p + 1 < pl.num_programs(0)))
        def _():
            for c in copies(te_ref[0], p + 1, 1 - slot):
                c.start()

        for i, dst in enumerate(w_casts):
            dst[...] = wbuf[slot, i].astype(BF16)
        cnt[0] = cnt[0] + 1


def _expert_up_body(te_ref, tf_ref, nx_ref, nt_ref, x_ref, wg_hbm, wu_hbm, o_ref, wg_s, wu_s, wbuf, sems, cnt):
    _stream_expert_weights(te_ref, tf_ref, nx_ref, nt_ref, (wg_hbm, wu_hbm), (wg_s, wu_s), wbuf, sems, cnt)

    @pl.when(pl.program_id(1) < nt_ref[0])
    def _():
        lo, hi = _unpack_halves(x_ref[...])
        x = jnp.concatenate([lo.astype(BF16), hi.astype(BF16)], axis=1)
        g = jnp.dot(x, wg_s[...], preferred_element_type=F32)
        u = jnp.dot(x, wu_s[...], preferred_element_type=F32)
        o_ref[...] = (jax.nn.silu(g) * u).astype(o_ref.dtype)


def _expert_up(xs, r_max, w_gate, w_up, tile_expert, tile_first, tile_next, n_tiles, tile):
    e, dm, f = w_gate.shape
    tf = _pick(f, (512, 256, 128))
    n_mt = r_max // tile
    clamp = lambda m, nt: jnp.minimum(m, nt[0] - 1)
    grid_spec = pltpu.PrefetchScalarGridSpec(
        num_scalar_prefetch=4,
        grid=(f // tf, n_mt),
        in_specs=[pl.BlockSpec((tile, dm // 2), lambda fi, m, te, tfst, nx, nt: (clamp(m, nt), 0)),
                  pl.BlockSpec(memory_space=pl.ANY),
                  pl.BlockSpec(memory_space=pl.ANY)],
        out_specs=pl.BlockSpec((tile, tf), lambda fi, m, te, tfst, nx, nt: (clamp(m, nt), fi)),
        scratch_shapes=[pltpu.VMEM((dm, tf), BF16), pltpu.VMEM((dm, tf), BF16),
                        pltpu.VMEM((2, 2, dm, tf), F32), pltpu.SemaphoreType.DMA((2,)),
                        pltpu.SMEM((1,), I32)],
    )
    return pl.pallas_call(
        _expert_up_body,
        grid_spec=grid_spec,
        out_shape=jax.ShapeDtypeStruct((r_max, f), BF16),
        compiler_params=_params("arbitrary", "arbitrary"),
        name="expert_up",
    )(tile_expert, tile_first, tile_next, n_tiles, xs, w_gate, w_up)


def _expert_down_body(te_ref, tf_ref, nx_ref, nt_ref, h_ref, wd_hbm, o_ref, wd_s, wbuf, sems, cnt):
    _stream_expert_weights(te_ref, tf_ref, nx_ref, nt_ref, (wd_hbm,), (wd_s,), wbuf, sems, cnt)

    @pl.when(pl.program_id(1) < nt_ref[0])
    def _():
        o_ref[...] = jnp.dot(h_ref[...], wd_s[...], preferred_element_type=F32)


def _expert_down(hid, w_down, tile_expert, tile_first, tile_next, n_tiles, tile):
    r_max, f = hid.shape
    dm = w_down.shape[2]
    tn = _pick(dm, (2048, 1024, 512, 256, 128))
    n_mt = r_max // tile
    clamp = lambda m, nt: jnp.minimum(m, nt[0] - 1)
    grid_spec = pltpu.PrefetchScalarGridSpec(
        num_scalar_prefetch=4,
        grid=(dm // tn, n_mt),
        in_specs=[pl.BlockSpec((tile, f), lambda ni, m, te, tfst, nx, nt: (clamp(m, nt), 0)),
                  pl.BlockSpec(memory_space=pl.ANY)],
        out_specs=pl.BlockSpec((tile, tn), lambda ni, m, te, tfst, nx, nt: (clamp(m, nt), ni)),
        scratch_shapes=[pltpu.VMEM((f, tn), BF16), pltpu.VMEM((2, 1, f, tn), F32),
                        pltpu.SemaphoreType.DMA((2,)), pltpu.SMEM((1,), I32)],
    )
    return pl.pallas_call(
        _expert_down_body,
        grid_spec=grid_spec,
        out_shape=jax.ShapeDtypeStruct((r_max, dm), F32),
        compiler_params=_params("arbitrary", "arbitrary"),
        name="expert_down",
    )(tile_expert, tile_first, tile_next, n_tiles, hid, w_down)


def _shared_up_body(x_ref, wg_ref, wu_ref, o_ref):
    x = x_ref[...]
    g = jnp.dot(x, wg_ref[...], preferred_element_type=F32)
    u = jnp.dot(x, wu_ref[...], preferred_element_type=F32)
    o_ref[...] = (jax.nn.silu(g) * u).astype(o_ref.dtype)


def _shared_up(x_bf, wg, wu):
    m, dm = x_bf.shape
    f = wg.shape[1]
    tm = _pick(m, (1408, 1024, 768, 640, 512, 256))
    tf = _pick(f, (256, 128))
    return pl.pallas_call(
        _shared_up_body,
        grid=(m // tm, f // tf),
        in_specs=[pl.BlockSpec((tm, dm), lambda i, j: (i, 0)),
                  pl.BlockSpec((dm, tf), lambda i, j: (0, j)),
                  pl.BlockSpec((dm, tf), lambda i, j: (0, j))],
        out_specs=pl.BlockSpec((tm, tf), lambda i, j: (i, j)),
        out_shape=jax.ShapeDtypeStruct((m, f), BF16),
        compiler_params=_params("arbitrary", "arbitrary"),
        name="shared_up",
    )(x_bf, wg, wu)


def _combine_body(pos_ref, gate_ref, ys_hbm, x_ref, sh_ref, wsd_ref, g_ref, b_ref, oh_ref, ot_ref,
                  idx_smem, buf_ref, idx_sem, sems, *, tb, top_k, alpha, eps, head_blocks):
    i = pl.program_id(0)
    n = tb * top_k
    dm = x_ref.shape[1]

    def start(t):
        slot = t % 2
        cp = pltpu.make_async_copy(pos_ref.at[t], idx_smem.at[pl.ds(slot, 1)], idx_sem)
        cp.start()
        cp.wait()

        def issue(g, carry):
            for u in range(DMA_THREADS):
                r = g * DMA_THREADS + u
                pltpu.make_async_copy(ys_hbm.at[pl.ds(idx_smem[slot, r], 1), :],
                                      buf_ref.at[slot, pl.ds(r, 1), :], sems.at[slot]).start(priority=u)
            return carry
        lax.fori_loop(0, n // DMA_THREADS, issue, 0)

    @pl.when(i == 0)
    def _():
        start(0)

    @pl.when(i + 1 < pl.num_programs(0))
    def _():
        start(i + 1)

    shared = jnp.dot(sh_ref[...], wsd_ref[...], preferred_element_type=F32)
    slot = i % 2
    pltpu.make_async_copy(buf_ref.at[slot], buf_ref.at[slot], sems.at[slot]).wait()
    routed = shared
    for k in range(top_k):
        gate = jnp.concatenate([gate_ref[k * tb:(k + 1) * tb, :]] * (dm // LANE), axis=1)
        routed = routed + gate * buf_ref[slot, k * tb:(k + 1) * tb, :]
    out = _layer_norm(F32(alpha) * x_ref[...] + routed, g_ref[...], b_ref[...], eps)

    @pl.when(i < head_blocks)
    def _():
        oh_ref[...] = out

    @pl.when(i >= head_blocks)
    def _():
        ot_ref[...] = out


def _combine(pos, gate_rows, ys, x, sh_hid, ws_down, g, b, alpha, tb, n_head, cfg):
    m, dm = x.shape
    f = sh_hid.shape[1]
    n = tb * cfg.top_k
    nblk = m // tb
    assert n_head % tb == 0 and 0 < n_head < m
    hb = n_head // tb
    return pl.pallas_call(
        functools.partial(_combine_body, tb=tb, top_k=cfg.top_k, alpha=alpha, eps=cfg.ln_eps, head_blocks=hb),
        grid=(nblk,),
        in_specs=[pl.BlockSpec((nblk, 1, n), lambda i: (0, 0, 0)),
                  pl.BlockSpec((n, LANE), lambda i: (i, 0)),
                  pl.BlockSpec(memory_space=pl.ANY),
                  pl.BlockSpec((tb, dm), lambda i: (i, 0)),
                  pl.BlockSpec((tb, f), lambda i: (i, 0)),
                  pl.BlockSpec((f, dm), lambda i: (0, 0)),
                  pl.BlockSpec((1, dm), lambda i: (0, 0)),
                  pl.BlockSpec((1, dm), lambda i: (0, 0))],
        out_specs=[pl.BlockSpec((tb, dm), lambda i: (jnp.minimum(i, hb - 1), 0)),
                   pl.BlockSpec((tb, dm), lambda i: (jnp.maximum(i - hb, 0), 0))],
        out_shape=[jax.ShapeDtypeStruct((n_head, dm), F32), jax.ShapeDtypeStruct((m - n_head, dm), F32)],
        scratch_shapes=[pltpu.SMEM((2, n), I32), pltpu.VMEM((2, n, dm), F32),
                        pltpu.SemaphoreType.DMA(()), pltpu.SemaphoreType.DMA((2,))],
        compiler_params=_params("arbitrary"),
        name="moe_combine",
    )(pos, gate_rows, ys, x, sh_hid, ws_down, g, b)


def _routing_tables(ids, gates, ranks, counts, n_tok, tile, tb, tb_d, cfg):
    e, k = cfg.n_experts, cfg.top_k
    t_all = ids.shape[1]
    n = tb_d * k
    r_max = ((n_tok * k + e * (tile - 1)) // tile + 1) * tile
    counts = counts[:, 0].astype(I32)
    padded = ((counts + tile - 1) // tile) * tile
    ends = jnp.cumsum(padded)
    start_pad = ends - padded
    ar = jnp.arange(e, dtype=I32)
    eye = ids[:, :, None] == ar[None, None, :]
    dest = jnp.sum(jnp.where(eye, start_pad[None, None, :], 0), axis=2) + ranks
    tok = jnp.broadcast_to(jnp.arange(t_all, dtype=I32)[None, :], (k, t_all))
    real = tok < n_tok
    blocks = lambda a, b: a.reshape(k, t_all // b, b).transpose(1, 0, 2).reshape(t_all // b, 1, b * k)
    pos = blocks(jnp.where(real, dest, 0), tb)
    gate_rows = jnp.broadcast_to(blocks(jnp.where(real, gates, 0.0), tb).reshape(-1, 1), (t_all * k, LANE))
    spare_tok = r_max + (tok - n_tok) * k + jnp.arange(k, dtype=I32)[:, None]
    dst = blocks(jnp.where(real, dest, spare_tok), tb_d)
    gap = padded - counts
    gap_end = jnp.cumsum(gap)
    n_fill = gap_end[-1:]
    n_fill_max = -(-(e * (tile - 1)) // n) * n
    j = jnp.arange(n_fill_max, dtype=I32)
    owner = jnp.minimum(jnp.sum((gap_end[None, :] <= j[:, None]).astype(I32), axis=1), e - 1)
    own = owner[:, None] == ar[None, :]
    pick = lambda v: jnp.sum(jnp.where(own, v[None, :], 0), axis=1)
    fill_row = pick(start_pad + counts) + (j - pick(gap_end - gap))
    spare_fill = r_max + (t_all - n_tok) * k + (j - n_fill[0]) % n
    fill = jnp.where(j < n_fill[0], fill_row, spare_fill).reshape(n_fill_max // n, 1, n)
    rows_total = r_max + (t_all - n_tok) * k + n
    n_rows = ends[-1:]
    tile_start = jnp.arange(r_max // tile, dtype=I32) * tile
    tile_expert = jnp.minimum(jnp.sum((ends[None, :] <= tile_start[:, None]).astype(I32), axis=1), e - 1)
    prev = jnp.concatenate([jnp.full((1,), -1, I32), tile_expert[:-1]])
    tile_first = (tile_expert != prev).astype(I32)
    later = jnp.where((padded > 0)[None, :] & (ar[None, :] > ar[:, None]), ar[None, :], e)
    next_expert = jnp.min(later, axis=1)
    next_expert = jnp.where(next_expert == e, -1, next_expert)
    tile_next = jnp.sum(jnp.where(tile_expert[:, None] == ar[None, :], next_expert[None, :], 0), axis=1)
    tiles = (n_rows // tile, tile_expert, tile_first, tile_next)
    return dst, fill, n_fill, rows_total, r_max, pos, gate_rows, tiles


def _layer(cfg, l, x_all, n_p, seq, n_s, past, cache_kv, cache_idx_k, page_table, state_conv, state_h,
           w_in, ik_g, ik_b, conv_w, conv_b, lru_w_a, lru_b_a, lru_w_x, lru_b_x, lru_lambda,
           w_o_attn, w_o_lru, w_out, ln1_g, ln1_b, w_router, b_router,
           w_gate, w_up, w_down, ws_gate, ws_up, ws_down, ln2_g, ln2_b):
    t_all, dm = x_all.shape
    t_p = n_p * seq
    d, di = cfg.head_dim, cfg.idx_dim
    qw, kvw, iw = cfg.n_heads * d, cfg.n_kv_heads * d, cfg.idx_heads * di
    lw = conv_w.shape[1]
    alpha = (2.0 * cfg.depth) ** 0.25
    c_qkv = qw + 2 * kvw
    c_i = c_qkv + iw + di
    c_w = c_i + cfg.idx_heads

    x_bf = x_all.astype(BF16)
    w_in_bf = w_in.astype(BF16)
    z_qkv = _matmul(x_bf, w_in_bf, 0, c_qkv, name="in_proj_qkv")
    z_i = _matmul(x_bf, w_in_bf, c_qkv, iw, name="in_proj_idx_q")
    z_kw = _matmul(x_bf, w_in_bf, c_qkv + iw, 2 * LANE, name="in_proj_idx_kw")
    z_b = _matmul(x_bf, w_in_bf[:, c_w:], name="in_proj_lru_gates")

    cos_p, sin_p = _rope_tables(jnp.arange(seq), d, cfg.rope_theta)
    cos_1, sin_1 = _rope_tables(jnp.full((1,), past), d, cfg.rope_theta)
    cosf = jnp.concatenate([cos_p, jnp.broadcast_to(cos_1, (t_all - t_p, d))], axis=0)
    sinf = jnp.concatenate([sin_p, jnp.broadcast_to(sin_1, (t_all - t_p, d))], axis=0)
    kv_prompt, kv_tail, ki_all, k_bf, ki_bf = _finalize_keys(
        z_qkv, z_kw, cosf, sinf, ik_g.reshape(1, di), ik_b.reshape(1, di), n_p, seq, cfg)
    v_t = _transpose_values(z_qkv, t_p, cfg)

    o_attn_p = _prompt_attention(z_qkv, z_i, z_kw, cosf, sinf, k_bf, ki_bf, v_t, n_p, seq, cfg)
    wa_bf, wx_bf = lru_w_a.astype(BF16), lru_w_x.astype(BF16)
    row = lambda v: v.reshape(1, -1)
    h_lru_p, h_last_p = _lru_prompt(z_b, conv_w, row(conv_b), wa_bf, wx_bf, row(lru_b_a), row(lru_b_x),
                                    row(lru_lambda), n_p, seq, cfg)

    sl = slice(t_p, t_p + n_s)
    cos_s, sin_s = jnp.broadcast_to(cos_1[None], (n_s, 1, d)), jnp.broadcast_to(sin_1[None], (n_s, 1, d))
    qi_s = z_i[sl].reshape(n_s, cfg.idx_heads, di)
    w_rep = jnp.broadcast_to(z_kw[sl, di:di + cfg.idx_heads][:, :, None], (n_s, cfg.idx_heads, LANE))
    n_phys = cache_idx_k.shape[0]
    scores = _sample_scores(page_table, qi_s, w_rep, ki_all[sl][:, None, :], cos_s, sin_s, cache_idx_k, cfg)
    q_s = z_qkv[sl, :qw].reshape(n_s, cfg.n_heads, d)
    kv_new = kv_tail[:n_s].reshape(n_s, 2 * cfg.n_kv_heads, d)
    o_attn_s = _sample_attend(page_table, scores, q_s, kv_new, cos_s, sin_s,
                              cache_kv.reshape(n_phys * cfg.page_size * 2 * cfg.n_kv_heads, d), cfg)
    xl_s = z_b[sl, :lw]
    hist_s = jnp.moveaxis(state_conv, 1, 0)
    h_s, h_s_bf = _lru_sample(xl_s, hist_s, state_h, conv_w, row(conv_b), wa_bf, wx_bf,
                              row(lru_b_a), row(lru_b_x), row(lru_lambda), cfg)

    pad = t_all - t_p - n_s
    o_attn = jnp.concatenate([o_attn_p, o_attn_s.reshape(n_s, qw), jnp.zeros((pad, qw), BF16)], axis=0)
    h_lru = jnp.concatenate([h_lru_p, h_s_bf, jnp.zeros((pad, lw), BF16)], axis=0)

    merged = _merge(o_attn, h_lru, w_o_attn.astype(BF16), w_o_lru.astype(BF16), z_b, lw, cfg)
    x1, x1_bf, x1_pk = _proj_residual_ln(merged, w_out.astype(BF16), x_all, row(ln1_g), row(ln1_b), alpha, cfg)

    n_tok = t_p + n_s
    ids, gates, ranks, counts = _router(x1_bf, w_router.T.astype(BF16), b_router.reshape(-1, 1), n_tok, cfg)
    tile = cfg.moe_tile
    tb = _pick(t_all, (64, 32, 16, 8))
    tb_d = _pick(t_all, (128, 64, 32, 16, 8))
    dst, fill, n_fill, rows_total, r_max, pos, gate_rows, (n_tiles, tile_expert, tile_first, tile_next) = \
        _routing_tables(ids, gates, ranks, counts, n_tok, tile, tb, tb_d, cfg)
    xs = _dispatch(x1_pk, dst, fill, n_fill, rows_total, tb_d, cfg)
    hid = _expert_up(xs, r_max, w_gate, w_up, tile_expert, tile_first, tile_next, n_tiles, tile)
    ys = _expert_down(hid, w_down, tile_expert, tile_first, tile_next, n_tiles, tile)
    sh_hid = _shared_up(x1_bf, ws_gate.astype(BF16), ws_up.astype(BF16))
    y_prompt, y_tail = _combine(pos, gate_rows, ys, x1, sh_hid, ws_down.astype(BF16), row(ln2_g), row(ln2_b),
                                alpha, tb, t_p, cfg)

    keep = cfg.conv_width - 1
    conv_p = jnp.stack([z_b[(n + 1) * seq - keep:(n + 1) * seq, :lw] for n in range(n_p)], axis=0)
    conv_s = jnp.concatenate([state_conv[:, 1:, :], xl_s[:, None, :]], axis=1)
    outs = dict(
        kv_p=kv_prompt.reshape(n_p, seq, 2, cfg.n_kv_heads, d),
        ik_p=ki_all[:t_p].reshape(n_p, seq, di),
        cv_p=conv_p, h_p=h_last_p.reshape(n_p, lw),
        kv_s=kv_tail[:n_s].reshape(n_s, 1, 2, cfg.n_kv_heads, d),
        ik_s=ki_all[sl].reshape(n_s, 1, di),
        cv_s=conv_s, h_s=h_s)
    return y_prompt, y_tail[:n_s], outs


def _forward(cfg, x_prompt, x_sample, cache_kv, cache_idx_k, page_table, state_conv, state_h,
             w_in, idx_k_norm_g, idx_k_norm_b, conv_w, conv_b, lru_w_a, lru_b_a, lru_w_x, lru_b_x,
             lru_lambda, w_o_attn, w_o_lru, w_out, ln1_g, ln1_b, w_router, b_router,
             w_gate, w_up, w_down, ws_gate, ws_up, ws_down, ln2_g, ln2_b):
    n_p, seq, dm = x_prompt.shape
    n_s, t_s, _ = x_sample.shape
    assert t_s == 1 and cfg.depth == 1 and w_in.shape[0] == 1
    past = page_table.shape[1] * cfg.page_size
    t_p = n_p * seq
    t_all = -(-(t_p + n_s) // cfg.row_align) * cfg.row_align
    pad = t_all - t_p - n_s
    x_all = jnp.concatenate([x_prompt.reshape(t_p, dm), x_sample.reshape(n_s, dm),
                             jnp.zeros((pad, dm), x_prompt.dtype)], axis=0)
    l = 0
    y_p, y_s, o = _layer(cfg, l, x_all, n_p, seq, n_s, past, cache_kv[l], cache_idx_k[l], page_table,
                      state_conv[l], state_h[l], w_in[l], idx_k_norm_g[l], idx_k_norm_b[l],
                      conv_w[l], conv_b[l], lru_w_a[l], lru_b_a[l], lru_w_x[l], lru_b_x[l], lru_lambda[l],
                      w_o_attn[l], w_o_lru[l], w_out[l], ln1_g[l], ln1_b[l], w_router[l], b_router[l],
                      w_gate[l], w_up[l], w_down[l], ws_gate[l], ws_up[l], ws_down[l], ln2_g[l], ln2_b[l])
    lead = lambda a: a[None]
    return (y_p.reshape(n_p, seq, dm), y_s.reshape(n_s, 1, dm), lead(o["kv_p"]), lead(o["ik_p"]), lead(o["cv_p"]), lead(o["h_p"]),
            lead(o["kv_s"]), lead(o["ik_s"]), lead(o["cv_s"]), lead(o["h_s"]))


def kernel(x_prompt, x_sample, cache_kv, cache_idx_k, page_table, state_conv, state_h, w_in, idx_k_norm_g, idx_k_norm_b, conv_w, conv_b, lru_w_a, lru_b_a, lru_w_x, lru_b_x, lru_lambda, w_o_attn, w_o_lru, w_out, ln1_g, ln1_b, w_router, b_router, w_gate, w_up, w_down, ws_gate, ws_up, ws_down, ln2_g, ln2_b):
    return _forward(Cfg(), x_prompt, x_sample, cache_kv, cache_idx_k, page_table, state_conv, state_h,
                    w_in, idx_k_norm_g, idx_k_norm_b, conv_w, conv_b, lru_w_a, lru_b_a, lru_w_x, lru_b_x,
                    lru_lambda, w_o_attn, w_o_lru, w_out, ln1_g, ln1_b, w_router, b_router,
                    w_gate, w_up, w_down, ws_gate, ws_up, ws_down, ln2_g, ln2_b)
```
